```python
import jax, jax.numpy as jnp
from jax import lax
import numpy as np

D_MODEL = 1024
BATCH = 1
SEQ = 16384
DEPTH = 1

N_META = 16
EPS = 1e-6
ROPE_THETA = 10000.0
GLA_HEADS = 4
GLA_DK = 64
GLA_DV = 128
GLA_LOWRANK = 16
GLA_GATE_NORM = 16.0
GLA_CHUNK = 64
DSA_HEADS = 8
DSA_HD = 64
IDX_HEADS = 8
IDX_HD = 64
TOPK_MAX = 256
Q_BLOCK = 128
D_FF = 2816

GLA_WIDTH = GLA_HEADS * GLA_DV
DSA_WIDTH = DSA_HEADS * DSA_HD
MIX_WIDTH = GLA_WIDTH + DSA_WIDTH
IN_SIZES = (GLA_HEADS * GLA_DK, GLA_HEADS * GLA_DK, GLA_WIDTH, GLA_WIDTH, GLA_LOWRANK,
            DSA_WIDTH, DSA_WIDTH, DSA_WIDTH, IDX_HEADS * IDX_HD, IDX_HD, IDX_HEADS)
IN_COLS = sum(IN_SIZES)

kernel_name = "hymba_gla_dsa_macaron"


def rmsnorm(x, g):
    x32 = x.astype(jnp.float32)
    y = x32 * lax.rsqrt(jnp.mean(x32 * x32, axis=-1, keepdims=True) + EPS)
    return (y * g.astype(jnp.float32)).astype(x.dtype)


def swiglu(h, w_gate, w_up, w_down):
    return (jax.nn.silu(h @ w_gate) * (h @ w_up)) @ w_down


def rope_tables(L, dim):
    pos = jnp.arange(L, dtype=jnp.float32)
    inv_freq = 1.0 / (ROPE_THETA ** (jnp.arange(0, dim, 2, dtype=jnp.float32) / dim))
    ang = pos[:, None] * inv_freq[None, :]
    ang = jnp.concatenate([ang, ang], axis=-1)
    return jnp.cos(ang), jnp.sin(ang)


def apply_rope(x, cos, sin):
    half = x.shape[-1] // 2
    rot = jnp.concatenate([-x[..., half:], x[..., :half]], axis=-1)
    c = cos[None, :, None, :].astype(x.dtype)
    s = sin[None, :, None, :].astype(x.dtype)
    return x * c + rot * s


def gla_mixer(q, k, v, r, a_low, w_a2, b_a, head_norm):
    B, L, _ = q.shape
    dt = q.dtype
    f32 = jnp.float32
    q = q.reshape(B, L, GLA_HEADS, GLA_DK).astype(f32) * (GLA_DK ** -0.5)
    k = k.reshape(B, L, GLA_HEADS, GLA_DK).astype(f32)
    v = v.reshape(B, L, GLA_HEADS, GLA_DV).astype(f32)
    g = jax.nn.log_sigmoid((a_low @ w_a2 + b_a).astype(f32)) / GLA_GATE_NORM
    g = g.reshape(B, L, GLA_HEADS, GLA_DK)
    pad = GLA_CHUNK - N_META

    def to_chunks(t):
        t = jnp.pad(t, ((0, 0), (pad, 0), (0, 0), (0, 0)))
        nc = t.shape[1] // GLA_CHUNK
        return t.reshape(B, nc, GLA_CHUNK, GLA_HEADS, t.shape[-1]).transpose(1, 0, 3, 2, 4)

    qc, kc, vc, gc = to_chunks(q), to_chunks(k), to_chunks(v), to_chunks(g)
    causal = jnp.tril(jnp.ones((GLA_CHUNK, GLA_CHUNK), dtype=bool))

    def step(S, inp):
        qb, kb, vb, gb = inp
        b = jnp.cumsum(gb, axis=2)
        o_inter = jnp.einsum('bhcd,bhde->bhce', qb * jnp.exp(b), S)
        diff = b[:, :, :, None, :] - b[:, :, None, :, :]
        decay = jnp.exp(jnp.where(causal[None, None, :, :, None], diff, -jnp.inf))
        A = jnp.einsum('bhid,bhjd,bhijd->bhij', qb, kb, decay)
        o_intra = jnp.einsum('bhij,bhje->bhie', A, vb)
        b_last = b[:, :, -1:, :]
        S_new = jnp.exp(b_last[:, :, 0, :])[..., None] * S + \
            jnp.einsum('bhjd,bhje->bhde', kb * jnp.exp(b_last - b), vb)
        return S_new, o_inter + o_intra

    S0 = jnp.zeros((B, GLA_HEADS, GLA_DK, GLA_DV), f32)
    _, oc = lax.scan(step, S0, (qc, kc, vc, gc))
    o = oc.transpose(1, 0, 3, 2, 4).reshape(B, -1, GLA_HEADS, GLA_DV)[:, pad:]
    o = rmsnorm(o, head_norm)
    o = o.reshape(B, L, GLA_WIDTH) * jax.nn.silu(r.astype(f32))
    return o.astype(dt)


def dsa_mixer(q, k, v, iq, ik, iw, cos, sin, k_sel):
    B, L, _ = q.shape
    f32 = jnp.float32
    q = apply_rope(q.reshape(B, L, DSA_HEADS, DSA_HD), cos, sin)
    k = apply_rope(k.reshape(B, L, DSA_HEADS, DSA_HD), cos, sin)
    v = v.reshape(B, L, DSA_HEADS, DSA_HD)
    iq = apply_rope(iq.reshape(B, L, IDX_HEADS, IDX_HD), cos, sin)
    ik = apply_rope(ik.reshape(B, L, 1, IDX_HD), cos, sin)[:, :, 0]
    iw = iw * (IDX_HEADS ** -0.5)
    n_blk = -(-L // Q_BLOCK)
    Lp = n_blk * Q_BLOCK

    def to_blocks(t):
        t = jnp.pad(t, ((0, 0), (0, Lp - L)) + ((0, 0),) * (t.ndim - 2))
        t = t.reshape((B, n_blk, Q_BLOCK) + t.shape[2:])
        return jnp.moveaxis(t, 1, 0)

    key_pos = jnp.arange(L)
    q_pos = jnp.arange(Lp).reshape(n_blk, Q_BLOCK)

    def block(args):
        qb, iqb, iwb, tq = args
        idx_logits = jnp.einsum('bqhd,bsd->bqhs', iqb, ik) * (IDX_HD ** -0.5)
        score = jnp.einsum('bqh,bqhs->bqs', iwb, jax.nn.relu(idx_logits)).astype(f32)
        visible = key_pos[None, :] <= tq[:, None]
        score = jnp.where(visible[None], score, -jnp.inf)
        _, idx = lax.top_k(score, k_sel)
        ks = jax.vmap(lambda kk, ii: kk[ii])(k, idx)
        vs = jax.vmap(lambda vv, ii: vv[ii])(v, idx)
        logits = jnp.einsum('bqhd,bqkhd->bhqk', qb, ks).astype(f32) * (DSA_HD ** -0.5)
        valid = idx <= tq[None, :, None]
        logits = jnp.where(valid[:, None], logits, -jnp.inf)
        p = jax.nn.softmax(logits, axis=-1).astype(vs.dtype)
        return jnp.einsum('bhqk,bqkhd->bqhd', p, vs)

    out = lax.map(block, (to_blocks(q), to_blocks(iq), to_blocks(iw), q_pos))
    out = jnp.moveaxis(out, 0, 1).reshape(B, Lp, DSA_WIDTH)[:, :L]
    return out


def token_mixing(hn, w_in, gla_w_a2, gla_b_a, gla_head_norm, w_out, cos, sin, k_sel):
    proj = hn @ w_in
    split_points = np.cumsum(IN_SIZES)[:-1].tolist()
    gq, gk, gv, gr, ga, dq, dk, dv, iq, ik, iw = jnp.split(proj, split_points, axis=-1)
    o_gla = gla_mixer(gq, gk, gv, gr, ga, gla_w_a2, gla_b_a, gla_head_norm)
    o_dsa = dsa_mixer(dq, dk, dv, iq, ik, iw, cos, sin, k_sel)
    return jnp.concatenate([o_gla, o_dsa], axis=-1) @ w_out


def setup_inputs(seed: int = 0) -> dict:
    key = jax.random.key(seed)
    ks = jax.random.split(key, 20)
    f32 = jnp.float32

    def w(k, shape, fan_in):
        return jax.random.normal(k, shape, f32) * (fan_in ** -0.5)

    def gain(k, shape):
        return 1.0 + 0.02 * jax.random.normal(k, shape, f32)

    return {
        "x": jax.random.normal(ks[0], (BATCH, SEQ, D_MODEL), f32),
        "meta_tokens": jax.random.normal(ks[1], (N_META, D_MODEL), f32),
        "ffn1_norm": gain(ks[2], (DEPTH, D_MODEL)),
        "ffn1_w_gate": w(ks[3], (DEPTH, D_MODEL, D_FF), D_MODEL),
        "ffn1_w_up": w(ks[4], (DEPTH, D_MODEL, D_FF), D_MODEL),
        "ffn1_w_down": w(ks[5], (DEPTH, D_FF, D_MODEL), D_FF),
        "mix_norm": gain(ks[6], (DEPTH, D_MODEL)),
        "w_in": w(ks[7], (DEPTH, D_MODEL, IN_COLS), D_MODEL),
        "gla_w_a2": w(ks[8], (DEPTH, GLA_LOWRANK, GLA_HEADS * GLA_DK), GLA_LOWRANK),
        "gla_b_a": 0.01 * jax.random.normal(ks[9], (DEPTH, GLA_HEADS * GLA_DK), f32),
        "gla_head_norm": gain(ks[10], (DEPTH, GLA_DV)),
        "w_out": w(ks[11], (DEPTH, MIX_WIDTH, D_MODEL), MIX_WIDTH),
        "ffn2_norm": gain(ks[12], (DEPTH, D_MODEL)),
        "ffn2_w_gate": w(ks[13], (DEPTH, D_MODEL, D_FF), D_MODEL),
        "ffn2_w_up": w(ks[14], (DEPTH, D_MODEL, D_FF), D_MODEL),
        "ffn2_w_down": w(ks[15], (DEPTH, D_FF, D_MODEL), D_FF),
        "final_norm": gain(ks[16], (D_MODEL,)),
    }


def reference(x, meta_tokens, ffn1_norm, ffn1_w_gate, ffn1_w_up, ffn1_w_down, mix_norm, w_in,
              gla_w_a2, gla_b_a, gla_head_norm, w_out, ffn2_norm, ffn2_w_gate, ffn2_w_up,
              ffn2_w_down, final_norm):
    B, S, _ = x.shape
    k_sel = min(TOPK_MAX, S // 4)
    meta = jnp.broadcast_to(meta_tokens[None].astype(x.dtype), (B, N_META, D_MODEL))
    h = jnp.concatenate([meta, x], axis=1)
    L = h.shape[1]
    cos, sin = rope_tables(L, DSA_HD)
    for layer in range(DEPTH):
        h = h + 0.5 * swiglu(rmsnorm(h, ffn1_norm[layer]), ffn1_w_gate[layer], ffn1_w_up[layer], ffn1_w_down[layer])
        h = h + token_mixing(rmsnorm(h, mix_norm[layer]), w_in[layer], gla_w_a2[layer], gla_b_a[layer],
                             gla_head_norm[layer], w_out[layer], cos, sin, k_sel)
        h = h + 0.5 * swiglu(rmsnorm(h, ffn2_norm[layer]), ffn2_w_gate[layer], ffn2_w_up[layer], ffn2_w_down[layer])
    h = rmsnorm(h, final_norm)
    return h[:, N_META:]
```

```python
import functools

import numpy as np
import jax
import jax.numpy as jnp
from jax import lax
from jax.experimental import pallas as pl
from jax.experimental.pallas import tpu as pltpu

N_META = 16
EPS = 1e-6
ROPE_THETA = 10000.0
GLA_HEADS = 4
GLA_DK = 64
GLA_DV = 128
GLA_LOWRANK = 16
GLA_GATE_NORM = 16.0
DSA_HEADS = 8
DSA_HD = 64
IDX_HEADS = 8
IDX_HD = 64
TOPK_MAX = 256

GLA_QK = GLA_HEADS * GLA_DK
GLA_WIDTH = GLA_HEADS * GLA_DV
DSA_WIDTH = DSA_HEADS * DSA_HD
IDX_WIDTH = IDX_HEADS * IDX_HD

LANES = 128
ROW_TILE = 384
QBLK = 128
GLA_STEP = 128
GLA_SUB = 16
VMEM_LIMIT = 60 * 1024 * 1024
NEG_BIG = -1e30

MISC_IK = 0
MISC_IW = IDX_HD
MISC_GA = IDX_HD + IDX_HEADS

F32 = jnp.float32
BF16 = jnp.bfloat16


def _dot(a, b):
    return jnp.dot(a, b, preferred_element_type=F32)


def _dot_nt(a, b):
    return lax.dot_general(a, b, (((1,), (1,)), ((), ())), preferred_element_type=F32)


def _rmsnorm(x, g):
    return x * lax.rsqrt(jnp.mean(x * x, axis=-1, keepdims=True) + EPS) * g


def _const_spec(shape):
    return pl.BlockSpec(shape, lambda *_: (0,) * len(shape), pipeline_mode=pl.Buffered(1))


def _swiglu_half(x, norm_ref, wg_ref, wu_ref, wd_ref, ff_chunk):
    hn = _rmsnorm(x, norm_ref[...]).astype(BF16)
    d_ff = wg_ref.shape[1]
    acc = jnp.zeros(x.shape, F32)
    for c in range(d_ff // ff_chunk):
        sl = slice(c * ff_chunk, (c + 1) * ff_chunk)
        g = _dot(hn, wg_ref[:, sl])
        u = _dot(hn, wu_ref[:, sl])
        a = (g * jax.nn.sigmoid(g) * u).astype(BF16)
        acc = acc + _dot(a, wd_ref[sl, :])
    return x + 0.5 * acc


def _ffn1_kernel(h_ref, norm_ref, wg_ref, wu_ref, wd_ref, o_ref, *, ff_chunk):
    o_ref[...] = _swiglu_half(h_ref[...], norm_ref, wg_ref, wu_ref, wd_ref, ff_chunk)


def _ffn_chunk(d_ff):
    for c in (1408, 1024, 512, 256, 128):
        if d_ff % c == 0:
            return c
    return d_ff


def _ffn1(h, norm, wg, wu, wd):
    lp, d = h.shape
    d_ff = wg.shape[1]
    row = pl.BlockSpec((ROW_TILE, d), lambda i: (i, 0))
    return pl.pallas_call(
        functools.partial(_ffn1_kernel, ff_chunk=_ffn_chunk(d_ff)),
        grid=(lp // ROW_TILE,),
        in_specs=[row, _const_spec((1, d)), _const_spec((d, d_ff)), _const_spec((d, d_ff)),
                  _const_spec((d_ff, d))],
        out_specs=row,
        out_shape=jax.ShapeDtypeStruct((lp, d), F32),
        compiler_params=pltpu.CompilerParams(dimension_semantics=("parallel",),
                                             vmem_limit_bytes=VMEM_LIMIT),
        name="ffn1",
    )(h, norm, wg, wu, wd)


def _rope(x, cos, sin_signed, first_half):
    parts = []
    for c in range(x.shape[1] // LANES):
        xc = x[:, c * LANES:(c + 1) * LANES]
        rot = jnp.where(first_half, pltpu.roll(xc, LANES - 32, 1), pltpu.roll(xc, 32, 1))
        parts.append(xc * cos + rot * sin_signed)
    return parts[0] if len(parts) == 1 else jnp.concatenate(parts, axis=1)


def _log_sigmoid(x):
    return jnp.minimum(x, 0.0) - jnp.log(1.0 + jnp.exp(-jnp.abs(x)))


def _in_proj_kernel(h_ref, norm_ref, w_ref, wa2_ref, ba_ref, cos_ref, sin_ref,
                    gqk_ref, gv_ref, gr_ref, gg_ref, dq_ref, dk_ref, dv_ref, iq_ref, ik_ref, iw_ref):
    hn = _rmsnorm(h_ref[...], norm_ref[...]).astype(BF16)
    cos = cos_ref[...]
    sin_s = sin_ref[...]
    lane = lax.broadcasted_iota(jnp.int32, cos.shape, 1)
    first_half = (lane % DSA_HD) < (DSA_HD // 2)

    def proj(c0, width):
        return _dot(hn, w_ref[:, c0:c0 + width])

    c = 0
    gqk_ref[...] = proj(c, 2 * GLA_QK); c += 2 * GLA_QK
    gv_ref[...] = proj(c, GLA_WIDTH); c += GLA_WIDTH
    gr_ref[...] = proj(c, GLA_WIDTH); c += GLA_WIDTH
    dq_ref[...] = (_rope(proj(c, DSA_WIDTH), cos, sin_s, first_half) * (DSA_HD ** -0.5)).astype(BF16)
    c += DSA_WIDTH
    dk_ref[...] = _rope(proj(c, DSA_WIDTH), cos, sin_s, first_half).astype(BF16); c += DSA_WIDTH
    dv_ref[...] = proj(c, DSA_WIDTH).astype(BF16); c += DSA_WIDTH
    iq_ref[...] = (_rope(proj(c, IDX_WIDTH), cos, sin_s, first_half) * (IDX_HD ** -0.5)).astype(BF16)
    c += IDX_WIDTH
    misc = proj(c, LANES)
    ik_ref[...] = _rope(misc, cos, sin_s, first_half)[:, MISC_IK:MISC_IK + IDX_HD].astype(BF16)
    iw_ref[...] = misc * (IDX_HEADS ** -0.5)
    pre = _dot(misc.astype(BF16), wa2_ref[...]) + ba_ref[...]
    gg_ref[...] = _log_sigmoid(pre) * (1.0 / GLA_GATE_NORM)


def _in_proj(h, norm, w_all, wa2p, ba, cos2, sin2s):
    lp, d = h.shape
    ncols = w_all.shape[1]

    def row(width):
        return pl.BlockSpec((ROW_TILE, width), lambda i: (i, 0))

    outs = [
        (2 * GLA_QK, F32), (GLA_WIDTH, F32), (GLA_WIDTH, F32), (GLA_QK, F32),
        (DSA_WIDTH, BF16), (DSA_WIDTH, BF16), (DSA_WIDTH, BF16), (IDX_WIDTH, BF16),
        (IDX_HD, BF16), (LANES, F32),
    ]
    return pl.pallas_call(
        _in_proj_kernel,
        grid=(lp // ROW_TILE,),
        in_specs=[row(d), _const_spec((1, d)), _const_spec((d, ncols)), _const_spec((LANES, GLA_QK)),
                  _const_spec((1, GLA_QK)), row(LANES), row(LANES)],
        out_specs=[row(w) for w, _ in outs],
        out_shape=[jax.ShapeDtypeStruct((lp, w), dt) for w, dt in outs],
        compiler_params=pltpu.CompilerParams(dimension_semantics=("parallel",),
                                             vmem_limit_bytes=VMEM_LIMIT),
        name="in_proj",
    )(h, norm, w_all, wa2p, ba, cos2, sin2s)


def _gla_kernel(qk_ref, v_ref, r_ref, g_ref, hnorm_ref, o_ref, s_ref):
    n_sub = GLA_STEP // GLA_SUB

    @pl.when(pl.program_id(0) == 0)
    def _():
        s_ref[...] = jnp.zeros(s_ref.shape, F32)

    q = qk_ref[:, :GLA_QK] * (GLA_DK ** -0.5)
    k = qk_ref[:, GLA_QK:]
    v = v_ref[...]
    g = g_ref[...]

    ri = lax.broadcasted_iota(jnp.int32, (GLA_STEP, GLA_STEP), 0)
    ci = lax.broadcasted_iota(jnp.int32, (GLA_STEP, GLA_STEP), 1)
    tri = ((ri // GLA_SUB == ci // GLA_SUB) & (ci <= ri)).astype(F32)
    b = jnp.dot(tri, g, preferred_element_type=F32, precision=lax.Precision.HIGHEST)

    b3 = b.reshape(n_sub, GLA_SUB, GLA_QK)
    k3 = k.reshape(n_sub, GLA_SUB, GLA_QK)
    v3 = v.reshape(n_sub, GLA_SUB, GLA_WIDTH)
    b_last = jnp.broadcast_to(b3[:, GLA_SUB - 1:GLA_SUB, :], b3.shape).reshape(GLA_STEP, GLA_QK)

    pos = lax.broadcasted_iota(jnp.int32, (GLA_STEP, GLA_QK), 0) % GLA_SUB
    t_rows = []
    for j in range(GLA_SUB):
        kj = jnp.broadcast_to(k3[:, j:j + 1, :], k3.shape).reshape(GLA_STEP, GLA_QK)
        bj = jnp.broadcast_to(b3[:, j:j + 1, :], b3.shape).reshape(GLA_STEP, GLA_QK)
        e = jnp.where(pos >= j, b - bj, -jnp.inf)
        t_rows.append((q * kj * jnp.exp(e)).astype(BF16))
    t = jnp.concatenate(t_rows, axis=0)
    hr = lax.broadcasted_iota(jnp.int32, (GLA_QK, GLA_WIDTH), 0) // GLA_DK
    hc = lax.broadcasted_iota(jnp.int32, (GLA_QK, GLA_WIDTH), 1) // GLA_DV
    head_sum = (hr == hc).astype(BF16)
    p = _dot(t, head_sum)
    o = jnp.zeros((GLA_STEP, GLA_WIDTH), F32)
    for j in range(GLA_SUB):
        vj = jnp.broadcast_to(v3[:, j:j + 1, :], v3.shape).reshape(GLA_STEP, GLA_WIDTH)
        o = o + p[j * GLA_STEP:(j + 1) * GLA_STEP, :] * vj

    qe = q * jnp.exp(b)
    kd = k * jnp.exp(b_last - b)
    kd_t = kd.T
    dec_t = jnp.exp(b).T
    lane_head = lax.broadcasted_iota(jnp.int32, (GLA_SUB, GLA_QK), 1) // GLA_DK
    s = s_ref[...]
    o_inter = []
    for c in range(n_sub):
        rows = slice(c * GLA_SUB, (c + 1) * GLA_SUB)
        qe_c = qe[rows, :]
        q_stack = jnp.concatenate(
            [jnp.where(lane_head == h, qe_c, 0.0) for h in range(GLA_HEADS)], axis=0).astype(BF16)
        r_stack = _dot(q_stack, s.astype(BF16))
        o_inter.append(jnp.concatenate(
            [r_stack[h * GLA_SUB:(h + 1) * GLA_SUB, :] for h in range(GLA_HEADS)], axis=1))
        u = _dot(kd_t[:, rows].astype(BF16), v[rows, :].astype(BF16))
        u_c = jnp.concatenate(
            [u[h * GLA_DK:(h + 1) * GLA_DK, h * GLA_DV:(h + 1) * GLA_DV] for h in range(GLA_HEADS)], axis=0)
        last = c * GLA_SUB + GLA_SUB - 1
        s = dec_t[:, last:last + 1] * s + u_c
    s_ref[...] = s
    o = o + jnp.concatenate(o_inter, axis=0)

    r = r_ref[...]
    gate = r * jax.nn.sigmoid(r)
    hn = hnorm_ref[...]
    outs = []
    for h in range(GLA_HEADS):
        oh = o[:, h * GLA_DV:(h + 1) * GLA_DV]
        outs.append(_rmsnorm(oh, hn))
    o_ref[...] = (jnp.concatenate(outs, axis=1) * gate).astype(o_ref.dtype)


def _gla(gqk, gv, gr, gg, hnorm):
    lp = gqk.shape[0]

    def row(width):
        return pl.BlockSpec((GLA_STEP, width), lambda i: (i, 0))

    return pl.pallas_call(
        _gla_kernel,
        grid=(lp // GLA_STEP,),
        in_specs=[row(2 * GLA_QK), row(GLA_WIDTH), row(GLA_WIDTH), row(GLA_QK), _const_spec((1, GLA_DV))],
        out_specs=row(GLA_WIDTH),
        out_shape=jax.ShapeDtypeStruct((lp, GLA_WIDTH), BF16),
        scratch_shapes=[pltpu.VMEM((GLA_QK, GLA_DV), F32)],
        compiler_params=pltpu.CompilerParams(dimension_semantics=("arbitrary",),
                                             vmem_limit_bytes=VMEM_LIMIT),
        name="gla",
    )(gqk, gv, gr, gg, hnorm)


def _key_to_float(u):
    bits = jnp.where(u >= 0, u, u ^ jnp.int32(0x7FFFFFFF))
    return lax.bitcast_convert_type(bits, F32)


_KEY_NEG_INF = np.int32(np.uint32(0xFF800000) ^ np.uint32(0x7FFFFFFF))
_INT_MIN = np.int32(-(2 ** 31))


def _dsa_kernel(q_ref, iq_ref, iw_ref, k_ref, v_ref, ik_ref, o_ref, sc_ref, *, k_sel):
    i = pl.program_id(0)
    nkb = i + 1
    row_t = i * QBLK + lax.broadcasted_iota(jnp.int32, (QBLK, QBLK), 0)
    lane_s = lax.broadcasted_iota(jnp.int32, (QBLK, QBLK), 1)

    iq = iq_ref[...]
    iq_stack = jnp.concatenate([iq[:, h * IDX_HD:(h + 1) * IDX_HD] for h in range(IDX_HEADS)], axis=0)
    iw = iw_ref[...]
    iw_cols = [iw[:, MISC_IW + h:MISC_IW + h + 1] for h in range(IDX_HEADS)]

    def score_body(j, carry):
        ks = pl.multiple_of(j * QBLK, QBLK)
        lg = jnp.maximum(_dot_nt(iq_stack, ik_ref[pl.ds(ks, QBLK), :]), 0.0)
        sc = iw_cols[0] * lg[0:QBLK, :]
        for h in range(1, IDX_HEADS):
            sc = sc + iw_cols[h] * lg[h * QBLK:(h + 1) * QBLK, :]
        visible = (ks + lane_s) <= row_t
        sc_ref[:, pl.ds(ks, QBLK)] = jnp.where(visible, sc, -jnp.inf)
        return carry

    lax.fori_loop(0, nkb, score_body, 0)

    def count_ge(cand):
        def body(j, acc):
            ks = pl.multiple_of(j * QBLK, QBLK)
            return acc + jnp.where(sc_ref[:, pl.ds(ks, QBLK)] >= cand, 1.0, 0.0)
        acc = lax.fori_loop(0, nkb, body, jnp.zeros((QBLK, QBLK), F32))
        return jnp.sum(acc, axis=1, keepdims=True)

    def bisect_body(it, u):
        bit = lax.shift_left(jnp.int32(1), 31 - it)
        cand_u = u + bit
        ok = count_ge(_key_to_float(cand_u)) >= float(k_sel)
        return jnp.where(ok, cand_u, u)

    u = lax.fori_loop(0, 32, bisect_body, jnp.full((QBLK, 1), _INT_MIN, jnp.int32))
    u = jnp.maximum(u, _KEY_NEG_INF)
    thr = _key_to_float(u)
    thr_next = _key_to_float(u + 1)
    n_up = count_ge(thr_next)
    budget = float(k_sel) - n_up

    tri = (lax.broadcasted_iota(jnp.int32, (QBLK, QBLK), 0) <= lane_s).astype(BF16)

    def bias_body(j, run):
        ks = pl.multiple_of(j * QBLK, QBLK)
        x = sc_ref[:, pl.ds(ks, QBLK)]
        above = x >= thr_next
        tie = jnp.logical_and(x >= thr, jnp.logical_not(above))
        tie_f = jnp.where(tie, 1.0, 0.0)
        rank = run + _dot(tie_f.astype(BF16), tri)
        keep = jnp.logical_or(above, jnp.logical_and(tie, rank <= budget))
        keep = jnp.logical_and(keep, (ks + lane_s) <= row_t)
        sc_ref[:, pl.ds(ks, QBLK)] = jnp.where(keep, 0.0, NEG_BIG)
        return run + jnp.sum(tie_f, axis=1, keepdims=True)

    lax.fori_loop(0, nkb, bias_body, jnp.zeros((QBLK, 1), F32))

    q = q_ref[...]
    q_heads = [q[:, h * DSA_HD:(h + 1) * DSA_HD] for h in range(DSA_HEADS)]

    def attn_body(j, carry):
        ms, ls, accs = carry
        ks = pl.multiple_of(j * QBLK, QBLK)
        bias = sc_ref[:, pl.ds(ks, QBLK)]
        kb = k_ref[pl.ds(ks, QBLK), :]
        vb = v_ref[pl.ds(ks, QBLK), :]
        new_m, new_l, new_acc = [], [], []
        for h in range(DSA_HEADS):
            hs = slice(h * DSA_HD, (h + 1) * DSA_HD)
            s = _dot_nt(q_heads[h], kb[:, hs]) + bias
            m_new = jnp.maximum(ms[h], jnp.max(s, axis=1, keepdims=True))
            alpha = jnp.exp(ms[h] - m_new)
            p = jnp.exp(s - m_new)
            new_m.append(m_new)
            new_l.append(alpha * ls[h] + jnp.sum(p, axis=1, keepdims=True))
            new_acc.append(alpha * accs[h] + _dot(p.astype(BF16), vb[:, hs]))
        return tuple(new_m), tuple(new_l), tuple(new_acc)

    init = (tuple(jnp.full((QBLK, 1), NEG_BIG, F32) for _ in range(DSA_HEADS)),
            tuple(jnp.zeros((QBLK, 1), F32) for _ in range(DSA_HEADS)),
            tuple(jnp.zeros((QBLK, DSA_HD), F32) for _ in range(DSA_HEADS)))
    _, ls, accs = lax.fori_loop(0, nkb, attn_body, init)
    o_ref[...] = jnp.concatenate([accs[h] / ls[h] for h in range(DSA_HEADS)], axis=1).astype(o_ref.dtype)


def _dsa(dq, dk, dv, iq, ik, iw, k_sel):
    lp = dq.shape[0]

    def row(width):
        return pl.BlockSpec((QBLK, width), lambda i: (i, 0))

    return pl.pallas_call(
        functools.partial(_dsa_kernel, k_sel=k_sel),
        grid=(lp // QBLK,),
        in_specs=[row(DSA_WIDTH), row(IDX_WIDTH), row(LANES), _const_spec((lp, DSA_WIDTH)),
                  _const_spec((lp, DSA_WIDTH)), _const_spec((lp, IDX_HD))],
        out_specs=row(DSA_WIDTH),
        out_shape=jax.ShapeDtypeStruct((lp, DSA_WIDTH), BF16),
        scratch_shapes=[pltpu.VMEM((QBLK, lp), F32)],
        compiler_params=pltpu.CompilerParams(dimension_semantics=("parallel",),
                                             vmem_limit_bytes=VMEM_LIMIT),
        name="dsa",
    )(dq, iq, iw, dk, dv, ik)


def _out_ffn2_kernel(h_ref, og_ref, od_ref, wo_ref, norm_ref, wg_ref, wu_ref, wd_ref, fnorm_ref, o_ref, *,
                     ff_chunk):
    h2 = h_ref[...] + _dot(og_ref[...], wo_ref[:GLA_WIDTH, :]) + _dot(od_ref[...], wo_ref[GLA_WIDTH:, :])
    h3 = _swiglu_half(h2, norm_ref, wg_ref, wu_ref, wd_ref, ff_chunk)
    o_ref[...] = _rmsnorm(h3, fnorm_ref[...])


def _out_ffn2(h, o_gla, o_dsa, wo, norm, wg, wu, wd, fnorm):
    lp, d = h.shape
    d_ff = wg.shape[1]

    def row(width):
        return pl.BlockSpec((ROW_TILE, width), lambda i: (i, 0))

    return pl.pallas_call(
        functools.partial(_out_ffn2_kernel, ff_chunk=_ffn_chunk(d_ff)),
        grid=(lp // ROW_TILE,),
        in_specs=[row(d), row(GLA_WIDTH), row(DSA_WIDTH), _const_spec((GLA_WIDTH + DSA_WIDTH, d)),
                  _const_spec((1, d)), _const_spec((d, d_ff)), _const_spec((d, d_ff)), _const_spec((d_ff, d)),
                  _const_spec((1, d))],
        out_specs=row(d),
        out_shape=jax.ShapeDtypeStruct((lp, d), F32),
        compiler_params=pltpu.CompilerParams(dimension_semantics=("parallel",),
                                             vmem_limit_bytes=VMEM_LIMIT),
        name="out_ffn2",
    )(h, o_gla, o_dsa, wo, norm, wg, wu, wd, fnorm)


def _rope_tables(lp):
    pos = jnp.arange(lp, dtype=F32)
    inv_freq = 1.0 / (ROPE_THETA ** (jnp.arange(0, DSA_HD, 2, dtype=F32) / DSA_HD))
    ang = pos[:, None] * inv_freq[None, :]
    cos = jnp.cos(ang)
    sin = jnp.sin(ang)
    cos2 = jnp.concatenate([cos, cos, cos, cos], axis=-1)
    sin2s = jnp.concatenate([-sin, sin, -sin, sin], axis=-1)
    return cos2, sin2s


def _layer(h, k_sel, ffn1_norm, ffn1_w_gate, ffn1_w_up, ffn1_w_down, mix_norm, w_in, gla_w_a2, gla_b_a,
           gla_head_norm, w_out, ffn2_norm, ffn2_w_gate, ffn2_w_up, ffn2_w_down, final_norm, cos2, sin2s):
    d = h.shape[1]
    h1 = _ffn1(h, ffn1_norm.reshape(1, d), ffn1_w_gate.astype(BF16), ffn1_w_up.astype(BF16),
               ffn1_w_down.astype(BF16))

    sizes = (GLA_QK, GLA_QK, GLA_WIDTH, GLA_WIDTH, GLA_LOWRANK, DSA_WIDTH, DSA_WIDTH, DSA_WIDTH, IDX_WIDTH,
             IDX_HD, IDX_HEADS)
    offs = np.concatenate([[0], np.cumsum(sizes)])
    col = lambda n: w_in[:, offs[n]:offs[n + 1]]
    pad = jnp.zeros((d, LANES - IDX_HD - IDX_HEADS - GLA_LOWRANK), w_in.dtype)
    w_all = jnp.concatenate([col(0), col(1), col(2), col(3), col(5), col(6), col(7), col(8),
                             col(9), col(10), col(4), pad], axis=1).astype(BF16)
    wa2p = jnp.zeros((LANES, GLA_QK), F32).at[MISC_GA:MISC_GA + GLA_LOWRANK, :].set(gla_w_a2).astype(BF16)

    gqk, gv, gr, gg, dq, dk, dv, iq, ik, iw = _in_proj(
        h1, mix_norm.reshape(1, d), w_all, wa2p, gla_b_a.reshape(1, GLA_QK), cos2, sin2s)
    o_gla = _gla(gqk, gv, gr, gg, gla_head_norm.reshape(1, GLA_DV))
    o_dsa = _dsa(dq, dk, dv, iq, ik, iw, k_sel)
    return _out_ffn2(h1, o_gla, o_dsa, w_out.astype(BF16), ffn2_norm.reshape(1, d), ffn2_w_gate.astype(BF16),
                     ffn2_w_up.astype(BF16), ffn2_w_down.astype(BF16), final_norm.reshape(1, d))


def kernel(x, meta_tokens, ffn1_norm, ffn1_w_gate, ffn1_w_up, ffn1_w_down, mix_norm, w_in, gla_w_a2, gla_b_a,
           gla_head_norm, w_out, ffn2_norm, ffn2_w_gate, ffn2_w_up, ffn2_w_down, final_norm):
    batch, seq, d = x.shape
    depth = ffn1_norm.shape[0]
    assert depth == 1, "the final norm is fused into the layer's last kernel"
    k_sel = min(TOPK_MAX, seq // 4)
    l_tok = seq + N_META
    lp = -(-l_tok // ROW_TILE) * ROW_TILE
    assert lp % QBLK == 0 and lp % GLA_STEP == 0
    cos2, sin2s = _rope_tables(lp)
    outs = []
    for bi in range(batch):
        h = jnp.concatenate([meta_tokens.astype(x.dtype), x[bi], jnp.zeros((lp - l_tok, d), x.dtype)], axis=0)
        h = _layer(h, k_sel, ffn1_norm[0], ffn1_w_gate[0], ffn1_w_up[0], ffn1_w_down[0], mix_norm[0], w_in[0],
                   gla_w_a2[0], gla_b_a[0], gla_head_norm[0], w_out[0], ffn2_norm[0], ffn2_w_gate[0],
                   ffn2_w_up[0], ffn2_w_down[0], final_norm, cos2, sin2s)
        outs.append(h[N_META:l_tok])
    return jnp.stack(outs, axis=0)
```

```python
import functools

import numpy as np
import jax
import jax.numpy as jnp
from jax import lax
from jax.experimental import pallas as pl
from jax.experimental.pallas import tpu as pltpu

N_META = 16
EPS = 1e-6
ROPE_THETA = 10000.0
GLA_HEADS = 4
GLA_DK = 64
GLA_DV = 128
GLA_LOWRANK = 16
GLA_GATE_NORM = 16.0
DSA_HEADS = 8
DSA_HD = 64
IDX_HEADS = 8
IDX_HD = 64
TOPK_MAX = 256

GLA_QK = GLA_HEADS * GLA_DK
GLA_WIDTH = GLA_HEADS * GLA_DV
DSA_WIDTH = DSA_HEADS * DSA_HD
IDX_WIDTH = IDX_HEADS * IDX_HD

LANES = 128
SUBLANES = 8
ROW_TILE = 384
QBLK = 128
KSUB = 128
KTILE = 512
ATILE = KTILE // 2
GLA_STEP = 128
GLA_SUB = 16
ROW_PAD = 1536
ACC_ROWS = DSA_HD + 16
VMEM_LIMIT = 60 * 1024 * 1024
NEG_BIG = -1e30

MISC_IK = 0
MISC_IW = IDX_HD
MISC_GA = IDX_HD + IDX_HEADS

F32 = jnp.float32
BF16 = jnp.bfloat16


def _dot(a, b):
    return jnp.dot(a, b, preferred_element_type=F32)


def _rmsnorm(x, g):
    return x * lax.rsqrt(jnp.mean(x * x, axis=-1, keepdims=True) + EPS) * g


def _const_spec(shape):
    return pl.BlockSpec(shape, lambda *_: (0,) * len(shape), pipeline_mode=pl.Buffered(1))


def _swiglu_half(x, norm_ref, wg_ref, wu_ref, wd_ref, ff_chunk):
    hn = _rmsnorm(x, norm_ref[...]).astype(BF16)
    d_ff = wg_ref.shape[1]
    acc = jnp.zeros(x.shape, F32)
    for c in range(d_ff // ff_chunk):
        sl = slice(c * ff_chunk, (c + 1) * ff_chunk)
        g = _dot(hn, wg_ref[:, sl])
        u = _dot(hn, wu_ref[:, sl])
        a = (g * jax.nn.sigmoid(g) * u).astype(BF16)
        acc = acc + _dot(a, wd_ref[sl, :])
    return x + 0.5 * acc


def _ffn1_kernel(h_ref, norm_ref, wg_ref, wu_ref, wd_ref, o_ref, *, ff_chunk):
    o_ref[...] = _swiglu_half(h_ref[...], norm_ref, wg_ref, wu_ref, wd_ref, ff_chunk)


def _ffn_chunk(d_ff):
    for c in (1408, 1024, 512, 256, 128):
        if d_ff % c == 0:
            return c
    return d_ff


def _ffn1(h, norm, wg, wu, wd):
    lp, d = h.shape
    d_ff = wg.shape[1]
    row = pl.BlockSpec((ROW_TILE, d), lambda i: (i, 0))
    return pl.pallas_call(
        functools.partial(_ffn1_kernel, ff_chunk=_ffn_chunk(d_ff)),
        grid=(lp // ROW_TILE,),
        in_specs=[row, _const_spec((1, d)), _const_spec((d, d_ff)), _const_spec((d, d_ff)),
                  _const_spec((d_ff, d))],
        out_specs=row,
        out_shape=jax.ShapeDtypeStruct((lp, d), F32),
        compiler_params=pltpu.CompilerParams(dimension_semantics=("parallel",),
                                             vmem_limit_bytes=VMEM_LIMIT),
        name="ffn1",
    )(h, norm, wg, wu, wd)


def _rope(x, cos, sin_signed, first_half):
    parts = []
    for c in range(x.shape[1] // LANES):
        xc = x[:, c * LANES:(c + 1) * LANES]
        rot = jnp.where(first_half, pltpu.roll(xc, LANES - 32, 1), pltpu.roll(xc, 32, 1))
        parts.append(xc * cos + rot * sin_signed)
    return parts[0] if len(parts) == 1 else jnp.concatenate(parts, axis=1)


def _log_sigmoid(x):
    return jnp.minimum(x, 0.0) - jnp.log(1.0 + jnp.exp(-jnp.abs(x)))


def _in_proj_kernel(h_ref, norm_ref, w_ref, wa2_ref, ba_ref, cos_ref, sin_ref,
                    gqk_ref, gv_ref, gr_ref, gg_ref, dqt_ref, dk_ref, dvt_ref, iqt_ref, ik_ref, iwt_ref):
    hn = _rmsnorm(h_ref[...], norm_ref[...]).astype(BF16)
    cos = cos_ref[...]
    sin_s = sin_ref[...]
    lane = lax.broadcasted_iota(jnp.int32, cos.shape, 1)
    first_half = (lane % DSA_HD) < (DSA_HD // 2)

    def proj(c0, width):
        return _dot(hn, w_ref[:, c0:c0 + width])

    c = 0
    gqk_ref[...] = proj(c, 2 * GLA_QK); c += 2 * GLA_QK
    gv_ref[...] = proj(c, GLA_WIDTH); c += GLA_WIDTH
    gr_ref[...] = proj(c, GLA_WIDTH); c += GLA_WIDTH
    dqt_ref[...] = (_rope(proj(c, DSA_WIDTH), cos, sin_s, first_half) * (DSA_HD ** -0.5)).T.astype(BF16)
    c += DSA_WIDTH
    dk_ref[...] = _rope(proj(c, DSA_WIDTH), cos, sin_s, first_half).astype(BF16); c += DSA_WIDTH
    dvt_ref[...] = proj(c, DSA_WIDTH).T.astype(BF16); c += DSA_WIDTH
    iqt_ref[...] = (_rope(proj(c, IDX_WIDTH), cos, sin_s, first_half) * (IDX_HD ** -0.5)).T.astype(BF16)
    c += IDX_WIDTH
    misc = proj(c, LANES)
    ik_ref[...] = _rope(misc, cos, sin_s, first_half)[:, MISC_IK:MISC_IK + IDX_HD].astype(BF16)
    iwt_ref[...] = (misc * (IDX_HEADS ** -0.5)).T
    pre = _dot(misc.astype(BF16), wa2_ref[...]) + ba_ref[...]
    gg_ref[...] = _log_sigmoid(pre) * (1.0 / GLA_GATE_NORM)


def _in_proj(h, norm, w_all, wa2p, ba, cos2, sin2s):
    lp, d = h.shape
    ncols = w_all.shape[1]

    def row(width):
        return pl.BlockSpec((ROW_TILE, width), lambda i: (i, 0))

    def col(height):
        return pl.BlockSpec((height, ROW_TILE), lambda i: (0, i))

    outs = [
        (row(2 * GLA_QK), (lp, 2 * GLA_QK), F32), (row(GLA_WIDTH), (lp, GLA_WIDTH), F32),
        (row(GLA_WIDTH), (lp, GLA_WIDTH), F32), (row(GLA_QK), (lp, GLA_QK), F32),
        (col(DSA_WIDTH), (DSA_WIDTH, lp), BF16), (row(DSA_WIDTH), (lp, DSA_WIDTH), BF16),
        (col(DSA_WIDTH), (DSA_WIDTH, lp), BF16), (col(IDX_WIDTH), (IDX_WIDTH, lp), BF16),
        (row(IDX_HD), (lp, IDX_HD), BF16), (col(LANES), (LANES, lp), F32),
    ]
    return pl.pallas_call(
        _in_proj_kernel,
        grid=(lp // ROW_TILE,),
        in_specs=[row(d), _const_spec((1, d)), _const_spec((d, ncols)), _const_spec((LANES, GLA_QK)),
                  _const_spec((1, GLA_QK)), row(LANES), row(LANES)],
        out_specs=[spec for spec, _, _ in outs],
        out_shape=[jax.ShapeDtypeStruct(shape, dt) for _, shape, dt in outs],
        compiler_params=pltpu.CompilerParams(dimension_semantics=("parallel",),
                                             vmem_limit_bytes=VMEM_LIMIT),
        name="in_proj",
    )(h, norm, w_all, wa2p, ba, cos2, sin2s)


def _gla_kernel(qk_ref, v_ref, r_ref, g_ref, hnorm_ref, o_ref, s_ref):
    n_sub = GLA_STEP // GLA_SUB

    @pl.when(pl.program_id(0) == 0)
    def _():
        s_ref[...] = jnp.zeros(s_ref.shape, F32)

    q = qk_ref[:, :GLA_QK] * (GLA_DK ** -0.5)
    k = qk_ref[:, GLA_QK:]
    v = v_ref[...]
    g = g_ref[...]

    ri = lax.broadcasted_iota(jnp.int32, (GLA_STEP, GLA_STEP), 0)
    ci = lax.broadcasted_iota(jnp.int32, (GLA_STEP, GLA_STEP), 1)
    tri = ((ri // GLA_SUB == ci // GLA_SUB) & (ci <= ri)).astype(F32)
    b = jnp.dot(tri, g, preferred_element_type=F32, precision=lax.Precision.HIGHEST)

    b3 = b.reshape(n_sub, GLA_SUB, GLA_QK)
    k3 = k.reshape(n_sub, GLA_SUB, GLA_QK)
    v3 = v.reshape(n_sub, GLA_SUB, GLA_WIDTH)
    b_last = jnp.broadcast_to(b3[:, GLA_SUB - 1:GLA_SUB, :], b3.shape).reshape(GLA_STEP, GLA_QK)

    pos = lax.broadcasted_iota(jnp.int32, (GLA_STEP, GLA_QK), 0) % GLA_SUB
    t_rows = []
    for j in range(GLA_SUB):
        kj = jnp.broadcast_to(k3[:, j:j + 1, :], k3.shape).reshape(GLA_STEP, GLA_QK)
        bj = jnp.broadcast_to(b3[:, j:j + 1, :], b3.shape).reshape(GLA_STEP, GLA_QK)
        e = jnp.where(pos >= j, b - bj, -jnp.inf)
        t_rows.append((q * kj * jnp.exp(e)).astype(BF16))
    t = jnp.concatenate(t_rows, axis=0)
    hr = lax.broadcasted_iota(jnp.int32, (GLA_QK, GLA_WIDTH), 0) // GLA_DK
    hc = lax.broadcasted_iota(jnp.int32, (GLA_QK, GLA_WIDTH), 1) // GLA_DV
    head_sum = (hr == hc).astype(BF16)
    p = _dot(t, head_sum)
    o = jnp.zeros((GLA_STEP, GLA_WIDTH), F32)
    for j in range(GLA_SUB):
        vj = jnp.broadcast_to(v3[:, j:j + 1, :], v3.shape).reshape(GLA_STEP, GLA_WIDTH)
        o = o + p[j * GLA_STEP:(j + 1) * GLA_STEP, :] * vj

    qe = q * jnp.exp(b)
    kd = k * jnp.exp(b_last - b)
    kd_t = kd.T
    dec_t = jnp.exp(b).T
    lane_head = lax.broadcasted_iota(jnp.int32, (GLA_SUB, GLA_QK), 1) // GLA_DK
    s = s_ref[...]
    o_inter = []
    for c in range(n_sub):
        rows = slice(c * GLA_SUB, (c + 1) * GLA_SUB)
        qe_c = qe[rows, :]
        q_stack = jnp.concatenate(
            [jnp.where(lane_head == h, qe_c, 0.0) for h in range(GLA_HEADS)], axis=0).astype(BF16)
        r_stack = _dot(q_stack, s.astype(BF16))
        o_inter.append(jnp.concatenate(
            [r_stack[h * GLA_SUB:(h + 1) * GLA_SUB, :] for h in range(GLA_HEADS)], axis=1))
        u = _dot(kd_t[:, rows].astype(BF16), v[rows, :].astype(BF16))
        u_c = jnp.concatenate(
            [u[h * GLA_DK:(h + 1) * GLA_DK, h * GLA_DV:(h + 1) * GLA_DV] for h in range(GLA_HEADS)], axis=0)
        last = c * GLA_SUB + GLA_SUB - 1
        s = dec_t[:, last:last + 1] * s + u_c
    s_ref[...] = s
    o = o + jnp.concatenate(o_inter, axis=0)

    r = r_ref[...]
    gate = r * jax.nn.sigmoid(r)
    hn = hnorm_ref[...]
    outs = []
    for h in range(GLA_HEADS):
        oh = o[:, h * GLA_DV:(h + 1) * GLA_DV]
        outs.append(_rmsnorm(oh, hn))
    o_ref[...] = (jnp.concatenate(outs, axis=1) * gate).astype(o_ref.dtype)


def _gla(gqk, gv, gr, gg, hnorm):
    lp = gqk.shape[0]

    def row(width):
        return pl.BlockSpec((GLA_STEP, width), lambda i: (i, 0))

    return pl.pallas_call(
        _gla_kernel,
        grid=(lp // GLA_STEP,),
        in_specs=[row(2 * GLA_QK), row(GLA_WIDTH), row(GLA_WIDTH), row(GLA_QK), _const_spec((1, GLA_DV))],
        out_specs=row(GLA_WIDTH),
        out_shape=jax.ShapeDtypeStruct((lp, GLA_WIDTH), BF16),
        scratch_shapes=[pltpu.VMEM((GLA_QK, GLA_DV), F32)],
        compiler_params=pltpu.CompilerParams(dimension_semantics=("arbitrary",),
                                             vmem_limit_bytes=VMEM_LIMIT),
        name="gla",
    )(gqk, gv, gr, gg, hnorm)


def _key_to_float(u):
    bits = jnp.where(u >= 0, u, u ^ jnp.int32(0x7FFFFFFF))
    return lax.bitcast_convert_type(bits, F32)


_KEY_NEG_INF = np.int32(np.uint32(0xFF800000) ^ np.uint32(0x7FFFFFFF))
_INT_MIN = np.int32(-(2 ** 31))


def _fold_rows(x, op):
    parts = [x[r:r + SUBLANES, :] for r in range(0, x.shape[0], SUBLANES)]
    while len(parts) > 1:
        parts = [op(parts[a], parts[a + 1]) if a + 1 < len(parts) else parts[a] for a in range(0, len(parts), 2)]
    return parts[0]


def _dsa_block(qt_ref, iqt_ref, iwt_ref, k_ref, vt_ref, ik_ref, o_ref, sc_ref, bias_ref, acc_ref, rhs_ref,
               sa_ref, sb_ref, k_sel):
    i = pl.program_id(0)
    n_tiles = (i * QBLK + QBLK + KTILE - 1) // KTILE
    last = n_tiles - 1
    q_pos = i * QBLK + lax.broadcasted_iota(jnp.int32, (KSUB, QBLK), 1)
    key_row = lax.broadcasted_iota(jnp.int32, (KSUB, QBLK), 0)

    def visible(ks):
        return (ks + key_row) <= q_pos

    def peeled(body, carry):
        carry = lax.fori_loop(0, last, functools.partial(body, masked=False), carry)
        return body(last, carry, masked=True)

    iqt = iqt_ref[...]
    iq_all = jnp.concatenate([iqt[h * IDX_HD:(h + 1) * IDX_HD, :] for h in range(IDX_HEADS)], axis=1)
    iw = iwt_ref[MISC_IW:MISC_IW + IDX_HEADS, :]

    def score_tile(t, carry, masked):
        for sub in range(KTILE // KSUB):
            ks = pl.multiple_of(t * KTILE + sub * KSUB, KSUB)
            lg = _dot(ik_ref[pl.ds(ks, KSUB), :], iq_all)
            sc = iw[0:1, :] * jnp.maximum(lg[:, 0:QBLK], 0.0)
            for h in range(1, IDX_HEADS):
                sc = sc + iw[h:h + 1, :] * jnp.maximum(lg[:, h * QBLK:(h + 1) * QBLK], 0.0)
            if masked:
                sc = jnp.where(visible(ks), sc, -jnp.inf)
            sc_ref[pl.ds(ks, KSUB), :] = sc
        return carry

    peeled(score_tile, 0)

    def count_ge(cand):
        def body(t, acc):
            ks = pl.multiple_of(t * KTILE, KTILE)
            return acc + _fold_rows(jnp.where(sc_ref[pl.ds(ks, KTILE), :] >= cand, 1.0, 0.0), jnp.add)
        acc = lax.fori_loop(0, n_tiles, body, jnp.zeros((SUBLANES, QBLK), F32))
        return jnp.sum(acc, axis=0, keepdims=True)

    def bisect_body(it, carry):
        u, n_ge = carry
        cand_u = u + lax.shift_left(jnp.int32(1), 31 - it)
        cnt = count_ge(_key_to_float(cand_u))
        ok = cnt >= float(k_sel)
        return jnp.where(ok, cand_u, u), jnp.where(ok, cnt, n_ge)

    u, n_ge = lax.fori_loop(0, 32, bisect_body,
                            (jnp.full((1, QBLK), _INT_MIN, jnp.int32), jnp.zeros((1, QBLK), F32)))
    stalled = u < _KEY_NEG_INF
    u = jnp.maximum(u, _KEY_NEG_INF)
    thr = _key_to_float(u)
    thr_next = _key_to_float(u + 1)
    has_ties = jnp.max(jnp.where(stalled, 0.0, n_ge)) > float(k_sel)

    def bias_fast(t, carry, masked):
        for sub in range(KTILE // KSUB):
            ks = pl.multiple_of(t * KTILE + sub * KSUB, KSUB)
            keep = sc_ref[pl.ds(ks, KSUB), :] >= thr
            if masked:
                keep = jnp.logical_and(keep, visible(ks))
            bias_ref[pl.ds(ks, KSUB), :] = jnp.where(keep, 0.0, NEG_BIG).astype(BF16)
        return carry

    def bias_ties(t, run, masked, budget, tri):
        for sub in range(KTILE // KSUB):
            ks = pl.multiple_of(t * KTILE + sub * KSUB, KSUB)
            x = sc_ref[pl.ds(ks, KSUB), :]
            above = x >= thr_next
            tie = jnp.logical_and(x >= thr, jnp.logical_not(above))
            tie_f = jnp.where(tie, 1.0, 0.0)
            rank = run + _dot(tri, tie_f.astype(BF16))
            keep = jnp.logical_or(above, jnp.logical_and(tie, rank <= budget))
            if masked:
                keep = jnp.logical_and(keep, visible(ks))
            bias_ref[pl.ds(ks, KSUB), :] = jnp.where(keep, 0.0, NEG_BIG).astype(BF16)
            run = run + jnp.sum(tie_f, axis=0, keepdims=True)
        return run

    @pl.when(jnp.logical_not(has_ties))
    def _():
        peeled(bias_fast, 0)

    @pl.when(has_ties)
    def _():
        budget = float(k_sel) - count_ge(thr_next)
        tri = (lax.broadcasted_iota(jnp.int32, (KSUB, KSUB), 1) <= key_row).astype(BF16)
        peeled(functools.partial(bias_ties, budget=budget, tri=tri), jnp.zeros((1, QBLK), F32))

    qt = qt_ref[...]
    eye = (lax.broadcasted_iota(jnp.int32, (QBLK, QBLK), 0)
           == lax.broadcasted_iota(jnp.int32, (QBLK, QBLK), 1)).astype(BF16)
    zero = jnp.zeros((DSA_HD, QBLK), BF16)
    n_pairs = DSA_HEADS // 2
    for c in range(n_pairs):
        qa = qt[(2 * c) * DSA_HD:(2 * c + 1) * DSA_HD, :]
        qb = qt[(2 * c + 1) * DSA_HD:(2 * c + 2) * DSA_HD, :]
        rhs_ref[c] = jnp.concatenate([jnp.concatenate([qa, zero], axis=1),
                                      jnp.concatenate([zero, qb], axis=1),
                                      jnp.concatenate([eye, eye], axis=1)], axis=0)
    acc_ref[...] = jnp.zeros(acc_ref.shape, F32)
    ones_rows = jnp.ones((ACC_ROWS - DSA_HD, ATILE), BF16)

    def qk_stage(ks, s_ref):
        bias = bias_ref[pl.ds(ks, ATILE), :]
        tile_max = []
        for c in range(n_pairs):
            lhs = jnp.concatenate([k_ref[pl.ds(ks, ATILE), c * 2 * DSA_HD:(c + 1) * 2 * DSA_HD], bias], axis=1)
            s = _dot(lhs, rhs_ref[c])
            s_ref[c] = s
            tile_max.append(_fold_rows(s, jnp.maximum))
        return tuple(tile_max)

    def pv_stage(ks, s_ref, tile_max, ms):
        new_ms = []
        for c in range(n_pairs):
            m_new = jnp.maximum(ms[c], jnp.max(tile_max[c], axis=0, keepdims=True))
            alpha = jnp.exp(ms[c] - m_new)
            p = jnp.exp(s_ref[c] - m_new).astype(BF16)
            for hh in range(2):
                h = 2 * c + hh
                rows = slice(h * ACC_ROWS, (h + 1) * ACC_ROWS)
                vt_aug = jnp.concatenate([vt_ref[h * DSA_HD:(h + 1) * DSA_HD, pl.ds(ks, ATILE)], ones_rows], axis=0)
                pv = _dot(vt_aug, p[:, hh * QBLK:(hh + 1) * QBLK])
                acc_ref[rows, :] = alpha[:, hh * QBLK:(hh + 1) * QBLK] * acc_ref[rows, :] + pv
            new_ms.append(m_new)
        return tuple(new_ms)

    def attn_tile(t, carry):
        max_a, ms = carry
        k0 = pl.multiple_of(t * KTILE, KTILE)
        k1 = pl.multiple_of(t * KTILE + ATILE, ATILE)
        k2 = pl.multiple_of(t * KTILE + KTILE, KTILE)
        max_b = qk_stage(k1, sb_ref)
        ms = pv_stage(k0, sa_ref, max_a, ms)
        max_a = qk_stage(k2, sa_ref)
        ms = pv_stage(k1, sb_ref, max_b, ms)
        return max_a, ms

    ms = tuple(jnp.full((1, 2 * QBLK), NEG_BIG, F32) for _ in range(n_pairs))
    max_a, ms = lax.fori_loop(0, last, attn_tile, (qk_stage(0, sa_ref), ms))
    k0 = pl.multiple_of(last * KTILE, KTILE)
    k1 = pl.multiple_of(last * KTILE + ATILE, ATILE)
    max_b = qk_stage(k1, sb_ref)
    ms = pv_stage(k0, sa_ref, max_a, ms)
    pv_stage(k1, sb_ref, max_b, ms)
    outs = []
    for h in range(DSA_HEADS):
        a = acc_ref[h * ACC_ROWS:(h + 1) * ACC_ROWS, :]
        outs.append(a[:DSA_HD, :] / a[DSA_HD:DSA_HD + 1, :])
    o_ref[...] = jnp.concatenate(outs, axis=0).T.astype(o_ref.dtype)


def _dsa_kernel(qt_ref, iqt_ref, iwt_ref, k_ref, vt_ref, ik_ref, o_ref, sc_ref, bias_ref, acc_ref, rhs_ref,
                sa_ref, sb_ref, *, k_sel, l_tok):
    has_tokens = pl.program_id(0) * QBLK < l_tok

    @pl.when(has_tokens)
    def _():
        _dsa_block(qt_ref, iqt_ref, iwt_ref, k_ref, vt_ref, ik_ref, o_ref, sc_ref, bias_ref, acc_ref, rhs_ref,
                   sa_ref, sb_ref, k_sel)

    @pl.when(jnp.logical_not(has_tokens))
    def _():
        o_ref[...] = jnp.zeros(o_ref.shape, o_ref.dtype)


def _dsa(dqt, dk, dvt, iqt, ik, iwt, k_sel, l_tok):
    lp = dk.shape[0]

    def col(height):
        return pl.BlockSpec((height, QBLK), lambda i: (0, i))

    return pl.pallas_call(
        functools.partial(_dsa_kernel, k_sel=k_sel, l_tok=l_tok),
        grid=(lp // QBLK,),
        in_specs=[col(DSA_WIDTH), col(IDX_WIDTH), col(LANES), _const_spec((lp, DSA_WIDTH)),
                  _const_spec((DSA_WIDTH, lp)), _const_spec((lp, IDX_HD))],
        out_specs=pl.BlockSpec((QBLK, DSA_WIDTH), lambda i: (i, 0)),
        out_shape=jax.ShapeDtypeStruct((lp, DSA_WIDTH), BF16),
        scratch_shapes=[pltpu.VMEM((lp, QBLK), F32), pltpu.VMEM((lp, QBLK), BF16),
                        pltpu.VMEM((DSA_HEADS * ACC_ROWS, QBLK), F32),
                        pltpu.VMEM((DSA_HEADS // 2, 2 * LANES, 2 * QBLK), BF16),
                        pltpu.VMEM((DSA_HEADS // 2, ATILE, 2 * QBLK), F32),
                        pltpu.VMEM((DSA_HEADS // 2, ATILE, 2 * QBLK), F32)],
        compiler_params=pltpu.CompilerParams(dimension_semantics=("parallel",),
                                             vmem_limit_bytes=VMEM_LIMIT),
        name="dsa",
    )(dqt, iqt, iwt, dk, dvt, ik)


def _out_ffn2_kernel(h_ref, og_ref, od_ref, wo_ref, norm_ref, wg_ref, wu_ref, wd_ref, fnorm_ref, o_ref, *,
                     ff_chunk):
    h2 = h_ref[...] + _dot(og_ref[...], wo_ref[:GLA_WIDTH, :]) + _dot(od_ref[...], wo_ref[GLA_WIDTH:, :])
    h3 = _swiglu_half(h2, norm_ref, wg_ref, wu_ref, wd_ref, ff_chunk)
    o_ref[...] = _rmsnorm(h3, fnorm_ref[...])


def _out_ffn2(h, o_gla, o_dsa, wo, norm, wg, wu, wd, fnorm):
    lp, d = h.shape
    d_ff = wg.shape[1]

    def row(width):
        return pl.BlockSpec((ROW_TILE, width), lambda i: (i, 0))

    return pl.pallas_call(
        functools.partial(_out_ffn2_kernel, ff_chunk=_ffn_chunk(d_ff)),
        grid=(lp // ROW_TILE,),
        in_specs=[row(d), row(GLA_WIDTH), row(DSA_WIDTH), _const_spec((GLA_WIDTH + DSA_WIDTH, d)),
                  _const_spec((1, d)), _const_spec((d, d_ff)), _const_spec((d, d_ff)), _const_spec((d_ff, d)),
                  _const_spec((1, d))],
        out_specs=row(d),
        out_shape=jax.ShapeDtypeStruct((lp, d), F32),
        compiler_params=pltpu.CompilerParams(dimension_semantics=("parallel",),
                                             vmem_limit_bytes=VMEM_LIMIT),
        name="out_ffn2",
    )(h, o_gla, o_dsa, wo, norm, wg, wu, wd, fnorm)


def _rope_tables(lp):
    pos = jnp.arange(lp, dtype=F32)
    inv_freq = 1.0 / (ROPE_THETA ** (jnp.arange(0, DSA_HD, 2, dtype=F32) / DSA_HD))
    ang = pos[:, None] * inv_freq[None, :]
    cos = jnp.cos(ang)
    sin = jnp.sin(ang)
    cos2 = jnp.concatenate([cos, cos, cos, cos], axis=-1)
    sin2s = jnp.concatenate([-sin, sin, -sin, sin], axis=-1)
    return cos2, sin2s


def _layer(h, k_sel, l_tok, ffn1_norm, ffn1_w_gate, ffn1_w_up, ffn1_w_down, mix_norm, w_in, gla_w_a2, gla_b_a,
           gla_head_norm, w_out, ffn2_norm, ffn2_w_gate, ffn2_w_up, ffn2_w_down, final_norm, cos2, sin2s):
    d = h.shape[1]
    h1 = _ffn1(h, ffn1_norm.reshape(1, d), ffn1_w_gate.astype(BF16), ffn1_w_up.astype(BF16),
               ffn1_w_down.astype(BF16))

    sizes = (GLA_QK, GLA_QK, GLA_WIDTH, GLA_WIDTH, GLA_LOWRANK, DSA_WIDTH, DSA_WIDTH, DSA_WIDTH, IDX_WIDTH,
             IDX_HD, IDX_HEADS)
    offs = np.concatenate([[0], np.cumsum(sizes)])
    col = lambda n: w_in[:, offs[n]:offs[n + 1]]
    pad = jnp.zeros((d, LANES - IDX_HD - IDX_HEADS - GLA_LOWRANK), w_in.dtype)
    w_all = jnp.concatenate([col(0), col(1), col(2), col(3), col(5), col(6), col(7), col(8),
                             col(9), col(10), col(4), pad], axis=1).astype(BF16)
    wa2p = jnp.zeros((LANES, GLA_QK), F32).at[MISC_GA:MISC_GA + GLA_LOWRANK, :].set(gla_w_a2).astype(BF16)

    gqk, gv, gr, gg, dqt, dk, dvt, iqt, ik, iwt = _in_proj(
        h1, mix_norm.reshape(1, d), w_all, wa2p, gla_b_a.reshape(1, GLA_QK), cos2, sin2s)
    o_gla = _gla(gqk, gv, gr, gg, gla_head_norm.reshape(1, GLA_DV))
    o_dsa = _dsa(dqt, dk, dvt, iqt, ik, iwt, k_sel, l_tok)
    return _out_ffn2(h1, o_gla, o_dsa, w_out.astype(BF16), ffn2_norm.reshape(1, d), ffn2_w_gate.astype(BF16),
                     ffn2_w_up.astype(BF16), ffn2_w_down.astype(BF16), final_norm.reshape(1, d))


def kernel(x, meta_tokens, ffn1_norm, ffn1_w_gate, ffn1_w_up, ffn1_w_down, mix_norm, w_in, gla_w_a2, gla_b_a,
           gla_head_norm, w_out, ffn2_norm, ffn2_w_gate, ffn2_w_up, ffn2_w_down, final_norm):
    batch, seq, d = x.shape
    depth = ffn1_norm.shape[0]
    assert depth == 1, "the final norm is fused into the layer's last kernel"
    k_sel = min(TOPK_MAX, seq // 4)
    l_tok = seq + N_META
    lp = -(-l_tok // ROW_PAD) * ROW_PAD
    cos2, sin2s = _rope_tables(lp)
    outs = []
    for bi in range(batch):
        h = jnp.concatenate([meta_tokens.astype(x.dtype), x[bi], jnp.zeros((lp - l_tok, d), x.dtype)], axis=0)
        h = _layer(h, k_sel, l_tok, ffn1_norm[0], ffn1_w_gate[0], ffn1_w_up[0], ffn1_w_down[0], mix_norm[0],
                   w_in[0], gla_w_a2[0], gla_b_a[0], gla_head_norm[0], w_out[0], ffn2_norm[0], ffn2_w_gate[0],
                   ffn2_w_up[0], ffn2_w_down[0], final_norm, cos2, sin2s)
        outs.append(h[N_META:l_tok])
    return jnp.stack(outs, axis=0)
```

```python
import functools

import numpy as np
import jax
import jax.numpy as jnp
from jax import lax
from jax.experimental import pallas as pl
from jax.experimental.pallas import tpu as pltpu

N_META = 16
EPS = 1e-6
ROPE_THETA = 10000.0
GLA_HEADS = 4
GLA_DK = 64
GLA_DV = 128
GLA_LOWRANK = 16
GLA_GATE_NORM = 16.0
DSA_HEADS = 8
DSA_HD = 64
IDX_HEADS = 8
IDX_HD = 64
TOPK_MAX = 256

GLA_QK = GLA_HEADS * GLA_DK
GLA_WIDTH = GLA_HEADS * GLA_DV
DSA_WIDTH = DSA_HEADS * DSA_HD
IDX_WIDTH = IDX_HEADS * IDX_HD

LANES = 128
SUBLANES = 8
ROW_TILE = 384
QBLK = 128
KSUB = 128
KTILE = 512
ATILE = KTILE // 2
N_GROUPS = 2 * KSUB
UNCHECKED_HALVINGS = 12
GLA_STEP = 128
GLA_SUB = 16
ROW_PAD = 1536
ACC_ROWS = DSA_HD + 16
VMEM_LIMIT = 60 * 1024 * 1024
NEG_BIG = -1e30

MISC_IK = 0
MISC_IW = IDX_HD
MISC_GA = IDX_HD + IDX_HEADS

F32 = jnp.float32
BF16 = jnp.bfloat16


def _dot(a, b):
    return jnp.dot(a, b, preferred_element_type=F32)


def _rmsnorm(x, g):
    return x * lax.rsqrt(jnp.mean(x * x, axis=-1, keepdims=True) + EPS) * g


def _const_spec(shape):
    return pl.BlockSpec(shape, lambda *_: (0,) * len(shape), pipeline_mode=pl.Buffered(1))


def _swiglu_half(x, norm_ref, wg_ref, wu_ref, wd_ref, ff_chunk):
    hn = _rmsnorm(x, norm_ref[...]).astype(BF16)
    d_ff = wg_ref.shape[1]
    acc = jnp.zeros(x.shape, F32)
    for c in range(d_ff // ff_chunk):
        sl = slice(c * ff_chunk, (c + 1) * ff_chunk)
        g = _dot(hn, wg_ref[:, sl])
        u = _dot(hn, wu_ref[:, sl])
        a = (g * jax.nn.sigmoid(g) * u).astype(BF16)
        acc = acc + _dot(a, wd_ref[sl, :])
    return x + 0.5 * acc


def _ffn1_kernel(h_ref, norm_ref, wg_ref, wu_ref, wd_ref, o_ref, *, ff_chunk):
    o_ref[...] = _swiglu_half(h_ref[...], norm_ref, wg_ref, wu_ref, wd_ref, ff_chunk)


def _ffn_chunk(d_ff):
    for c in (1408, 1024, 512, 256, 128):
        if d_ff % c == 0:
            return c
    return d_ff


def _ffn1(h, norm, wg, wu, wd):
    lp, d = h.shape
    d_ff = wg.shape[1]
    row = pl.BlockSpec((ROW_TILE, d), lambda i: (i, 0))
    return pl.pallas_call(
        functools.partial(_ffn1_kernel, ff_chunk=_ffn_chunk(d_ff)),
        grid=(lp // ROW_TILE,),
        in_specs=[row, _const_spec((1, d)), _const_spec((d, d_ff)), _const_spec((d, d_ff)),
                  _const_spec((d_ff, d))],
        out_specs=row,
        out_shape=jax.ShapeDtypeStruct((lp, d), F32),
        compiler_params=pltpu.CompilerParams(dimension_semantics=("parallel",),
                                             vmem_limit_bytes=VMEM_LIMIT),
        name="ffn1",
    )(h, norm, wg, wu, wd)


def _rope(x, cos, sin_signed, first_half):
    parts = []
    for c in range(x.shape[1] // LANES):
        xc = x[:, c * LANES:(c + 1) * LANES]
        rot = jnp.where(first_half, pltpu.roll(xc, LANES - 32, 1), pltpu.roll(xc, 32, 1))
        parts.append(xc * cos + rot * sin_signed)
    return parts[0] if len(parts) == 1 else jnp.concatenate(parts, axis=1)


def _log_sigmoid(x):
    return jnp.minimum(x, 0.0) - jnp.log(1.0 + jnp.exp(-jnp.abs(x)))


def _in_proj_kernel(h_ref, norm_ref, w_ref, wa2_ref, ba_ref, cos_ref, sin_ref,
                    gqk_ref, gv_ref, gr_ref, gg_ref, dqt_ref, dk_ref, dvt_ref, iqt_ref, ik_ref, iwt_ref):
    hn = _rmsnorm(h_ref[...], norm_ref[...]).astype(BF16)
    cos = cos_ref[...]
    sin_s = sin_ref[...]
    lane = lax.broadcasted_iota(jnp.int32, cos.shape, 1)
    first_half = (lane % DSA_HD) < (DSA_HD // 2)

    def proj(c0, width):
        return _dot(hn, w_ref[:, c0:c0 + width])

    c = 0
    gqk_ref[...] = proj(c, 2 * GLA_QK); c += 2 * GLA_QK
    gv_ref[...] = proj(c, GLA_WIDTH); c += GLA_WIDTH
    gr_ref[...] = proj(c, GLA_WIDTH); c += GLA_WIDTH
    dqt_ref[...] = (_rope(proj(c, DSA_WIDTH), cos, sin_s, first_half) * (DSA_HD ** -0.5)).T.astype(BF16)
    c += DSA_WIDTH
    dk_ref[...] = _rope(proj(c, DSA_WIDTH), cos, sin_s, first_half).astype(BF16); c += DSA_WIDTH
    dvt_ref[...] = proj(c, DSA_WIDTH).T.astype(BF16); c += DSA_WIDTH
    iqt_ref[...] = (_rope(proj(c, IDX_WIDTH), cos, sin_s, first_half) * (IDX_HD ** -0.5)).T.astype(BF16)
    c += IDX_WIDTH
    misc = proj(c, LANES)
    ik_ref[...] = _rope(misc, cos, sin_s, first_half)[:, MISC_IK:MISC_IK + IDX_HD].astype(BF16)
    iwt_ref[...] = (misc * (IDX_HEADS ** -0.5)).T
    pre = _dot(misc.astype(BF16), wa2_ref[...]) + ba_ref[...]
    gg_ref[...] = _log_sigmoid(pre) * (1.0 / GLA_GATE_NORM)


def _in_proj(h, norm, w_all, wa2p, ba, cos2, sin2s):
    lp, d = h.shape
    ncols = w_all.shape[1]

    def row(width):
        return pl.BlockSpec((ROW_TILE, width), lambda i: (i, 0))

    def col(height):
        return pl.BlockSpec((height, ROW_TILE), lambda i: (0, i))

    outs = [
        (row(2 * GLA_QK), (lp, 2 * GLA_QK), F32), (row(GLA_WIDTH), (lp, GLA_WIDTH), F32),
        (row(GLA_WIDTH), (lp, GLA_WIDTH), F32), (row(GLA_QK), (lp, GLA_QK), F32),
        (col(DSA_WIDTH), (DSA_WIDTH, lp), BF16), (row(DSA_WIDTH), (lp, DSA_WIDTH), BF16),
        (col(DSA_WIDTH), (DSA_WIDTH, lp), BF16), (col(IDX_WIDTH), (IDX_WIDTH, lp), BF16),
        (row(IDX_HD), (lp, IDX_HD), BF16), (col(LANES), (LANES, lp), F32),
    ]
    return pl.pallas_call(
        _in_proj_kernel,
        grid=(lp // ROW_TILE,),
        in_specs=[row(d), _const_spec((1, d)), _const_spec((d, ncols)), _const_spec((LANES, GLA_QK)),
                  _const_spec((1, GLA_QK)), row(LANES), row(LANES)],
        out_specs=[spec for spec, _, _ in outs],
        out_shape=[jax.ShapeDtypeStruct(shape, dt) for _, shape, dt in outs],
        compiler_params=pltpu.CompilerParams(dimension_semantics=("parallel",),
                                             vmem_limit_bytes=VMEM_LIMIT),
        name="in_proj",
    )(h, norm, w_all, wa2p, ba, cos2, sin2s)


def _gla_kernel(qk_ref, v_ref, r_ref, g_ref, hnorm_ref, o_ref, s_ref):
    n_sub = GLA_STEP // GLA_SUB

    @pl.when(pl.program_id(0) == 0)
    def _():
        s_ref[...] = jnp.zeros(s_ref.shape, F32)

    q = qk_ref[:, :GLA_QK] * (GLA_DK ** -0.5)
    k = qk_ref[:, GLA_QK:]
    v = v_ref[...]
    g = g_ref[...]

    ri = lax.broadcasted_iota(jnp.int32, (GLA_STEP, GLA_STEP), 0)
    ci = lax.broadcasted_iota(jnp.int32, (GLA_STEP, GLA_STEP), 1)
    tri = ((ri // GLA_SUB == ci // GLA_SUB) & (ci <= ri)).astype(F32)
    b = jnp.dot(tri, g, preferred_element_type=F32, precision=lax.Precision.HIGHEST)

    b3 = b.reshape(n_sub, GLA_SUB, GLA_QK)
    k3 = k.reshape(n_sub, GLA_SUB, GLA_QK)
    v3 = v.reshape(n_sub, GLA_SUB, GLA_WIDTH)
    b_last = jnp.broadcast_to(b3[:, GLA_SUB - 1:GLA_SUB, :], b3.shape).reshape(GLA_STEP, GLA_QK)

    pos = lax.broadcasted_iota(jnp.int32, (GLA_STEP, GLA_QK), 0) % GLA_SUB
    t_rows = []
    for j in range(GLA_SUB):
        kj = jnp.broadcast_to(k3[:, j:j + 1, :], k3.shape).reshape(GLA_STEP, GLA_QK)
        bj = jnp.broadcast_to(b3[:, j:j + 1, :], b3.shape).reshape(GLA_STEP, GLA_QK)
        e = jnp.where(pos >= j, b - bj, -jnp.inf)
        t_rows.append((q * kj * jnp.exp(e)).astype(BF16))
    t = jnp.concatenate(t_rows, axis=0)
    hr = lax.broadcasted_iota(jnp.int32, (GLA_QK, GLA_WIDTH), 0) // GLA_DK
    hc = lax.broadcasted_iota(jnp.int32, (GLA_QK, GLA_WIDTH), 1) // GLA_DV
    head_sum = (hr == hc).astype(BF16)
    p = _dot(t, head_sum)
    o = jnp.zeros((GLA_STEP, GLA_WIDTH), F32)
    for j in range(GLA_SUB):
        vj = jnp.broadcast_to(v3[:, j:j + 1, :], v3.shape).reshape(GLA_STEP, GLA_WIDTH)
        o = o + p[j * GLA_STEP:(j + 1) * GLA_STEP, :] * vj

    qe = q * jnp.exp(b)
    kd = k * jnp.exp(b_last - b)
    kd_t = kd.T
    dec_t = jnp.exp(b).T
    lane_head = lax.broadcasted_iota(jnp.int32, (GLA_SUB, GLA_QK), 1) // GLA_DK
    s = s_ref[...]
    o_inter = []
    for c in range(n_sub):
        rows = slice(c * GLA_SUB, (c + 1) * GLA_SUB)
        qe_c = qe[rows, :]
        q_stack = jnp.concatenate(
            [jnp.where(lane_head == h, qe_c, 0.0) for h in range(GLA_HEADS)], axis=0).astype(BF16)
        r_stack = _dot(q_stack, s.astype(BF16))
        o_inter.append(jnp.concatenate(
            [r_stack[h * GLA_SUB:(h + 1) * GLA_SUB, :] for h in range(GLA_HEADS)], axis=1))
        u = _dot(kd_t[:, rows].astype(BF16), v[rows, :].astype(BF16))
        u_c = jnp.concatenate(
            [u[h * GLA_DK:(h + 1) * GLA_DK, h * GLA_DV:(h + 1) * GLA_DV] for h in range(GLA_HEADS)], axis=0)
        last = c * GLA_SUB + GLA_SUB - 1
        s = dec_t[:, last:last + 1] * s + u_c
    s_ref[...] = s
    o = o + jnp.concatenate(o_inter, axis=0)

    r = r_ref[...]
    gate = r * jax.nn.sigmoid(r)
    hn = hnorm_ref[...]
    outs = []
    for h in range(GLA_HEADS):
        oh = o[:, h * GLA_DV:(h + 1) * GLA_DV]
        outs.append(_rmsnorm(oh, hn))
    o_ref[...] = (jnp.concatenate(outs, axis=1) * gate).astype(o_ref.dtype)


def _gla(gqk, gv, gr, gg, hnorm):
    lp = gqk.shape[0]

    def row(width):
        return pl.BlockSpec((GLA_STEP, width), lambda i: (i, 0))

    return pl.pallas_call(
        _gla_kernel,
        grid=(lp // GLA_STEP,),
        in_specs=[row(2 * GLA_QK), row(GLA_WIDTH), row(GLA_WIDTH), row(GLA_QK), _const_spec((1, GLA_DV))],
        out_specs=row(GLA_WIDTH),
        out_shape=jax.ShapeDtypeStruct((lp, GLA_WIDTH), BF16),
        scratch_shapes=[pltpu.VMEM((GLA_QK, GLA_DV), F32)],
        compiler_params=pltpu.CompilerParams(dimension_semantics=("arbitrary",),
                                             vmem_limit_bytes=VMEM_LIMIT),
        name="gla",
    )(gqk, gv, gr, gg, hnorm)


def _key_to_float(u):
    bits = jnp.where(u >= 0, u, u ^ jnp.int32(0x7FFFFFFF))
    return lax.bitcast_convert_type(bits, F32)


def _float_to_key(f):
    bits = lax.bitcast_convert_type(f, jnp.int32)
    return jnp.where(bits >= 0, bits, bits ^ jnp.int32(0x7FFFFFFF))


_KEY_NEG_INF = np.int32(np.uint32(0xFF800000) ^ np.uint32(0x7FFFFFFF))


def _fold_rows(x, op):
    parts = [x[r:r + SUBLANES, :] for r in range(0, x.shape[0], SUBLANES)]
    while len(parts) > 1:
        parts = [op(parts[a], parts[a + 1]) if a + 1 < len(parts) else parts[a] for a in range(0, len(parts), 2)]
    return parts[0]


def _dsa_block(qt_ref, iqt_ref, iwt_ref, k_ref, vt_ref, ik_ref, o_ref, sc_ref, bias_ref, acc_ref, rhs_ref,
               sa_ref, sb_ref, gm_ref, k_sel):
    i = pl.program_id(0)
    n_tiles = (i * QBLK + QBLK + KTILE - 1) // KTILE
    last = n_tiles - 1
    q_pos = i * QBLK + lax.broadcasted_iota(jnp.int32, (KSUB, QBLK), 1)
    key_row = lax.broadcasted_iota(jnp.int32, (KSUB, QBLK), 0)

    def visible(ks):
        return (ks + key_row) <= q_pos

    def peeled(body, carry):
        carry = lax.fori_loop(0, last, functools.partial(body, masked=False), carry)
        return body(last, carry, masked=True)

    iqt = iqt_ref[...]
    iq_all = jnp.concatenate([iqt[h * IDX_HD:(h + 1) * IDX_HD, :] for h in range(IDX_HEADS)], axis=1)
    iw = iwt_ref[MISC_IW:MISC_IW + IDX_HEADS, :]
    gm_ref[...] = jnp.full(gm_ref.shape, -jnp.inf, F32)

    def score_tile(t, carry, masked):
        for sub in range(KTILE // KSUB):
            ks = pl.multiple_of(t * KTILE + sub * KSUB, KSUB)
            lg = _dot(ik_ref[pl.ds(ks, KSUB), :], iq_all)
            sc = iw[0:1, :] * jnp.maximum(lg[:, 0:QBLK], 0.0)
            for h in range(1, IDX_HEADS):
                sc = sc + iw[h:h + 1, :] * jnp.maximum(lg[:, h * QBLK:(h + 1) * QBLK], 0.0)
            if masked:
                sc = jnp.where(visible(ks), sc, -jnp.inf)
            sc_ref[pl.ds(ks, KSUB), :] = sc
            grp = slice((sub % 2) * KSUB, (sub % 2 + 1) * KSUB)
            gm_ref[grp, :] = jnp.maximum(gm_ref[grp, :], sc)
        return carry

    peeled(score_tile, 0)

    kk = float(k_sel)

    def count(pred):
        def body(t, acc):
            ks = pl.multiple_of(t * KTILE, KTILE)
            return acc + _fold_rows(jnp.where(pred(sc_ref[pl.ds(ks, KTILE), :]), 1.0, 0.0), jnp.add)
        acc = lax.fori_loop(0, n_tiles, body, jnp.zeros((SUBLANES, QBLK), F32))
        return jnp.sum(acc, axis=0, keepdims=True)

    def count_ge(cand):
        return count(lambda x: x >= cand)

    gm = gm_ref[...]
    lo = jnp.maximum(_float_to_key(jnp.min(_fold_rows(gm, jnp.minimum), axis=0, keepdims=True)), _KEY_NEG_INF)
    hi = _float_to_key(jnp.max(_fold_rows(gm, jnp.maximum), axis=0, keepdims=True))
    n_lo = count_ge(_key_to_float(lo))

    def settled(lo, hi, n_lo):
        return jnp.logical_or(n_lo <= kk, lo >= hi)

    def any_open(lo, hi, n_lo):
        return jnp.max(jnp.where(settled(lo, hi, n_lo), 0.0, 1.0))

    def halve(lo, hi, n_lo):
        open_ = jnp.logical_not(settled(lo, hi, n_lo))
        mid = (lo | hi) - ((lo ^ hi) >> 1)
        cnt = count_ge(_key_to_float(mid))
        up = jnp.logical_and(open_, cnt >= kk)
        down = jnp.logical_and(open_, cnt < kk)
        return jnp.where(up, mid, lo), jnp.where(down, mid - 1, hi), jnp.where(up, cnt, n_lo)

    def checked_halve(st):
        lo, hi, n_lo = halve(*st[:3])
        return lo, hi, n_lo, any_open(lo, hi, n_lo), st[4] + 1

    lo, hi, n_lo = lax.fori_loop(0, UNCHECKED_HALVINGS, lambda _, st: halve(*st), (lo, hi, n_lo))
    lo, hi, n_lo, _, _ = lax.while_loop(lambda st: jnp.logical_and(st[3] > 0.0, st[4] < 33), checked_halve,
                                        (lo, hi, n_lo, any_open(lo, hi, n_lo), jnp.int32(0)))
    few = n_lo < kk
    lo_f = _key_to_float(lo)

    def min_ge_body(t, acc):
        ks = pl.multiple_of(t * KTILE, KTILE)
        x = sc_ref[pl.ds(ks, KTILE), :]
        return jnp.minimum(acc, _fold_rows(jnp.where(x >= lo_f, x, jnp.inf), jnp.minimum))

    thr = jnp.min(lax.fori_loop(0, n_tiles, min_ge_body, jnp.full((SUBLANES, QBLK), jnp.inf, F32)),
                  axis=0, keepdims=True)
    thr = jnp.where(few, -jnp.inf, thr)
    has_ties = jnp.max(jnp.where(few, 0.0, n_lo)) > kk

    def bias_fast(t, carry, masked):
        for sub in range(KTILE // KSUB):
            ks = pl.multiple_of(t * KTILE + sub * KSUB, KSUB)
            keep = sc_ref[pl.ds(ks, KSUB), :] >= thr
            if masked:
                keep = jnp.logical_and(keep, visible(ks))
            bias_ref[pl.ds(ks, KSUB), :] = jnp.where(keep, 0.0, NEG_BIG).astype(BF16)
        return carry

    def bias_ties(t, run, masked, budget, tri):
        for sub in range(KTILE // KSUB):
            ks = pl.multiple_of(t * KTILE + sub * KSUB, KSUB)
            x = sc_ref[pl.ds(ks, KSUB), :]
            above = x > thr
            tie = x == thr
            tie_f = jnp.where(tie, 1.0, 0.0)
            rank = run + _dot(tri, tie_f.astype(BF16))
            keep = jnp.logical_or(above, jnp.logical_and(tie, rank <= budget))
            if masked:
                keep = jnp.logical_and(keep, visible(ks))
            bias_ref[pl.ds(ks, KSUB), :] = jnp.where(keep, 0.0, NEG_BIG).astype(BF16)
            run = run + jnp.sum(tie_f, axis=0, keepdims=True)
        return run

    @pl.when(jnp.logical_not(has_ties))
    def _():
        peeled(bias_fast, 0)

    @pl.when(has_ties)
    def _():
        budget = kk - count(lambda x: x > thr)
        tri = (lax.broadcasted_iota(jnp.int32, (KSUB, KSUB), 1) <= key_row).astype(BF16)
        peeled(functools.partial(bias_ties, budget=budget, tri=tri), jnp.zeros((1, QBLK), F32))

    qt = qt_ref[...]
    eye = (lax.broadcasted_iota(jnp.int32, (QBLK, QBLK), 0)
           == lax.broadcasted_iota(jnp.int32, (QBLK, QBLK), 1)).astype(BF16)
    zero = jnp.zeros((DSA_HD, QBLK), BF16)
    n_pairs = DSA_HEADS // 2
    for c in range(n_pairs):
        qa = qt[(2 * c) * DSA_HD:(2 * c + 1) * DSA_HD, :]
        qb = qt[(2 * c + 1) * DSA_HD:(2 * c + 2) * DSA_HD, :]
        rhs_ref[c] = jnp.concatenate([jnp.concatenate([qa, zero], axis=1),
                                      jnp.concatenate([zero, qb], axis=1),
                                      jnp.concatenate([eye, eye], axis=1)], axis=0)
    acc_ref[...] = jnp.zeros(acc_ref.shape, F32)
    ones_rows = jnp.ones((ACC_ROWS - DSA_HD, ATILE), BF16)

    def qk_stage(ks, s_ref):
        bias = bias_ref[pl.ds(ks, ATILE), :]
        tile_max = []
        for c in range(n_pairs):
            lhs = jnp.concatenate([k_ref[pl.ds(ks, ATILE), c * 2 * DSA_HD:(c + 1) * 2 * DSA_HD], bias], axis=1)
            s = _dot(lhs, rhs_ref[c])
            s_ref[c] = s
            tile_max.append(_fold_rows(s, jnp.maximum))
        return tuple(tile_max)

    def pv_stage(ks, s_ref, tile_max, ms):
        new_ms = []
        for c in range(n_pairs):
            m_new = jnp.maximum(ms[c], jnp.max(tile_max[c], axis=0, keepdims=True))
            alpha = jnp.exp(ms[c] - m_new)
            p = jnp.exp(s_ref[c] - m_new).astype(BF16)
            for hh in range(2):
                h = 2 * c + hh
                rows = slice(h * ACC_ROWS, (h + 1) * ACC_ROWS)
                vt_aug = jnp.concatenate([vt_ref[h * DSA_HD:(h + 1) * DSA_HD, pl.ds(ks, ATILE)], ones_rows], axis=0)
                pv = _dot(vt_aug, p[:, hh * QBLK:(hh + 1) * QBLK])
                acc_ref[rows, :] = alpha[:, hh * QBLK:(hh + 1) * QBLK] * acc_ref[rows, :] + pv
            new_ms.append(m_new)
        return tuple(new_ms)

    def attn_tile(t, carry):
        max_a, ms = carry
        k0 = pl.multiple_of(t * KTILE, KTILE)
        k1 = pl.multiple_of(t * KTILE + ATILE, ATILE)
        k2 = pl.multiple_of(t * KTILE + KTILE, KTILE)
        max_b = qk_stage(k1, sb_ref)
        ms = pv_stage(k0, sa_ref, max_a, ms)
        max_a = qk_stage(k2, sa_ref)
        ms = pv_stage(k1, sb_ref, max_b, ms)
        return max_a, ms

    ms = tuple(jnp.full((1, 2 * QBLK), NEG_BIG, F32) for _ in range(n_pairs))
    max_a, ms = lax.fori_loop(0, last, attn_tile, (qk_stage(0, sa_ref), ms))
    k0 = pl.multiple_of(last * KTILE, KTILE)
    k1 = pl.multiple_of(last * KTILE + ATILE, ATILE)
    max_b = qk_stage(k1, sb_ref)
    ms = pv_stage(k0, sa_ref, max_a, ms)
    pv_stage(k1, sb_ref, max_b, ms)
    outs = []
    for h in range(DSA_HEADS):
        a = acc_ref[h * ACC_ROWS:(h + 1) * ACC_ROWS, :]
        outs.append(a[:DSA_HD, :] / a[DSA_HD:DSA_HD + 1, :])
    o_ref[...] = jnp.concatenate(outs, axis=0).T.astype(o_ref.dtype)


def _dsa_kernel(qt_ref, iqt_ref, iwt_ref, k_ref, vt_ref, ik_ref, o_ref, sc_ref, bias_ref, acc_ref, rhs_ref,
                sa_ref, sb_ref, gm_ref, *, k_sel, l_tok):
    has_tokens = pl.program_id(0) * QBLK < l_tok

    @pl.when(has_tokens)
    def _():
        _dsa_block(qt_ref, iqt_ref, iwt_ref, k_ref, vt_ref, ik_ref, o_ref, sc_ref, bias_ref, acc_ref, rhs_ref,
                   sa_ref, sb_ref, gm_ref, k_sel)

    @pl.when(jnp.logical_not(has_tokens))
    def _():
        o_ref[...] = jnp.zeros(o_ref.shape, o_ref.dtype)


def _dsa(dqt, dk, dvt, iqt, ik, iwt, k_sel, l_tok):
    lp = dk.shape[0]

    def col(height):
        return pl.BlockSpec((height, QBLK), lambda i: (0, i))

    return pl.pallas_call(
        functools.partial(_dsa_kernel, k_sel=k_sel, l_tok=l_tok),
        grid=(lp // QBLK,),
        in_specs=[col(DSA_WIDTH), col(IDX_WIDTH), col(LANES), _const_spec((lp, DSA_WIDTH)),
                  _const_spec((DSA_WIDTH, lp)), _const_spec((lp, IDX_HD))],
        out_specs=pl.BlockSpec((QBLK, DSA_WIDTH), lambda i: (i, 0)),
        out_shape=jax.ShapeDtypeStruct((lp, DSA_WIDTH), BF16),
        scratch_shapes=[pltpu.VMEM((lp, QBLK), F32), pltpu.VMEM((lp, QBLK), BF16),
                        pltpu.VMEM((DSA_HEADS * ACC_ROWS, QBLK), F32),
                        pltpu.VMEM((DSA_HEADS // 2, 2 * LANES, 2 * QBLK), BF16),
                        pltpu.VMEM((DSA_HEADS // 2, ATILE, 2 * QBLK), F32),
                        pltpu.VMEM((DSA_HEADS // 2, ATILE, 2 * QBLK), F32),
                        pltpu.VMEM((N_GROUPS, QBLK), F32)],
        compiler_params=pltpu.CompilerParams(dimension_semantics=("parallel",),
                                             vmem_limit_bytes=VMEM_LIMIT),
        name="dsa",
    )(dqt, iqt, iwt, dk, dvt, ik)


def _out_ffn2_kernel(h_ref, og_ref, od_ref, wo_ref, norm_ref, wg_ref, wu_ref, wd_ref, fnorm_ref, o_ref, *,
                     ff_chunk):
    h2 = h_ref[...] + _dot(og_ref[...], wo_ref[:GLA_WIDTH, :]) + _dot(od_ref[...], wo_ref[GLA_WIDTH:, :])
    h3 = _swiglu_half(h2, norm_ref, wg_ref, wu_ref, wd_ref, ff_chunk)
    o_ref[...] = _rmsnorm(h3, fnorm_ref[...])


def _out_ffn2(h, o_gla, o_dsa, wo, norm, wg, wu, wd, fnorm):
    lp, d = h.shape
    d_ff = wg.shape[1]

    def row(width):
        return pl.BlockSpec((ROW_TILE, width), lambda i: (i, 0))

    return pl.pallas_call(
        functools.partial(_out_ffn2_kernel, ff_chunk=_ffn_chunk(d_ff)),
        grid=(lp // ROW_TILE,),
        in_specs=[row(d), row(GLA_WIDTH), row(DSA_WIDTH), _const_spec((GLA_WIDTH + DSA_WIDTH, d)),
                  _const_spec((1, d)), _const_spec((d, d_ff)), _const_spec((d, d_ff)), _const_spec((d_ff, d)),
                  _const_spec((1, d))],
        out_specs=row(d),
        out_shape=jax.ShapeDtypeStruct((lp, d), F32),
        compiler_params=pltpu.CompilerParams(dimension_semantics=("parallel",),
                                             vmem_limit_bytes=VMEM_LIMIT),
        name="out_ffn2",
    )(h, o_gla, o_dsa, wo, norm, wg, wu, wd, fnorm)


def _rope_tables(lp):
    pos = jnp.arange(lp, dtype=F32)
    inv_freq = 1.0 / (ROPE_THETA ** (jnp.arange(0, DSA_HD, 2, dtype=F32) / DSA_HD))
    ang = pos[:, None] * inv_freq[None, :]
    cos = jnp.cos(ang)
    sin = jnp.sin(ang)
    cos2 = jnp.concatenate([cos, cos, cos, cos], axis=-1)
    sin2s = jnp.concatenate([-sin, sin, -sin, sin], axis=-1)
    return cos2, sin2s


def _layer(h, k_sel, l_tok, ffn1_norm, ffn1_w_gate, ffn1_w_up, ffn1_w_down, mix_norm, w_in, gla_w_a2, gla_b_a,
           gla_head_norm, w_out, ffn2_norm, ffn2_w_gate, ffn2_w_up, ffn2_w_down, final_norm, cos2, sin2s):
    d = h.shape[1]
    h1 = _ffn1(h, ffn1_norm.reshape(1, d), ffn1_w_gate.astype(BF16), ffn1_w_up.astype(BF16),
               ffn1_w_down.astype(BF16))

    sizes = (GLA_QK, GLA_QK, GLA_WIDTH, GLA_WIDTH, GLA_LOWRANK, DSA_WIDTH, DSA_WIDTH, DSA_WIDTH, IDX_WIDTH,
             IDX_HD, IDX_HEADS)
    offs = np.concatenate([[0], np.cumsum(sizes)])
    col = lambda n: w_in[:, offs[n]:offs[n + 1]]
    pad = jnp.zeros((d, LANES - IDX_HD - IDX_HEADS - GLA_LOWRANK), w_in.dtype)
    w_all = jnp.concatenate([col(0), col(1), col(2), col(3), col(5), col(6), col(7), col(8),
                             col(9), col(10), col(4), pad], axis=1).astype(BF16)
    wa2p = jnp.zeros((LANES, GLA_QK), F32).at[MISC_GA:MISC_GA + GLA_LOWRANK, :].set(gla_w_a2).astype(BF16)

    gqk, gv, gr, gg, dqt, dk, dvt, iqt, ik, iwt = _in_proj(
        h1, mix_norm.reshape(1, d), w_all, wa2p, gla_b_a.reshape(1, GLA_QK), cos2, sin2s)
    o_gla = _gla(gqk, gv, gr, gg, gla_head_norm.reshape(1, GLA_DV))
    o_dsa = _dsa(dqt, dk, dvt, iqt, ik, iwt, k_sel, l_tok)
    return _out_ffn2(h1, o_gla, o_dsa, w_out.astype(BF16), ffn2_norm.reshape(1, d), ffn2_w_gate.astype(BF16),
                     ffn2_w_up.astype(BF16), ffn2_w_down.astype(BF16), final_norm.reshape(1, d))


def kernel(x, meta_tokens, ffn1_norm, ffn1_w_gate, ffn1_w_up, ffn1_w_down, mix_norm, w_in, gla_w_a2, gla_b_a,
           gla_head_norm, w_out, ffn2_norm, ffn2_w_gate, ffn2_w_up, ffn2_w_down, final_norm):
    batch, seq, d = x.shape
    depth = ffn1_norm.shape[0]
    assert depth == 1, "the final norm is fused into the layer's last kernel"
    k_sel = min(TOPK_MAX, seq // 4)
    l_tok = seq + N_META
    lp = -(-l_tok // ROW_PAD) * ROW_PAD
    cos2, sin2s = _rope_tables(lp)
    outs = []
    for bi in range(batch):
        h = jnp.concatenate([meta_tokens.astype(x.dtype), x[bi], jnp.zeros((lp - l_tok, d), x.dtype)], axis=0)
        h = _layer(h, k_sel, l_tok, ffn1_norm[0], ffn1_w_gate[0], ffn1_w_up[0], ffn1_w_down[0], mix_norm[0],
                   w_in[0], gla_w_a2[0], gla_b_a[0], gla_head_norm[0], w_out[0], ffn2_norm[0], ffn2_w_gate[0],
                   ffn2_w_up[0], ffn2_w_down[0], final_norm, cos2, sin2s)
        outs.append(h[N_META:l_tok])
    return jnp.stack(outs, axis=0)
```

```python
import functools

import numpy as np
import jax
import jax.numpy as jnp
from jax import lax
from jax.experimental import pallas as pl
from jax.experimental.pallas import tpu as pltpu

N_META = 16
EPS = 1e-6
ROPE_THETA = 10000.0
GLA_HEADS = 4
GLA_DK = 64
GLA_DV = 128
GLA_LOWRANK = 16
GLA_GATE_NORM = 16.0
DSA_HEADS = 8
DSA_HD = 64
IDX_HEADS = 8
IDX_HD = 64
TOPK_MAX = 256

GLA_QK = GLA_HEADS * GLA_DK
GLA_WIDTH = GLA_HEADS * GLA_DV
DSA_WIDTH = DSA_HEADS * DSA_HD
IDX_WIDTH = IDX_HEADS * IDX_HD

LANES = 128
SUBLANES = 8
ROW_TILE = 384
QBLK = 128
KSUB = 128
KTILE = 512
ATILE = KTILE // 2
N_GROUPS = 2 * KSUB
UNCHECKED_HALVINGS = 12
GLA_STEP = 128
GLA_SUB = 16
ROW_PAD = 1536
ACC_ROWS = DSA_HD + 16
VMEM_LIMIT = 60 * 1024 * 1024
NEG_BIG = -1e30

MISC_IK = 0
MISC_IW = IDX_HD
MISC_GA = IDX_HD + IDX_HEADS

F32 = jnp.float32
BF16 = jnp.bfloat16


def _dot(a, b):
    return jnp.dot(a, b, preferred_element_type=F32)


def _rmsnorm(x, g):
    return x * lax.rsqrt(jnp.mean(x * x, axis=-1, keepdims=True) + EPS) * g


def _const_spec(shape):
    return pl.BlockSpec(shape, lambda *_: (0,) * len(shape), pipeline_mode=pl.Buffered(1))


def _swiglu_half(x, norm_ref, wg_ref, wu_ref, wd_ref, ff_chunk):
    hn = _rmsnorm(x, norm_ref[...]).astype(BF16)
    d_ff = wg_ref.shape[1]
    acc = jnp.zeros(x.shape, F32)
    for c in range(d_ff // ff_chunk):
        sl = slice(c * ff_chunk, (c + 1) * ff_chunk)
        g = _dot(hn, wg_ref[:, sl])
        u = _dot(hn, wu_ref[:, sl])
        a = (g * jax.nn.sigmoid(g) * u).astype(BF16)
        acc = acc + _dot(a, wd_ref[sl, :])
    return x + 0.5 * acc


def _ffn1_kernel(h_ref, norm_ref, wg_ref, wu_ref, wd_ref, o_ref, *, ff_chunk):
    o_ref[...] = _swiglu_half(h_ref[...], norm_ref, wg_ref, wu_ref, wd_ref, ff_chunk)


def _ffn_chunk(d_ff):
    for c in (1408, 1024, 512, 256, 128):
        if d_ff % c == 0:
            return c
    return d_ff


def _ffn1(h, norm, wg, wu, wd):
    lp, d = h.shape
    d_ff = wg.shape[1]
    row = pl.BlockSpec((ROW_TILE, d), lambda i: (i, 0))
    return pl.pallas_call(
        functools.partial(_ffn1_kernel, ff_chunk=_ffn_chunk(d_ff)),
        grid=(lp // ROW_TILE,),
        in_specs=[row, _const_spec((1, d)), _const_spec((d, d_ff)), _const_spec((d, d_ff)),
                  _const_spec((d_ff, d))],
        out_specs=row,
        out_shape=jax.ShapeDtypeStruct((lp, d), F32),
        compiler_params=pltpu.CompilerParams(dimension_semantics=("parallel",),
                                             vmem_limit_bytes=VMEM_LIMIT),
        name="ffn1",
    )(h, norm, wg, wu, wd)


def _rope(x, cos, sin_signed, first_half):
    parts = []
    for c in range(x.shape[1] // LANES):
        xc = x[:, c * LANES:(c + 1) * LANES]
        rot = jnp.where(first_half, pltpu.roll(xc, LANES - 32, 1), pltpu.roll(xc, 32, 1))
        parts.append(xc * cos + rot * sin_signed)
    return parts[0] if len(parts) == 1 else jnp.concatenate(parts, axis=1)


def _log_sigmoid(x):
    return jnp.minimum(x, 0.0) - jnp.log(1.0 + jnp.exp(-jnp.abs(x)))


def _in_proj_kernel(h_ref, norm_ref, w_ref, wa2_ref, ba_ref, cos_ref, sin_ref,
                    gqk_ref, gv_ref, gr_ref, gg_ref, dqt_ref, dk_ref, dvt_ref, iqt_ref, ik_ref, iwt_ref):
    hn = _rmsnorm(h_ref[...], norm_ref[...]).astype(BF16)
    cos = cos_ref[...]
    sin_s = sin_ref[...]
    lane = lax.broadcasted_iota(jnp.int32, cos.shape, 1)
    first_half = (lane % DSA_HD) < (DSA_HD // 2)

    def proj(c0, width):
        return _dot(hn, w_ref[:, c0:c0 + width])

    c = 0
    gqk_ref[...] = proj(c, 2 * GLA_QK); c += 2 * GLA_QK
    gv_ref[...] = proj(c, GLA_WIDTH); c += GLA_WIDTH
    gr_ref[...] = proj(c, GLA_WIDTH); c += GLA_WIDTH
    dqt_ref[...] = (_rope(proj(c, DSA_WIDTH), cos, sin_s, first_half) * (DSA_HD ** -0.5)).T.astype(BF16)
    c += DSA_WIDTH
    dk_ref[...] = _rope(proj(c, DSA_WIDTH), cos, sin_s, first_half).astype(BF16); c += DSA_WIDTH
    dvt_ref[...] = proj(c, DSA_WIDTH).T.astype(BF16); c += DSA_WIDTH
    iqt_ref[...] = (_rope(proj(c, IDX_WIDTH), cos, sin_s, first_half) * (IDX_HD ** -0.5)).T.astype(BF16)
    c += IDX_WIDTH
    misc = proj(c, LANES)
    ik_ref[...] = _rope(misc, cos, sin_s, first_half)[:, MISC_IK:MISC_IK + IDX_HD].astype(BF16)
    iwt_ref[...] = (misc * (IDX_HEADS ** -0.5)).T
    pre = _dot(misc.astype(BF16), wa2_ref[...]) + ba_ref[...]
    gg_ref[...] = _log_sigmoid(pre) * (1.0 / GLA_GATE_NORM)


def _in_proj(h, norm, w_all, wa2p, ba, cos2, sin2s):
    lp, d = h.shape
    ncols = w_all.shape[1]

    def row(width):
        return pl.BlockSpec((ROW_TILE, width), lambda i: (i, 0))

    def col(height):
        return pl.BlockSpec((height, ROW_TILE), lambda i: (0, i))

    outs = [
        (row(2 * GLA_QK), (lp, 2 * GLA_QK), F32), (row(GLA_WIDTH), (lp, GLA_WIDTH), F32),
        (row(GLA_WIDTH), (lp, GLA_WIDTH), F32), (row(GLA_QK), (lp, GLA_QK), F32),
        (col(DSA_WIDTH), (DSA_WIDTH, lp), BF16), (row(DSA_WIDTH), (lp, DSA_WIDTH), BF16),
        (col(DSA_WIDTH), (DSA_WIDTH, lp), BF16), (col(IDX_WIDTH), (IDX_WIDTH, lp), BF16),
        (row(IDX_HD), (lp, IDX_HD), BF16), (col(LANES), (LANES, lp), F32),
    ]
    return pl.pallas_call(
        _in_proj_kernel,
        grid=(lp // ROW_TILE,),
        in_specs=[row(d), _const_spec((1, d)), _const_spec((d, ncols)), _const_spec((LANES, GLA_QK)),
                  _const_spec((1, GLA_QK)), row(LANES), row(LANES)],
        out_specs=[spec for spec, _, _ in outs],
        out_shape=[jax.ShapeDtypeStruct(shape, dt) for _, shape, dt in outs],
        compiler_params=pltpu.CompilerParams(dimension_semantics=("parallel",),
                                             vmem_limit_bytes=VMEM_LIMIT),
        name="in_proj",
    )(h, norm, w_all, wa2p, ba, cos2, sin2s)


def _gla_kernel(qk_ref, v_ref, r_ref, g_ref, hnorm_ref, o_ref, s_ref):
    n_sub = GLA_STEP // GLA_SUB

    @pl.when(pl.program_id(0) == 0)
    def _():
        s_ref[...] = jnp.zeros(s_ref.shape, F32)

    q = qk_ref[:, :GLA_QK] * (GLA_DK ** -0.5)
    k = qk_ref[:, GLA_QK:]
    v = v_ref[...]
    g = g_ref[...]

    ri = lax.broadcasted_iota(jnp.int32, (GLA_STEP, GLA_STEP), 0)
    ci = lax.broadcasted_iota(jnp.int32, (GLA_STEP, GLA_STEP), 1)
    tri = ((ri // GLA_SUB == ci // GLA_SUB) & (ci <= ri)).astype(F32)
    b = jnp.dot(tri, g, preferred_element_type=F32, precision=lax.Precision.HIGHEST)

    b3 = b.reshape(n_sub, GLA_SUB, GLA_QK)
    k3 = k.reshape(n_sub, GLA_SUB, GLA_QK)
    v3 = v.reshape(n_sub, GLA_SUB, GLA_WIDTH)
    b_last = jnp.broadcast_to(b3[:, GLA_SUB - 1:GLA_SUB, :], b3.shape).reshape(GLA_STEP, GLA_QK)

    pos = lax.broadcasted_iota(jnp.int32, (GLA_STEP, GLA_QK), 0) % GLA_SUB
    t_rows = []
    for j in range(GLA_SUB):
        kj = jnp.broadcast_to(k3[:, j:j + 1, :], k3.shape).reshape(GLA_STEP, GLA_QK)
        bj = jnp.broadcast_to(b3[:, j:j + 1, :], b3.shape).reshape(GLA_STEP, GLA_QK)
        e = jnp.where(pos >= j, b - bj, -jnp.inf)
        t_rows.append((q * kj * jnp.exp(e)).astype(BF16))
    t = jnp.concatenate(t_rows, axis=0)
    hr = lax.broadcasted_iota(jnp.int32, (GLA_QK, GLA_WIDTH), 0) // GLA_DK
    hc = lax.broadcasted_iota(jnp.int32, (GLA_QK, GLA_WIDTH), 1) // GLA_DV
    head_sum = (hr == hc).astype(BF16)
    p = _dot(t, head_sum)
    o = jnp.zeros((GLA_STEP, GLA_WIDTH), F32)
    for j in range(GLA_SUB):
        vj = jnp.broadcast_to(v3[:, j:j + 1, :], v3.shape).reshape(GLA_STEP, GLA_WIDTH)
        o = o + p[j * GLA_STEP:(j + 1) * GLA_STEP, :] * vj

    qe = q * jnp.exp(b)
    kd = k * jnp.exp(b_last - b)
    kd_t = kd.T
    dec_t = jnp.exp(b).T
    lane_head = lax.broadcasted_iota(jnp.int32, (GLA_SUB, GLA_QK), 1) // GLA_DK
    s = s_ref[...]
    o_inter = []
    for c in range(n_sub):
        rows = slice(c * GLA_SUB, (c + 1) * GLA_SUB)
        qe_c = qe[rows, :]
        q_stack = jnp.concatenate(
            [jnp.where(lane_head == h, qe_c, 0.0) for h in range(GLA_HEADS)], axis=0).astype(BF16)
        r_stack = _dot(q_stack, s.astype(BF16))
        o_inter.append(jnp.concatenate(
            [r_stack[h * GLA_SUB:(h + 1) * GLA_SUB, :] for h in range(GLA_HEADS)], axis=1))
        kd_c = kd_t[:, rows].astype(BF16)
        v_c = v[rows, :].astype(BF16)
        u_c = jnp.concatenate(
            [_dot(kd_c[h * GLA_DK:(h + 1) * GLA_DK, :], v_c[:, h * GLA_DV:(h + 1) * GLA_DV])
             for h in range(GLA_HEADS)], axis=0)
        last = c * GLA_SUB + GLA_SUB - 1
        s = dec_t[:, last:last + 1] * s + u_c
    s_ref[...] = s
    o = o + jnp.concatenate(o_inter, axis=0)

    r = r_ref[...]
    gate = r * jax.nn.sigmoid(r)
    hn = hnorm_ref[...]
    outs = []
    for h in range(GLA_HEADS):
        oh = o[:, h * GLA_DV:(h + 1) * GLA_DV]
        outs.append(_rmsnorm(oh, hn))
    o_ref[...] = (jnp.concatenate(outs, axis=1) * gate).astype(o_ref.dtype)


def _gla(gqk, gv, gr, gg, hnorm):
    lp = gqk.shape[0]

    def row(width):
        return pl.BlockSpec((GLA_STEP, width), lambda i: (i, 0))

    return pl.pallas_call(
        _gla_kernel,
        grid=(lp // GLA_STEP,),
        in_specs=[row(2 * GLA_QK), row(GLA_WIDTH), row(GLA_WIDTH), row(GLA_QK), _const_spec((1, GLA_DV))],
        out_specs=row(GLA_WIDTH),
        out_shape=jax.ShapeDtypeStruct((lp, GLA_WIDTH), BF16),
        scratch_shapes=[pltpu.VMEM((GLA_QK, GLA_DV), F32)],
        compiler_params=pltpu.CompilerParams(dimension_semantics=("arbitrary",),
                                             vmem_limit_bytes=VMEM_LIMIT),
        name="gla",
    )(gqk, gv, gr, gg, hnorm)


def _key_to_float(u):
    bits = jnp.where(u >= 0, u, u ^ jnp.int32(0x7FFFFFFF))
    return lax.bitcast_convert_type(bits, F32)


def _float_to_key(f):
    bits = lax.bitcast_convert_type(f, jnp.int32)
    return jnp.where(bits >= 0, bits, bits ^ jnp.int32(0x7FFFFFFF))


_KEY_NEG_INF = np.int32(np.uint32(0xFF800000) ^ np.uint32(0x7FFFFFFF))


def _tile_loop(n, body, carry, unroll):
    trips = n // unroll

    def group(u, c):
        for r in range(unroll):
            c = body(u * unroll + r, c)
        return c

    carry = lax.fori_loop(0, trips, group, carry)
    return lax.fori_loop(trips * unroll, n, body, carry)


def _fold_rows(x, op):
    parts = [x[r:r + SUBLANES, :] for r in range(0, x.shape[0], SUBLANES)]
    while len(parts) > 1:
        parts = [op(parts[a], parts[a + 1]) if a + 1 < len(parts) else parts[a] for a in range(0, len(parts), 2)]
    return parts[0]


def _dsa_block(qt_ref, iqt_ref, iwt_ref, k_ref, vt_ref, ik_ref, o_ref, sc_ref, bias_ref, acc_ref, rhs_ref,
               sa_ref, sb_ref, gm_ref, k_sel):
    i = pl.program_id(0)
    n_tiles = (i * QBLK + QBLK + KTILE - 1) // KTILE
    last = n_tiles - 1
    q_pos = i * QBLK + lax.broadcasted_iota(jnp.int32, (KSUB, QBLK), 1)
    key_row = lax.broadcasted_iota(jnp.int32, (KSUB, QBLK), 0)

    def visible(ks):
        return (ks + key_row) <= q_pos

    def peeled(body, carry):
        carry = _tile_loop(last, functools.partial(body, masked=False), carry, 4)
        return body(last, carry, masked=True)

    iqt = iqt_ref[...]
    iq_all = jnp.concatenate([iqt[h * IDX_HD:(h + 1) * IDX_HD, :] for h in range(IDX_HEADS)], axis=1)
    iw = iwt_ref[MISC_IW:MISC_IW + IDX_HEADS, :]
    gm_ref[...] = jnp.full(gm_ref.shape, -jnp.inf, F32)

    def score_tile(t, carry, masked):
        for sub in range(KTILE // KSUB):
            ks = pl.multiple_of(t * KTILE + sub * KSUB, KSUB)
            lg = _dot(ik_ref[pl.ds(ks, KSUB), :], iq_all)
            sc = iw[0:1, :] * jnp.maximum(lg[:, 0:QBLK], 0.0)
            for h in range(1, IDX_HEADS):
                sc = sc + iw[h:h + 1, :] * jnp.maximum(lg[:, h * QBLK:(h + 1) * QBLK], 0.0)
            if masked:
                sc = jnp.where(visible(ks), sc, -jnp.inf)
            sc_ref[pl.ds(ks, KSUB), :] = sc
            grp = slice((sub % 2) * KSUB, (sub % 2 + 1) * KSUB)
            gm_ref[grp, :] = jnp.maximum(gm_ref[grp, :], sc)
        return carry

    peeled(score_tile, 0)

    kk = float(k_sel)

    def count(pred):
        def body(t, acc):
            ks = pl.multiple_of(t * KTILE, KTILE)
            return acc + _fold_rows(jnp.where(pred(sc_ref[pl.ds(ks, KTILE), :]), 1.0, 0.0), jnp.add)
        acc = _tile_loop(n_tiles, body, jnp.zeros((SUBLANES, QBLK), F32), 4)
        return jnp.sum(acc, axis=0, keepdims=True)

    def count_ge(cand):
        return count(lambda x: x >= cand)

    gm = gm_ref[...]
    lo = jnp.maximum(_float_to_key(jnp.min(_fold_rows(gm, jnp.minimum), axis=0, keepdims=True)), _KEY_NEG_INF)
    hi = _float_to_key(jnp.max(_fold_rows(gm, jnp.maximum), axis=0, keepdims=True))
    n_lo = count_ge(_key_to_float(lo))

    def settled(lo, hi, n_lo):
        return jnp.logical_or(n_lo <= kk, lo >= hi)

    def any_open(lo, hi, n_lo):
        return jnp.max(jnp.where(settled(lo, hi, n_lo), 0.0, 1.0))

    def halve(lo, hi, n_lo):
        open_ = jnp.logical_not(settled(lo, hi, n_lo))
        mid = (lo | hi) - ((lo ^ hi) >> 1)
        cnt = count_ge(_key_to_float(mid))
        up = jnp.logical_and(open_, cnt >= kk)
        down = jnp.logical_and(open_, cnt < kk)
        return jnp.where(up, mid, lo), jnp.where(down, mid - 1, hi), jnp.where(up, cnt, n_lo)

    def checked_halve(st):
        lo, hi, n_lo = halve(*st[:3])
        return lo, hi, n_lo, any_open(lo, hi, n_lo), st[4] + 1

    lo, hi, n_lo = lax.fori_loop(0, UNCHECKED_HALVINGS, lambda _, st: halve(*st), (lo, hi, n_lo))
    lo, hi, n_lo, _, _ = lax.while_loop(lambda st: jnp.logical_and(st[3] > 0.0, st[4] < 33), checked_halve,
                                        (lo, hi, n_lo, any_open(lo, hi, n_lo), jnp.int32(0)))
    few = n_lo < kk
    lo_f = _key_to_float(lo)

    def min_ge_body(t, acc):
        ks = pl.multiple_of(t * KTILE, KTILE)
        x = sc_ref[pl.ds(ks, KTILE), :]
        return jnp.minimum(acc, _fold_rows(jnp.where(x >= lo_f, x, jnp.inf), jnp.minimum))

    thr = jnp.min(_tile_loop(n_tiles, min_ge_body, jnp.full((SUBLANES, QBLK), jnp.inf, F32), 4),
                  axis=0, keepdims=True)
    thr = jnp.where(few, -jnp.inf, thr)
    has_ties = jnp.max(jnp.where(few, 0.0, n_lo)) > kk

    def bias_fast(t, carry, masked):
        for sub in range(KTILE // KSUB):
            ks = pl.multiple_of(t * KTILE + sub * KSUB, KSUB)
            keep = sc_ref[pl.ds(ks, KSUB), :] >= thr
            if masked:
                keep = jnp.logical_and(keep, visible(ks))
            bias_ref[pl.ds(ks, KSUB), :] = jnp.where(keep, 0.0, NEG_BIG).astype(BF16)
        return carry

    def bias_ties(t, run, masked, budget, tri):
        for sub in range(KTILE // KSUB):
            ks = pl.multiple_of(t * KTILE + sub * KSUB, KSUB)
            x = sc_ref[pl.ds(ks, KSUB), :]
            above = x > thr
            tie = x == thr
            tie_f = jnp.where(tie, 1.0, 0.0)
            rank = run + _dot(tri, tie_f.astype(BF16))
            keep = jnp.logical_or(above, jnp.logical_and(tie, rank <= budget))
            if masked:
                keep = jnp.logical_and(keep, visible(ks))
            bias_ref[pl.ds(ks, KSUB), :] = jnp.where(keep, 0.0, NEG_BIG).astype(BF16)
            run = run + jnp.sum(tie_f, axis=0, keepdims=True)
        return run

    @pl.when(jnp.logical_not(has_ties))
    def _():
        peeled(bias_fast, 0)

    @pl.when(has_ties)
    def _():
        budget = kk - count(lambda x: x > thr)
        tri = (lax.broadcasted_iota(jnp.int32, (KSUB, KSUB), 1) <= key_row).astype(BF16)
        peeled(functools.partial(bias_ties, budget=budget, tri=tri), jnp.zeros((1, QBLK), F32))

    qt = qt_ref[...]
    eye = (lax.broadcasted_iota(jnp.int32, (QBLK, QBLK), 0)
           == lax.broadcasted_iota(jnp.int32, (QBLK, QBLK), 1)).astype(BF16)
    zero = jnp.zeros((DSA_HD, QBLK), BF16)
    n_pairs = DSA_HEADS // 2
    for c in range(n_pairs):
        qa = qt[(2 * c) * DSA_HD:(2 * c + 1) * DSA_HD, :]
        qb = qt[(2 * c + 1) * DSA_HD:(2 * c + 2) * DSA_HD, :]
        rhs_ref[c] = jnp.concatenate([jnp.concatenate([qa, zero], axis=1),
                                      jnp.concatenate([zero, qb], axis=1),
                                      jnp.concatenate([eye, eye], axis=1)], axis=0)
    acc_ref[...] = jnp.zeros(acc_ref.shape, F32)
    ones_rows = jnp.ones((ACC_ROWS - DSA_HD, ATILE), BF16)

    def qk_stage(ks, s_ref):
        bias = bias_ref[pl.ds(ks, ATILE), :]
        tile_max = []
        for c in range(n_pairs):
            lhs = jnp.concatenate([k_ref[pl.ds(ks, ATILE), c * 2 * DSA_HD:(c + 1) * 2 * DSA_HD], bias], axis=1)
            s = _dot(lhs, rhs_ref[c])
            s_ref[c] = s
            tile_max.append(_fold_rows(s, jnp.maximum))
        return tuple(tile_max)

    def pv_stage(ks, s_ref, tile_max, ms):
        new_ms = []
        for c in range(n_pairs):
            m_new = jnp.maximum(ms[c], jnp.max(tile_max[c], axis=0, keepdims=True))
            alpha = jnp.exp(ms[c] - m_new)
            p = jnp.exp(s_ref[c] - m_new).astype(BF16)
            for hh in range(2):
                h = 2 * c + hh
                rows = slice(h * ACC_ROWS, (h + 1) * ACC_ROWS)
                vt_aug = jnp.concatenate([vt_ref[h * DSA_HD:(h + 1) * DSA_HD, pl.ds(ks, ATILE)], ones_rows], axis=0)
                pv = _dot(vt_aug, p[:, hh * QBLK:(hh + 1) * QBLK])
                acc_ref[rows, :] = alpha[:, hh * QBLK:(hh + 1) * QBLK] * acc_ref[rows, :] + pv
            new_ms.append(m_new)
        return tuple(new_ms)

    def attn_tile(t, carry):
        max_a, ms = carry
        k0 = pl.multiple_of(t * KTILE, KTILE)
        k1 = pl.multiple_of(t * KTILE + ATILE, ATILE)
        k2 = pl.multiple_of(t * KTILE + KTILE, KTILE)
        max_b = qk_stage(k1, sb_ref)
        ms = pv_stage(k0, sa_ref, max_a, ms)
        max_a = qk_stage(k2, sa_ref)
        ms = pv_stage(k1, sb_ref, max_b, ms)
        return max_a, ms

    ms = tuple(jnp.full((1, 2 * QBLK), NEG_BIG, F32) for _ in range(n_pairs))
    max_a, ms = _tile_loop(last, attn_tile, (qk_stage(0, sa_ref), ms), 4)
    k0 = pl.multiple_of(last * KTILE, KTILE)
    k1 = pl.multiple_of(last * KTILE + ATILE, ATILE)
    max_b = qk_stage(k1, sb_ref)
    ms = pv_stage(k0, sa_ref, max_a, ms)
    pv_stage(k1, sb_ref, max_b, ms)
    outs = []
    for h in range(DSA_HEADS):
        a = acc_ref[h * ACC_ROWS:(h + 1) * ACC_ROWS, :]
        outs.append(a[:DSA_HD, :] / a[DSA_HD:DSA_HD + 1, :])
    o_ref[...] = jnp.concatenate(outs, axis=0).T.astype(o_ref.dtype)


def _dsa_kernel(qt_ref, iqt_ref, iwt_ref, k_ref, vt_ref, ik_ref, o_ref, sc_ref, bias_ref, acc_ref, rhs_ref,
                sa_ref, sb_ref, gm_ref, *, k_sel, l_tok):
    has_tokens = pl.program_id(0) * QBLK < l_tok

    @pl.when(has_tokens)
    def _():
        _dsa_block(qt_ref, iqt_ref, iwt_ref, k_ref, vt_ref, ik_ref, o_ref, sc_ref, bias_ref, acc_ref, rhs_ref,
                   sa_ref, sb_ref, gm_ref, k_sel)

    @pl.when(jnp.logical_not(has_tokens))
    def _():
        o_ref[...] = jnp.zeros(o_ref.shape, o_ref.dtype)


def _dsa(dqt, dk, dvt, iqt, ik, iwt, k_sel, l_tok):
    lp = dk.shape[0]

    def col(height):
        return pl.BlockSpec((height, QBLK), lambda i: (0, i))

    return pl.pallas_call(
        functools.partial(_dsa_kernel, k_sel=k_sel, l_tok=l_tok),
        grid=(lp // QBLK,),
        in_specs=[col(DSA_WIDTH), col(IDX_WIDTH), col(LANES), _const_spec((lp, DSA_WIDTH)),
                  _const_spec((DSA_WIDTH, lp)), _const_spec((lp, IDX_HD))],
        out_specs=pl.BlockSpec((QBLK, DSA_WIDTH), lambda i: (i, 0)),
        out_shape=jax.ShapeDtypeStruct((lp, DSA_WIDTH), BF16),
        scratch_shapes=[pltpu.VMEM((lp, QBLK), F32), pltpu.VMEM((lp, QBLK), BF16),
                        pltpu.VMEM((DSA_HEADS * ACC_ROWS, QBLK), F32),
                        pltpu.VMEM((DSA_HEADS // 2, 2 * LANES, 2 * QBLK), BF16),
                        pltpu.VMEM((DSA_HEADS // 2, ATILE, 2 * QBLK), F32),
                        pltpu.VMEM((DSA_HEADS // 2, ATILE, 2 * QBLK), F32),
                        pltpu.VMEM((N_GROUPS, QBLK), F32)],
        compiler_params=pltpu.CompilerParams(dimension_semantics=("parallel",),
                                             vmem_limit_bytes=VMEM_LIMIT),
        name="dsa",
    )(dqt, iqt, iwt, dk, dvt, ik)


def _out_ffn2_kernel(h_ref, og_ref, od_ref, wo_ref, norm_ref, wg_ref, wu_ref, wd_ref, fnorm_ref, o_ref, *,
                     ff_chunk):
    h2 = h_ref[...] + _dot(og_ref[...], wo_ref[:GLA_WIDTH, :]) + _dot(od_ref[...], wo_ref[GLA_WIDTH:, :])
    h3 = _swiglu_half(h2, norm_ref, wg_ref, wu_ref, wd_ref, ff_chunk)
    o_ref[...] = _rmsnorm(h3, fnorm_ref[...])


def _out_ffn2(h, o_gla, o_dsa, wo, norm, wg, wu, wd, fnorm):
    lp, d = h.shape
    d_ff = wg.shape[1]

    def row(width):
        return pl.BlockSpec((ROW_TILE, width), lambda i: (i, 0))

    return pl.pallas_call(
        functools.partial(_out_ffn2_kernel, ff_chunk=_ffn_chunk(d_ff)),
        grid=(lp // ROW_TILE,),
        in_specs=[row(d), row(GLA_WIDTH), row(DSA_WIDTH), _const_spec((GLA_WIDTH + DSA_WIDTH, d)),
                  _const_spec((1, d)), _const_spec((d, d_ff)), _const_spec((d, d_ff)), _const_spec((d_ff, d)),
                  _const_spec((1, d))],
        out_specs=row(d),
        out_shape=jax.ShapeDtypeStruct((lp, d), F32),
        compiler_params=pltpu.CompilerParams(dimension_semantics=("parallel",),
                                             vmem_limit_bytes=VMEM_LIMIT),
        name="out_ffn2",
    )(h, o_gla, o_dsa, wo, norm, wg, wu, wd, fnorm)


def _rope_tables(lp):
    pos = jnp.arange(lp, dtype=F32)
    inv_freq = 1.0 / (ROPE_THETA ** (jnp.arange(0, DSA_HD, 2, dtype=F32) / DSA_HD))
    ang = pos[:, None] * inv_freq[None, :]
    cos = jnp.cos(ang)
    sin = jnp.sin(ang)
    cos2 = jnp.concatenate([cos, cos, cos, cos], axis=-1)
    sin2s = jnp.concatenate([-sin, sin, -sin, sin], axis=-1)
    return cos2, sin2s


def _layer(h, k_sel, l_tok, ffn1_norm, ffn1_w_gate, ffn1_w_up, ffn1_w_down, mix_norm, w_in, gla_w_a2, gla_b_a,
           gla_head_norm, w_out, ffn2_norm, ffn2_w_gate, ffn2_w_up, ffn2_w_down, final_norm, cos2, sin2s):
    d = h.shape[1]
    h1 = _ffn1(h, ffn1_norm.reshape(1, d), ffn1_w_gate.astype(BF16), ffn1_w_up.astype(BF16),
               ffn1_w_down.astype(BF16))

    sizes = (GLA_QK, GLA_QK, GLA_WIDTH, GLA_WIDTH, GLA_LOWRANK, DSA_WIDTH, DSA_WIDTH, DSA_WIDTH, IDX_WIDTH,
             IDX_HD, IDX_HEADS)
    offs = np.concatenate([[0], np.cumsum(sizes)])
    col = lambda n: w_in[:, offs[n]:offs[n + 1]]
    pad = jnp.zeros((d, LANES - IDX_HD - IDX_HEADS - GLA_LOWRANK), w_in.dtype)
    w_all = jnp.concatenate([col(0), col(1), col(2), col(3), col(5), col(6), col(7), col(8),
                             col(9), col(10), col(4), pad], axis=1).astype(BF16)
    wa2p = jnp.zeros((LANES, GLA_QK), F32).at[MISC_GA:MISC_GA + GLA_LOWRANK, :].set(gla_w_a2).astype(BF16)

    gqk, gv, gr, gg, dqt, dk, dvt, iqt, ik, iwt = _in_proj(
        h1, mix_norm.reshape(1, d), w_all, wa2p, gla_b_a.reshape(1, GLA_QK), cos2, sin2s)
    o_gla = _gla(gqk, gv, gr, gg, gla_head_norm.reshape(1, GLA_DV))
    o_dsa = _dsa(dqt, dk, dvt, iqt, ik, iwt, k_sel, l_tok)
    return _out_ffn2(h1, o_gla, o_dsa, w_out.astype(BF16), ffn2_norm.reshape(1, d), ffn2_w_gate.astype(BF16),
                     ffn2_w_up.astype(BF16), ffn2_w_down.astype(BF16), final_norm.reshape(1, d))


def kernel(x, meta_tokens, ffn1_norm, ffn1_w_gate, ffn1_w_up, ffn1_w_down, mix_norm, w_in, gla_w_a2, gla_b_a,
           gla_head_norm, w_out, ffn2_norm, ffn2_w_gate, ffn2_w_up, ffn2_w_down, final_norm):
    batch, seq, d = x.shape
    depth = ffn1_norm.shape[0]
    assert depth == 1, "the final norm is fused into the layer's last kernel"
    k_sel = min(TOPK_MAX, seq // 4)
    l_tok = seq + N_META
    lp = -(-l_tok // ROW_PAD) * ROW_PAD
    cos2, sin2s = _rope_tables(lp)
    outs = []
    for bi in range(batch):
        h = jnp.concatenate([meta_tokens.astype(x.dtype), x[bi], jnp.zeros((lp - l_tok, d), x.dtype)], axis=0)
        h = _layer(h, k_sel, l_tok, ffn1_norm[0], ffn1_w_gate[0], ffn1_w_up[0], ffn1_w_down[0], mix_norm[0],
                   w_in[0], gla_w_a2[0], gla_b_a[0], gla_head_norm[0], w_out[0], ffn2_norm[0], ffn2_w_gate[0],
                   ffn2_w_up[0], ffn2_w_down[0], final_norm, cos2, sin2s)
        outs.append(h[N_META:l_tok])
    return jnp.stack(outs, axis=0)
```

```python
import functools

import numpy as np
import jax
import jax.numpy as jnp
from jax import lax
from jax.experimental import pallas as pl
from jax.experimental.pallas import tpu as pltpu

N_META = 16
EPS = 1e-6
ROPE_THETA = 10000.0
GLA_HEADS = 4
GLA_DK = 64
GLA_DV = 128
GLA_LOWRANK = 16
GLA_GATE_NORM = 16.0
DSA_HEADS = 8
DSA_HD = 64
IDX_HEADS = 8
IDX_HD = 64
TOPK_MAX = 256

GLA_QK = GLA_HEADS * GLA_DK
GLA_WIDTH = GLA_HEADS * GLA_DV
DSA_WIDTH = DSA_HEADS * DSA_HD
IDX_WIDTH = IDX_HEADS * IDX_HD

LANES = 128
SUBLANES = 8
ROW_TILE = 384
QBLK = 128
KSUB = 128
KTILE = 512
ATILE = KTILE // 2
N_GROUPS = 2 * KSUB
INTERPOLATED_PROBES = 8
UNCHECKED_HALVINGS = 4
GLA_STEP = 128
GLA_SUB = 16
ROW_PAD = 1536
ACC_ROWS = DSA_HD + 16
VMEM_LIMIT = 60 * 1024 * 1024
NEG_BIG = -1e30

MISC_IK = 0
MISC_IW = IDX_HD
MISC_GA = IDX_HD + IDX_HEADS

F32 = jnp.float32
BF16 = jnp.bfloat16


def _dot(a, b):
    return jnp.dot(a, b, preferred_element_type=F32)


def _rmsnorm(x, g):
    return x * lax.rsqrt(jnp.mean(x * x, axis=-1, keepdims=True) + EPS) * g


def _const_spec(shape):
    return pl.BlockSpec(shape, lambda *_: (0,) * len(shape), pipeline_mode=pl.Buffered(1))


def _swiglu_half(x, norm_ref, wg_ref, wu_ref, wd_ref, ff_chunk):
    hn = _rmsnorm(x, norm_ref[...]).astype(BF16)
    d_ff = wg_ref.shape[1]
    acc = jnp.zeros(x.shape, F32)
    for c in range(d_ff // ff_chunk):
        sl = slice(c * ff_chunk, (c + 1) * ff_chunk)
        g = _dot(hn, wg_ref[:, sl])
        u = _dot(hn, wu_ref[:, sl])
        a = (g * jax.nn.sigmoid(g) * u).astype(BF16)
        acc = acc + _dot(a, wd_ref[sl, :])
    return x + 0.5 * acc


def _ffn1_kernel(h_ref, norm_ref, wg_ref, wu_ref, wd_ref, o_ref, *, ff_chunk):
    o_ref[...] = _swiglu_half(h_ref[...], norm_ref, wg_ref, wu_ref, wd_ref, ff_chunk)


def _ffn_chunk(d_ff):
    for c in (1408, 1024, 512, 256, 128):
        if d_ff % c == 0:
            return c
    return d_ff


def _ffn1(h, norm, wg, wu, wd):
    lp, d = h.shape
    d_ff = wg.shape[1]
    row = pl.BlockSpec((ROW_TILE, d), lambda i: (i, 0))
    return pl.pallas_call(
        functools.partial(_ffn1_kernel, ff_chunk=_ffn_chunk(d_ff)),
        grid=(lp // ROW_TILE,),
        in_specs=[row, _const_spec((1, d)), _const_spec((d, d_ff)), _const_spec((d, d_ff)),
                  _const_spec((d_ff, d))],
        out_specs=row,
        out_shape=jax.ShapeDtypeStruct((lp, d), F32),
        compiler_params=pltpu.CompilerParams(dimension_semantics=("parallel",),
                                             vmem_limit_bytes=VMEM_LIMIT),
        name="ffn1",
    )(h, norm, wg, wu, wd)


def _rope(x, cos, sin_signed, first_half):
    parts = []
    for c in range(x.shape[1] // LANES):
        xc = x[:, c * LANES:(c + 1) * LANES]
        rot = jnp.where(first_half, pltpu.roll(xc, LANES - 32, 1), pltpu.roll(xc, 32, 1))
        parts.append(xc * cos + rot * sin_signed)
    return parts[0] if len(parts) == 1 else jnp.concatenate(parts, axis=1)


def _log_sigmoid(x):
    return jnp.minimum(x, 0.0) - jnp.log(1.0 + jnp.exp(-jnp.abs(x)))


def _in_proj_kernel(h_ref, norm_ref, w_ref, wa2_ref, ba_ref, cos_ref, sin_ref,
                    gqk_ref, gv_ref, gr_ref, gg_ref, dqt_ref, dk_ref, dvt_ref, iqt_ref, ik_ref, iwt_ref):
    hn = _rmsnorm(h_ref[...], norm_ref[...]).astype(BF16)
    cos = cos_ref[...]
    sin_s = sin_ref[...]
    lane = lax.broadcasted_iota(jnp.int32, cos.shape, 1)
    first_half = (lane % DSA_HD) < (DSA_HD // 2)

    def proj(c0, width):
        return _dot(hn, w_ref[:, c0:c0 + width])

    c = 0
    gqk_ref[...] = proj(c, 2 * GLA_QK); c += 2 * GLA_QK
    gv_ref[...] = proj(c, GLA_WIDTH); c += GLA_WIDTH
    gr_ref[...] = proj(c, GLA_WIDTH); c += GLA_WIDTH
    dqt_ref[...] = (_rope(proj(c, DSA_WIDTH), cos, sin_s, first_half) * (DSA_HD ** -0.5)).T.astype(BF16)
    c += DSA_WIDTH
    dk_ref[...] = _rope(proj(c, DSA_WIDTH), cos, sin_s, first_half).astype(BF16); c += DSA_WIDTH
    dvt_ref[...] = proj(c, DSA_WIDTH).T.astype(BF16); c += DSA_WIDTH
    iqt_ref[...] = (_rope(proj(c, IDX_WIDTH), cos, sin_s, first_half) * (IDX_HD ** -0.5)).T.astype(BF16)
    c += IDX_WIDTH
    misc = proj(c, LANES)
    ik_ref[...] = _rope(misc, cos, sin_s, first_half)[:, MISC_IK:MISC_IK + IDX_HD].astype(BF16)
    iwt_ref[...] = (misc * (IDX_HEADS ** -0.5)).T
    pre = _dot(misc.astype(BF16), wa2_ref[...]) + ba_ref[...]
    gg_ref[...] = _log_sigmoid(pre) * (1.0 / GLA_GATE_NORM)


def _in_proj(h, norm, w_all, wa2p, ba, cos2, sin2s):
    lp, d = h.shape
    ncols = w_all.shape[1]

    def row(width):
        return pl.BlockSpec((ROW_TILE, width), lambda i: (i, 0))

    def col(height):
        return pl.BlockSpec((height, ROW_TILE), lambda i: (0, i))

    outs = [
        (row(2 * GLA_QK), (lp, 2 * GLA_QK), F32), (row(GLA_WIDTH), (lp, GLA_WIDTH), F32),
        (row(GLA_WIDTH), (lp, GLA_WIDTH), F32), (row(GLA_QK), (lp, GLA_QK), F32),
        (col(DSA_WIDTH), (DSA_WIDTH, lp), BF16), (row(DSA_WIDTH), (lp, DSA_WIDTH), BF16),
        (col(DSA_WIDTH), (DSA_WIDTH, lp), BF16), (col(IDX_WIDTH), (IDX_WIDTH, lp), BF16),
        (row(IDX_HD), (lp, IDX_HD), BF16), (col(LANES), (LANES, lp), F32),
    ]
    return pl.pallas_call(
        _in_proj_kernel,
        grid=(lp // ROW_TILE,),
        in_specs=[row(d), _const_spec((1, d)), _const_spec((d, ncols)), _const_spec((LANES, GLA_QK)),
                  _const_spec((1, GLA_QK)), row(LANES), row(LANES)],
        out_specs=[spec for spec, _, _ in outs],
        out_shape=[jax.ShapeDtypeStruct(shape, dt) for _, shape, dt in outs],
        compiler_params=pltpu.CompilerParams(dimension_semantics=("parallel",),
                                             vmem_limit_bytes=VMEM_LIMIT),
        name="in_proj",
    )(h, norm, w_all, wa2p, ba, cos2, sin2s)


def _gla_kernel(qk_ref, v_ref, r_ref, g_ref, hnorm_ref, o_ref, s_ref):
    n_sub = GLA_STEP // GLA_SUB

    @pl.when(pl.program_id(0) == 0)
    def _():
        s_ref[...] = jnp.zeros(s_ref.shape, F32)

    q = qk_ref[:, :GLA_QK] * (GLA_DK ** -0.5)
    k = qk_ref[:, GLA_QK:]
    v = v_ref[...]
    g = g_ref[...]

    ri = lax.broadcasted_iota(jnp.int32, (GLA_STEP, GLA_STEP), 0)
    ci = lax.broadcasted_iota(jnp.int32, (GLA_STEP, GLA_STEP), 1)
    tri = ((ri // GLA_SUB == ci // GLA_SUB) & (ci <= ri)).astype(F32)
    b = jnp.dot(tri, g, preferred_element_type=F32, precision=lax.Precision.HIGHEST)

    b3 = b.reshape(n_sub, GLA_SUB, GLA_QK)
    k3 = k.reshape(n_sub, GLA_SUB, GLA_QK)
    v3 = v.reshape(n_sub, GLA_SUB, GLA_WIDTH)
    b_last = jnp.broadcast_to(b3[:, GLA_SUB - 1:GLA_SUB, :], b3.shape).reshape(GLA_STEP, GLA_QK)

    pos = lax.broadcasted_iota(jnp.int32, (GLA_STEP, GLA_QK), 0) % GLA_SUB
    t_rows = []
    for j in range(GLA_SUB):
        kj = jnp.broadcast_to(k3[:, j:j + 1, :], k3.shape).reshape(GLA_STEP, GLA_QK)
        bj = jnp.broadcast_to(b3[:, j:j + 1, :], b3.shape).reshape(GLA_STEP, GLA_QK)
        e = jnp.where(pos >= j, b - bj, -jnp.inf)
        t_rows.append((q * kj * jnp.exp(e)).astype(BF16))
    t = jnp.concatenate(t_rows, axis=0)
    hr = lax.broadcasted_iota(jnp.int32, (GLA_QK, GLA_WIDTH), 0) // GLA_DK
    hc = lax.broadcasted_iota(jnp.int32, (GLA_QK, GLA_WIDTH), 1) // GLA_DV
    head_sum = (hr == hc).astype(BF16)
    p = _dot(t, head_sum)
    o = jnp.zeros((GLA_STEP, GLA_WIDTH), F32)
    for j in range(GLA_SUB):
        vj = jnp.broadcast_to(v3[:, j:j + 1, :], v3.shape).reshape(GLA_STEP, GLA_WIDTH)
        o = o + p[j * GLA_STEP:(j + 1) * GLA_STEP, :] * vj

    qe = q * jnp.exp(b)
    kd = k * jnp.exp(b_last - b)
    kd_t = kd.T
    dec_t = jnp.exp(b).T
    lane_head = lax.broadcasted_iota(jnp.int32, (GLA_SUB, GLA_QK), 1) // GLA_DK
    s = s_ref[...]
    o_inter = []
    for c in range(n_sub):
        rows = slice(c * GLA_SUB, (c + 1) * GLA_SUB)
        qe_c = qe[rows, :]
        q_stack = jnp.concatenate(
            [jnp.where(lane_head == h, qe_c, 0.0) for h in range(GLA_HEADS)], axis=0).astype(BF16)
        r_stack = _dot(q_stack, s.astype(BF16))
        o_inter.append(jnp.concatenate(
            [r_stack[h * GLA_SUB:(h + 1) * GLA_SUB, :] for h in range(GLA_HEADS)], axis=1))
        kd_c = kd_t[:, rows].astype(BF16)
        v_c = v[rows, :].astype(BF16)
        u_c = jnp.concatenate(
            [_dot(kd_c[h * GLA_DK:(h + 1) * GLA_DK, :], v_c[:, h * GLA_DV:(h + 1) * GLA_DV])
             for h in range(GLA_HEADS)], axis=0)
        last = c * GLA_SUB + GLA_SUB - 1
        s = dec_t[:, last:last + 1] * s + u_c
    s_ref[...] = s
    o = o + jnp.concatenate(o_inter, axis=0)

    r = r_ref[...]
    gate = r * jax.nn.sigmoid(r)
    hn = hnorm_ref[...]
    outs = []
    for h in range(GLA_HEADS):
        oh = o[:, h * GLA_DV:(h + 1) * GLA_DV]
        outs.append(_rmsnorm(oh, hn))
    o_ref[...] = (jnp.concatenate(outs, axis=1) * gate).astype(o_ref.dtype)


def _gla(gqk, gv, gr, gg, hnorm):
    lp = gqk.shape[0]

    def row(width):
        return pl.BlockSpec((GLA_STEP, width), lambda i: (i, 0))

    return pl.pallas_call(
        _gla_kernel,
        grid=(lp // GLA_STEP,),
        in_specs=[row(2 * GLA_QK), row(GLA_WIDTH), row(GLA_WIDTH), row(GLA_QK), _const_spec((1, GLA_DV))],
        out_specs=row(GLA_WIDTH),
        out_shape=jax.ShapeDtypeStruct((lp, GLA_WIDTH), BF16),
        scratch_shapes=[pltpu.VMEM((GLA_QK, GLA_DV), F32)],
        compiler_params=pltpu.CompilerParams(dimension_semantics=("arbitrary",),
                                             vmem_limit_bytes=VMEM_LIMIT),
        name="gla",
    )(gqk, gv, gr, gg, hnorm)


def _key_to_float(u):
    bits = jnp.where(u >= 0, u, u ^ jnp.int32(0x7FFFFFFF))
    return lax.bitcast_convert_type(bits, F32)


def _float_to_key(f):
    bits = lax.bitcast_convert_type(f, jnp.int32)
    return jnp.where(bits >= 0, bits, bits ^ jnp.int32(0x7FFFFFFF))


_KEY_NEG_INF = np.int32(np.uint32(0xFF800000) ^ np.uint32(0x7FFFFFFF))


def _tile_loop(n, body, carry, unroll):
    trips = n // unroll

    def group(u, c):
        for r in range(unroll):
            c = body(u * unroll + r, c)
        return c

    carry = lax.fori_loop(0, trips, group, carry)
    return lax.fori_loop(trips * unroll, n, body, carry)


def _fold_rows(x, op):
    parts = [x[r:r + SUBLANES, :] for r in range(0, x.shape[0], SUBLANES)]
    while len(parts) > 1:
        parts = [op(parts[a], parts[a + 1]) if a + 1 < len(parts) else parts[a] for a in range(0, len(parts), 2)]
    return parts[0]


def _dsa_block(qt_ref, iqt_ref, iwt_ref, k_ref, vt_ref, ik_ref, o_ref, sc_ref, bias_ref, acc_ref, rhs_ref,
               sa_ref, sb_ref, gm_ref, k_sel):
    i = pl.program_id(0)
    n_tiles = (i * QBLK + QBLK + KTILE - 1) // KTILE
    last = n_tiles - 1
    q_pos = i * QBLK + lax.broadcasted_iota(jnp.int32, (KSUB, QBLK), 1)
    key_row = lax.broadcasted_iota(jnp.int32, (KSUB, QBLK), 0)

    def visible(ks):
        return (ks + key_row) <= q_pos

    def peeled(body, carry):
        carry = _tile_loop(last, functools.partial(body, masked=False), carry, 4)
        return body(last, carry, masked=True)

    iqt = iqt_ref[...]
    iq_all = jnp.concatenate([iqt[h * IDX_HD:(h + 1) * IDX_HD, :] for h in range(IDX_HEADS)], axis=1)
    iw = iwt_ref[MISC_IW:MISC_IW + IDX_HEADS, :]
    gm_ref[...] = jnp.full(gm_ref.shape, -jnp.inf, F32)

    def score_tile(t, carry, masked):
        for sub in range(KTILE // KSUB):
            ks = pl.multiple_of(t * KTILE + sub * KSUB, KSUB)
            lg = _dot(ik_ref[pl.ds(ks, KSUB), :], iq_all)
            sc = iw[0:1, :] * jnp.maximum(lg[:, 0:QBLK], 0.0)
            for h in range(1, IDX_HEADS):
                sc = sc + iw[h:h + 1, :] * jnp.maximum(lg[:, h * QBLK:(h + 1) * QBLK], 0.0)
            if masked:
                sc = jnp.where(visible(ks), sc, -jnp.inf)
            sc_ref[pl.ds(ks, KSUB), :] = sc
            grp = slice((sub % 2) * KSUB, (sub % 2 + 1) * KSUB)
            gm_ref[grp, :] = jnp.maximum(gm_ref[grp, :], sc)
        return carry

    peeled(score_tile, 0)

    kk = float(k_sel)

    def count(pred):
        def body(t, acc):
            ks = pl.multiple_of(t * KTILE, KTILE)
            return acc + _fold_rows(jnp.where(pred(sc_ref[pl.ds(ks, KTILE), :]), 1.0, 0.0), jnp.add)
        acc = _tile_loop(n_tiles, body, jnp.zeros((SUBLANES, QBLK), F32), 4)
        return jnp.sum(acc, axis=0, keepdims=True)

    def count_ge(cand):
        return count(lambda x: x >= cand)

    gm = gm_ref[...]
    lo = jnp.maximum(_float_to_key(jnp.min(_fold_rows(gm, jnp.minimum), axis=0, keepdims=True)), _KEY_NEG_INF)
    hi = _float_to_key(jnp.max(_fold_rows(gm, jnp.maximum), axis=0, keepdims=True))
    n_lo = count_ge(_key_to_float(lo))

    def settled(lo, hi, n_lo):
        return jnp.logical_or(n_lo <= kk, lo >= hi)

    def any_open(lo, hi, n_lo):
        return jnp.max(jnp.where(settled(lo, hi, n_lo), 0.0, 1.0))

    def midpoint(lo, hi):
        return (lo | hi) - ((lo ^ hi) >> 1)

    def probe_at(key, lo, hi, n_lo):
        open_ = jnp.logical_not(settled(lo, hi, n_lo))
        cnt = count_ge(_key_to_float(key))
        up = jnp.logical_and(open_, cnt >= kk)
        down = jnp.logical_and(open_, cnt < kk)
        return jnp.where(up, key, lo), jnp.where(down, key - 1, hi), jnp.where(up, cnt, n_lo), cnt, up, down

    def halve(lo, hi, n_lo):
        return probe_at(midpoint(lo, hi), lo, hi, n_lo)[:3]

    log_k = float(np.log(k_sel))

    def interpolate(st):
        lo, hi, n_lo, g_lo, g_hi, side = st
        f_lo, f_hi = _key_to_float(lo), _key_to_float(hi)
        guess = f_lo + (g_lo - log_k) / jnp.maximum(g_lo - g_hi, 1e-9) * (f_hi - f_lo)
        usable = jnp.logical_and(lo > _KEY_NEG_INF, jnp.abs(guess) < jnp.inf)
        key = jnp.where(usable, jnp.clip(_float_to_key(guess), lo + 1, hi), midpoint(lo, hi))
        lo, hi, n_lo, cnt, up, down = probe_at(key, lo, hi, n_lo)
        g_cnt = jnp.log(jnp.maximum(cnt, 0.5))
        g_hi = jnp.where(down, g_cnt, jnp.where(jnp.logical_and(up, side > 0.0), 0.5 * (g_hi + log_k), g_hi))
        g_lo = jnp.where(up, g_cnt, jnp.where(jnp.logical_and(down, side < 0.0), 0.5 * (g_lo + log_k), g_lo))
        side = jnp.where(up, 1.0, jnp.where(down, -1.0, side))
        return lo, hi, n_lo, g_lo, g_hi, side

    def checked_halve(st):
        lo, hi, n_lo = halve(*halve(*st[:3]))
        return lo, hi, n_lo, any_open(lo, hi, n_lo), st[4] + 1

    lo, hi, n_lo, _, _, _ = lax.fori_loop(
        0, INTERPOLATED_PROBES, lambda _, st: interpolate(st),
        (lo, hi, n_lo, jnp.log(jnp.maximum(n_lo, 0.5)), jnp.full((1, QBLK), float(np.log(0.5)), F32),
         jnp.zeros((1, QBLK), F32)))
    lo, hi, n_lo = lax.fori_loop(0, UNCHECKED_HALVINGS, lambda _, st: halve(*st), (lo, hi, n_lo))
    lo, hi, n_lo, _, _ = lax.while_loop(lambda st: jnp.logical_and(st[3] > 0.0, st[4] < 17), checked_halve,
                                        (lo, hi, n_lo, any_open(lo, hi, n_lo), jnp.int32(0)))
    few = n_lo < kk
    lo_f = _key_to_float(lo)

    def min_ge_body(t, acc):
        ks = pl.multiple_of(t * KTILE, KTILE)
        x = sc_ref[pl.ds(ks, KTILE), :]
        return jnp.minimum(acc, _fold_rows(jnp.where(x >= lo_f, x, jnp.inf), jnp.minimum))

    thr = jnp.min(_tile_loop(n_tiles, min_ge_body, jnp.full((SUBLANES, QBLK), jnp.inf, F32), 4),
                  axis=0, keepdims=True)
    thr = jnp.where(few, -jnp.inf, thr)
    has_ties = jnp.max(jnp.where(few, 0.0, n_lo)) > kk

    def bias_fast(t, carry, masked):
        for sub in range(KTILE // KSUB):
            ks = pl.multiple_of(t * KTILE + sub * KSUB, KSUB)
            keep = sc_ref[pl.ds(ks, KSUB), :] >= thr
            if masked:
                keep = jnp.logical_and(keep, visible(ks))
            bias_ref[pl.ds(ks, KSUB), :] = jnp.where(keep, 0.0, NEG_BIG).astype(BF16)
        return carry

    def bias_ties(t, run, masked, budget, tri):
        for sub in range(KTILE // KSUB):
            ks = pl.multiple_of(t * KTILE + sub * KSUB, KSUB)
            x = sc_ref[pl.ds(ks, KSUB), :]
            above = x > thr
            tie = x == thr
            tie_f = jnp.where(tie, 1.0, 0.0)
            rank = run + _dot(tri, tie_f.astype(BF16))
            keep = jnp.logical_or(above, jnp.logical_and(tie, rank <= budget))
            if masked:
                keep = jnp.logical_and(keep, visible(ks))
            bias_ref[pl.ds(ks, KSUB), :] = jnp.where(keep, 0.0, NEG_BIG).astype(BF16)
            run = run + jnp.sum(tie_f, axis=0, keepdims=True)
        return run

    @pl.when(jnp.logical_not(has_ties))
    def _():
        peeled(bias_fast, 0)

    @pl.when(has_ties)
    def _():
        budget = kk - count(lambda x: x > thr)
        tri = (lax.broadcasted_iota(jnp.int32, (KSUB, KSUB), 1) <= key_row).astype(BF16)
        peeled(functools.partial(bias_ties, budget=budget, tri=tri), jnp.zeros((1, QBLK), F32))

    qt = qt_ref[...]
    eye = (lax.broadcasted_iota(jnp.int32, (QBLK, QBLK), 0)
           == lax.broadcasted_iota(jnp.int32, (QBLK, QBLK), 1)).astype(BF16)
    zero = jnp.zeros((DSA_HD, QBLK), BF16)
    n_pairs = DSA_HEADS // 2
    for c in range(n_pairs):
        qa = qt[(2 * c) * DSA_HD:(2 * c + 1) * DSA_HD, :]
        qb = qt[(2 * c + 1) * DSA_HD:(2 * c + 2) * DSA_HD, :]
        rhs_ref[c] = jnp.concatenate([jnp.concatenate([qa, zero], axis=1),
                                      jnp.concatenate([zero, qb], axis=1),
                                      jnp.concatenate([eye, eye], axis=1)], axis=0)
    acc_ref[...] = jnp.zeros(acc_ref.shape, F32)
    ones_rows = jnp.ones((ACC_ROWS - DSA_HD, ATILE), BF16)

    def qk_stage(ks, s_ref):
        bias = bias_ref[pl.ds(ks, ATILE), :]
        tile_max = []
        for c in range(n_pairs):
            lhs = jnp.concatenate([k_ref[pl.ds(ks, ATILE), c * 2 * DSA_HD:(c + 1) * 2 * DSA_HD], bias], axis=1)
            s = _dot(lhs, rhs_ref[c])
            s_ref[c] = s
            tile_max.append(_fold_rows(s, jnp.maximum))
        return tuple(tile_max)

    def pv_stage(ks, s_ref, tile_max, ms):
        new_ms = []
        for c in range(n_pairs):
            m_new = jnp.maximum(ms[c], jnp.max(tile_max[c], axis=0, keepdims=True))
            alpha = jnp.exp(ms[c] - m_new)
            p = jnp.exp(s_ref[c] - m_new).astype(BF16)
            for hh in range(2):
                h = 2 * c + hh
                rows = slice(h * ACC_ROWS, (h + 1) * ACC_ROWS)
                vt_aug = jnp.concatenate([vt_ref[h * DSA_HD:(h + 1) * DSA_HD, pl.ds(ks, ATILE)], ones_rows], axis=0)
                pv = _dot(vt_aug, p[:, hh * QBLK:(hh + 1) * QBLK])
                acc_ref[rows, :] = alpha[:, hh * QBLK:(hh + 1) * QBLK] * acc_ref[rows, :] + pv
            new_ms.append(m_new)
        return tuple(new_ms)

    def attn_tile(t, carry):
        max_a, ms = carry
        k0 = pl.multiple_of(t * KTILE, KTILE)
        k1 = pl.multiple_of(t * KTILE + ATILE, ATILE)
        k2 = pl.multiple_of(t * KTILE + KTILE, KTILE)
        max_b = qk_stage(k1, sb_ref)
        ms = pv_stage(k0, sa_ref, max_a, ms)
        max_a = qk_stage(k2, sa_ref)
        ms = pv_stage(k1, sb_ref, max_b, ms)
        return max_a, ms

    ms = tuple(jnp.full((1, 2 * QBLK), NEG_BIG, F32) for _ in range(n_pairs))
    max_a, ms = _tile_loop(last, attn_tile, (qk_stage(0, sa_ref), ms), 4)
    k0 = pl.multiple_of(last * KTILE, KTILE)
    k1 = pl.multiple_of(last * KTILE + ATILE, ATILE)
    max_b = qk_stage(k1, sb_ref)
    ms = pv_stage(k0, sa_ref, max_a, ms)
    pv_stage(k1, sb_ref, max_b, ms)
    outs = []
    for h in range(DSA_HEADS):
        a = acc_ref[h * ACC_ROWS:(h + 1) * ACC_ROWS, :]
        outs.append(a[:DSA_HD, :] / a[DSA_HD:DSA_HD + 1, :])
    o_ref[...] = jnp.concatenate(outs, axis=0).T.astype(o_ref.dtype)


def _dsa_kernel(qt_ref, iqt_ref, iwt_ref, k_ref, vt_ref, ik_ref, o_ref, sc_ref, bias_ref, acc_ref, rhs_ref,
                sa_ref, sb_ref, gm_ref, *, k_sel, l_tok):
    has_tokens = pl.program_id(0) * QBLK < l_tok

    @pl.when(has_tokens)
    def _():
        _dsa_block(qt_ref, iqt_ref, iwt_ref, k_ref, vt_ref, ik_ref, o_ref, sc_ref, bias_ref, acc_ref, rhs_ref,
                   sa_ref, sb_ref, gm_ref, k_sel)

    @pl.when(jnp.logical_not(has_tokens))
    def _():
        o_ref[...] = jnp.zeros(o_ref.shape, o_ref.dtype)


def _dsa(dqt, dk, dvt, iqt, ik, iwt, k_sel, l_tok):
    lp = dk.shape[0]

    def col(height):
        return pl.BlockSpec((height, QBLK), lambda i: (0, i))

    return pl.pallas_call(
        functools.partial(_dsa_kernel, k_sel=k_sel, l_tok=l_tok),
        grid=(lp // QBLK,),
        in_specs=[col(DSA_WIDTH), col(IDX_WIDTH), col(LANES), _const_spec((lp, DSA_WIDTH)),
                  _const_spec((DSA_WIDTH, lp)), _const_spec((lp, IDX_HD))],
        out_specs=pl.BlockSpec((QBLK, DSA_WIDTH), lambda i: (i, 0)),
        out_shape=jax.ShapeDtypeStruct((lp, DSA_WIDTH), BF16),
        scratch_shapes=[pltpu.VMEM((lp, QBLK), F32), pltpu.VMEM((lp, QBLK), BF16),
                        pltpu.VMEM((DSA_HEADS * ACC_ROWS, QBLK), F32),
                        pltpu.VMEM((DSA_HEADS // 2, 2 * LANES, 2 * QBLK), BF16),
                        pltpu.VMEM((DSA_HEADS // 2, ATILE, 2 * QBLK), F32),
                        pltpu.VMEM((DSA_HEADS // 2, ATILE, 2 * QBLK), F32),
                        pltpu.VMEM((N_GROUPS, QBLK), F32)],
        compiler_params=pltpu.CompilerParams(dimension_semantics=("parallel",),
                                             vmem_limit_bytes=VMEM_LIMIT),
        name="dsa",
    )(dqt, iqt, iwt, dk, dvt, ik)


def _out_ffn2_kernel(h_ref, og_ref, od_ref, wo_ref, norm_ref, wg_ref, wu_ref, wd_ref, fnorm_ref, o_ref, *,
                     ff_chunk):
    h2 = h_ref[...] + _dot(og_ref[...], wo_ref[:GLA_WIDTH, :]) + _dot(od_ref[...], wo_ref[GLA_WIDTH:, :])
    h3 = _swiglu_half(h2, norm_ref, wg_ref, wu_ref, wd_ref, ff_chunk)
    o_ref[...] = _rmsnorm(h3, fnorm_ref[...])


def _out_ffn2(h, o_gla, o_dsa, wo, norm, wg, wu, wd, fnorm):
    lp, d = h.shape
    d_ff = wg.shape[1]

    def row(width):
        return pl.BlockSpec((ROW_TILE, width), lambda i: (i, 0))

    return pl.pallas_call(
        functools.partial(_out_ffn2_kernel, ff_chunk=_ffn_chunk(d_ff)),
        grid=(lp // ROW_TILE,),
        in_specs=[row(d), row(GLA_WIDTH), row(DSA_WIDTH), _const_spec((GLA_WIDTH + DSA_WIDTH, d)),
                  _const_spec((1, d)), _const_spec((d, d_ff)), _const_spec((d, d_ff)), _const_spec((d_ff, d)),
                  _const_spec((1, d))],
        out_specs=row(d),
        out_shape=jax.ShapeDtypeStruct((lp, d), F32),
        compiler_params=pltpu.CompilerParams(dimension_semantics=("parallel",),
                                             vmem_limit_bytes=VMEM_LIMIT),
        name="out_ffn2",
    )(h, o_gla, o_dsa, wo, norm, wg, wu, wd, fnorm)


def _rope_tables(lp):
    pos = jnp.arange(lp, dtype=F32)
    inv_freq = 1.0 / (ROPE_THETA ** (jnp.arange(0, DSA_HD, 2, dtype=F32) / DSA_HD))
    ang = pos[:, None] * inv_freq[None, :]
    cos = jnp.cos(ang)
    sin = jnp.sin(ang)
    cos2 = jnp.concatenate([cos, cos, cos, cos], axis=-1)
    sin2s = jnp.concatenate([-sin, sin, -sin, sin], axis=-1)
    return cos2, sin2s


def _layer(h, k_sel, l_tok, ffn1_norm, ffn1_w_gate, ffn1_w_up, ffn1_w_down, mix_norm, w_in, gla_w_a2, gla_b_a,
           gla_head_norm, w_out, ffn2_norm, ffn2_w_gate, ffn2_w_up, ffn2_w_down, final_norm, cos2, sin2s):
    d = h.shape[1]
    h1 = _ffn1(h, ffn1_norm.reshape(1, d), ffn1_w_gate.astype(BF16), ffn1_w_up.astype(BF16),
               ffn1_w_down.astype(BF16))

    sizes = (GLA_QK, GLA_QK, GLA_WIDTH, GLA_WIDTH, GLA_LOWRANK, DSA_WIDTH, DSA_WIDTH, DSA_WIDTH, IDX_WIDTH,
             IDX_HD, IDX_HEADS)
    offs = np.concatenate([[0], np.cumsum(sizes)])
    col = lambda n: w_in[:, offs[n]:offs[n + 1]]
    pad = jnp.zeros((d, LANES - IDX_HD - IDX_HEADS - GLA_LOWRANK), w_in.dtype)
    w_all = jnp.concatenate([col(0), col(1), col(2), col(3), col(5), col(6), col(7), col(8),
                             col(9), col(10), col(4), pad], axis=1).astype(BF16)
    wa2p = jnp.zeros((LANES, GLA_QK), F32).at[MISC_GA:MISC_GA + GLA_LOWRANK, :].set(gla_w_a2).astype(BF16)

    gqk, gv, gr, gg, dqt, dk, dvt, iqt, ik, iwt = _in_proj(
        h1, mix_norm.reshape(1, d), w_all, wa2p, gla_b_a.reshape(1, GLA_QK), cos2, sin2s)
    o_gla = _gla(gqk, gv, gr, gg, gla_head_norm.reshape(1, GLA_DV))
    o_dsa = _dsa(dqt, dk, dvt, iqt, ik, iwt, k_sel, l_tok)
    return _out_ffn2(h1, o_gla, o_dsa, w_out.astype(BF16), ffn2_norm.reshape(1, d), ffn2_w_gate.astype(BF16),
                     ffn2_w_up.astype(BF16), ffn2_w_down.astype(BF16), final_norm.reshape(1, d))


def kernel(x, meta_tokens, ffn1_norm, ffn1_w_gate, ffn1_w_up, ffn1_w_down, mix_norm, w_in, gla_w_a2, gla_b_a,
           gla_head_norm, w_out, ffn2_norm, ffn2_w_gate, ffn2_w_up, ffn2_w_down, final_norm):
    batch, seq, d = x.shape
    depth = ffn1_norm.shape[0]
    assert depth == 1, "the final norm is fused into the layer's last kernel"
    k_sel = min(TOPK_MAX, seq // 4)
    l_tok = seq + N_META
    lp = -(-l_tok // ROW_PAD) * ROW_PAD
    cos2, sin2s = _rope_tables(lp)
    outs = []
    for bi in range(batch):
        h = jnp.concatenate([meta_tokens.astype(x.dtype), x[bi], jnp.zeros((lp - l_tok, d), x.dtype)], axis=0)
        h = _layer(h, k_sel, l_tok, ffn1_norm[0], ffn1_w_gate[0], ffn1_w_up[0], ffn1_w_down[0], mix_norm[0],
                   w_in[0], gla_w_a2[0], gla_b_a[0], gla_head_norm[0], w_out[0], ffn2_norm[0], ffn2_w_gate[0],
                   ffn2_w_up[0], ffn2_w_down[0], final_norm, cos2, sin2s)
        outs.append(h[N_META:l_tok])
    return jnp.stack(outs, axis=0)
```

```python
import functools

import numpy as np
import jax
import jax.numpy as jnp
from jax import lax
from jax.experimental import pallas as pl
from jax.experimental.pallas import tpu as pltpu

N_META = 16
EPS = 1e-6
ROPE_THETA = 10000.0
GLA_HEADS = 4
GLA_DK = 64
GLA_DV = 128
GLA_LOWRANK = 16
GLA_GATE_NORM = 16.0
DSA_HEADS = 8
DSA_HD = 64
IDX_HEADS = 8
IDX_HD = 64
TOPK_MAX = 256

GLA_QK = GLA_HEADS * GLA_DK
GLA_WIDTH = GLA_HEADS * GLA_DV
DSA_WIDTH = DSA_HEADS * DSA_HD
IDX_WIDTH = IDX_HEADS * IDX_HD

LANES = 128
SUBLANES = 8
ROW_TILE = 384
QBLK = 128
KSUB = 128
KTILE = 512
ATILE = KTILE // 2
N_GROUPS = 2 * KSUB
INTERPOLATED_PROBES = 8
UNCHECKED_HALVINGS = 4
GLA_STEP = 128
GLA_SUB = 16
ROW_PAD = 1536
ACC_ROWS = DSA_HD + 16
VMEM_LIMIT = 60 * 1024 * 1024
NEG_BIG = -1e30

MISC_IK = 0
MISC_IW = IDX_HD
MISC_GA = IDX_HD + IDX_HEADS

F32 = jnp.float32
BF16 = jnp.bfloat16


def _dot(a, b):
    return jnp.dot(a, b, preferred_element_type=F32)


def _rmsnorm(x, g):
    return x * lax.rsqrt(jnp.mean(x * x, axis=-1, keepdims=True) + EPS) * g


def _const_spec(shape):
    return pl.BlockSpec(shape, lambda *_: (0,) * len(shape), pipeline_mode=pl.Buffered(1))


def _swiglu_half(x, norm_ref, wg_ref, wu_ref, wd_ref, ff_chunk):
    hn = _rmsnorm(x, norm_ref[...]).astype(BF16)
    d_ff = wg_ref.shape[1]
    acc = jnp.zeros(x.shape, F32)
    for c in range(d_ff // ff_chunk):
        sl = slice(c * ff_chunk, (c + 1) * ff_chunk)
        g = _dot(hn, wg_ref[:, sl])
        u = _dot(hn, wu_ref[:, sl])
        a = (g * jax.nn.sigmoid(g) * u).astype(BF16)
        acc = acc + _dot(a, wd_ref[sl, :])
    return x + 0.5 * acc


def _ffn1_kernel(h_ref, norm_ref, wg_ref, wu_ref, wd_ref, o_ref, *, ff_chunk):
    o_ref[...] = _swiglu_half(h_ref[...], norm_ref, wg_ref, wu_ref, wd_ref, ff_chunk)


def _ffn_chunk(d_ff):
    if d_ff <= 4096:
        return d_ff
    for c in (2048, 1024, 512, 256):
        if d_ff % c == 0:
            return c
    return d_ff


def _ffn1(h, norm, wg, wu, wd):
    lp, d = h.shape
    d_ff = wg.shape[1]
    row = pl.BlockSpec((ROW_TILE, d), lambda i: (i, 0))
    return pl.pallas_call(
        functools.partial(_ffn1_kernel, ff_chunk=_ffn_chunk(d_ff)),
        grid=(lp // ROW_TILE,),
        in_specs=[row, _const_spec((1, d)), _const_spec((d, d_ff)), _const_spec((d, d_ff)),
                  _const_spec((d_ff, d))],
        out_specs=row,
        out_shape=jax.ShapeDtypeStruct((lp, d), F32),
        compiler_params=pltpu.CompilerParams(dimension_semantics=("parallel",),
                                             vmem_limit_bytes=VMEM_LIMIT),
        name="ffn1",
    )(h, norm, wg, wu, wd)


def _rope(x, cos, sin_signed, first_half):
    parts = []
    for c in range(x.shape[1] // LANES):
        xc = x[:, c * LANES:(c + 1) * LANES]
        rot = jnp.where(first_half, pltpu.roll(xc, LANES - 32, 1), pltpu.roll(xc, 32, 1))
        parts.append(xc * cos + rot * sin_signed)
    return parts[0] if len(parts) == 1 else jnp.concatenate(parts, axis=1)


def _log_sigmoid(x):
    return jnp.minimum(x, 0.0) - jnp.log(1.0 + jnp.exp(-jnp.abs(x)))


def _in_proj_kernel(h_ref, norm_ref, w_ref, wa2_ref, ba_ref, cos_ref, sin_ref,
                    gqk_ref, gv_ref, gr_ref, gg_ref, dqt_ref, dk_ref, dvt_ref, iqt_ref, ik_ref, iwt_ref):
    hn = _rmsnorm(h_ref[...], norm_ref[...]).astype(BF16)
    cos = cos_ref[...]
    sin_s = sin_ref[...]
    lane = lax.broadcasted_iota(jnp.int32, cos.shape, 1)
    first_half = (lane % DSA_HD) < (DSA_HD // 2)

    def proj(c0, width):
        return _dot(hn, w_ref[:, c0:c0 + width])

    c = 0
    gqk_ref[...] = proj(c, 2 * GLA_QK); c += 2 * GLA_QK
    gv_ref[...] = proj(c, GLA_WIDTH); c += GLA_WIDTH
    gr_ref[...] = proj(c, GLA_WIDTH); c += GLA_WIDTH
    dqt_ref[...] = (_rope(proj(c, DSA_WIDTH), cos, sin_s, first_half) * (DSA_HD ** -0.5)).T.astype(BF16)
    c += DSA_WIDTH
    dk_ref[...] = _rope(proj(c, DSA_WIDTH), cos, sin_s, first_half).astype(BF16); c += DSA_WIDTH
    dvt_ref[...] = proj(c, DSA_WIDTH).T.astype(BF16); c += DSA_WIDTH
    iqt_ref[...] = (_rope(proj(c, IDX_WIDTH), cos, sin_s, first_half) * (IDX_HD ** -0.5)).T.astype(BF16)
    c += IDX_WIDTH
    misc = proj(c, LANES)
    ik_ref[...] = _rope(misc, cos, sin_s, first_half)[:, MISC_IK:MISC_IK + IDX_HD].astype(BF16)
    iwt_ref[...] = (misc * (IDX_HEADS ** -0.5)).T
    pre = _dot(misc.astype(BF16), wa2_ref[...]) + ba_ref[...]
    gg_ref[...] = _log_sigmoid(pre) * (1.0 / GLA_GATE_NORM)


def _in_proj(h, norm, w_all, wa2p, ba, cos2, sin2s):
    lp, d = h.shape
    ncols = w_all.shape[1]

    def row(width):
        return pl.BlockSpec((ROW_TILE, width), lambda i: (i, 0))

    def col(height):
        return pl.BlockSpec((height, ROW_TILE), lambda i: (0, i))

    outs = [
        (row(2 * GLA_QK), (lp, 2 * GLA_QK), F32), (row(GLA_WIDTH), (lp, GLA_WIDTH), F32),
        (row(GLA_WIDTH), (lp, GLA_WIDTH), F32), (row(GLA_QK), (lp, GLA_QK), F32),
        (col(DSA_WIDTH), (DSA_WIDTH, lp), BF16), (row(DSA_WIDTH), (lp, DSA_WIDTH), BF16),
        (col(DSA_WIDTH), (DSA_WIDTH, lp), BF16), (col(IDX_WIDTH), (IDX_WIDTH, lp), BF16),
        (row(IDX_HD), (lp, IDX_HD), BF16), (col(LANES), (LANES, lp), F32),
    ]
    return pl.pallas_call(
        _in_proj_kernel,
        grid=(lp // ROW_TILE,),
        in_specs=[row(d), _const_spec((1, d)), _const_spec((d, ncols)), _const_spec((LANES, GLA_QK)),
                  _const_spec((1, GLA_QK)), row(LANES), row(LANES)],
        out_specs=[spec for spec, _, _ in outs],
        out_shape=[jax.ShapeDtypeStruct(shape, dt) for _, shape, dt in outs],
        compiler_params=pltpu.CompilerParams(dimension_semantics=("parallel",),
                                             vmem_limit_bytes=VMEM_LIMIT),
        name="in_proj",
    )(h, norm, w_all, wa2p, ba, cos2, sin2s)


def _gla_kernel(qk_ref, v_ref, r_ref, g_ref, hnorm_ref, esel_ref, o_ref, s_ref):
    n_sub = GLA_STEP // GLA_SUB

    @pl.when(pl.program_id(0) == 0)
    def _():
        s_ref[...] = jnp.zeros(s_ref.shape, F32)

    q = qk_ref[:, :GLA_QK] * (GLA_DK ** -0.5)
    k = qk_ref[:, GLA_QK:]
    v = v_ref[...]
    g = g_ref[...]

    ri = lax.broadcasted_iota(jnp.int32, (GLA_STEP, GLA_STEP), 0)
    ci = lax.broadcasted_iota(jnp.int32, (GLA_STEP, GLA_STEP), 1)
    tri = ((ri // GLA_SUB == ci // GLA_SUB) & (ci <= ri)).astype(F32)
    b = jnp.dot(tri, g, preferred_element_type=F32, precision=lax.Precision.HIGHEST)

    b3 = b.reshape(n_sub, GLA_SUB, GLA_QK)
    k3 = k.reshape(n_sub, GLA_SUB, GLA_QK)
    b_last = jnp.broadcast_to(b3[:, GLA_SUB - 1:GLA_SUB, :], b3.shape).reshape(GLA_STEP, GLA_QK)

    pos = lax.broadcasted_iota(jnp.int32, (GLA_STEP, GLA_QK), 0) % GLA_SUB
    t_cols = []
    for j in range(GLA_SUB):
        kj = jnp.broadcast_to(k3[:, j:j + 1, :], k3.shape).reshape(GLA_STEP, GLA_QK)
        bj = jnp.broadcast_to(b3[:, j:j + 1, :], b3.shape).reshape(GLA_STEP, GLA_QK)
        e = jnp.where(pos >= j, b - bj, -jnp.inf)
        t_cols.append((q * kj * jnp.exp(e)).astype(BF16))
    a = _dot(jnp.concatenate(t_cols, axis=1), esel_ref[...]).astype(BF16)
    head_of_lane = lax.broadcasted_iota(jnp.int32, (GLA_SUB, GLA_WIDTH), 1) // GLA_DV
    o_intra = []
    for c in range(n_sub):
        rows = slice(c * GLA_SUB, (c + 1) * GLA_SUB)
        v_c = v[rows, :]
        v_heads = jnp.concatenate([jnp.where(head_of_lane == h, v_c, 0.0) for h in range(GLA_HEADS)]
                                  + [jnp.zeros((LANES - GLA_HEADS * GLA_SUB, GLA_WIDTH), F32)], axis=0)
        o_intra.append(_dot(a[rows, :], v_heads.astype(BF16)))
    o = jnp.concatenate(o_intra, axis=0)

    qe = q * jnp.exp(b)
    kd = k * jnp.exp(b_last - b)
    kd_t = kd.T
    dec_t = jnp.exp(b).T
    lane_head = lax.broadcasted_iota(jnp.int32, (GLA_SUB, GLA_QK), 1) // GLA_DK
    s = s_ref[...]
    o_inter = []
    for c in range(n_sub):
        rows = slice(c * GLA_SUB, (c + 1) * GLA_SUB)
        qe_c = qe[rows, :]
        q_stack = jnp.concatenate(
            [jnp.where(lane_head == h, qe_c, 0.0) for h in range(GLA_HEADS)], axis=0).astype(BF16)
        r_stack = _dot(q_stack, s.astype(BF16))
        o_inter.append(jnp.concatenate(
            [r_stack[h * GLA_SUB:(h + 1) * GLA_SUB, :] for h in range(GLA_HEADS)], axis=1))
        kd_c = kd_t[:, rows].astype(BF16)
        v_c = v[rows, :].astype(BF16)
        u_c = jnp.concatenate(
            [_dot(kd_c[h * GLA_DK:(h + 1) * GLA_DK, :], v_c[:, h * GLA_DV:(h + 1) * GLA_DV])
             for h in range(GLA_HEADS)], axis=0)
        last = c * GLA_SUB + GLA_SUB - 1
        s = dec_t[:, last:last + 1] * s + u_c
    s_ref[...] = s
    o = o + jnp.concatenate(o_inter, axis=0)

    r = r_ref[...]
    gate = r * jax.nn.sigmoid(r)
    hn = hnorm_ref[...]
    outs = []
    for h in range(GLA_HEADS):
        oh = o[:, h * GLA_DV:(h + 1) * GLA_DV]
        outs.append(_rmsnorm(oh, hn))
    o_ref[...] = (jnp.concatenate(outs, axis=1) * gate).astype(o_ref.dtype)


def _gla_head_selector():
    r = np.arange(GLA_SUB * GLA_QK)
    j, h = r // GLA_QK, (r % GLA_QK) // GLA_DK
    sel = np.zeros((GLA_SUB * GLA_QK, LANES), np.float32)
    sel[r, h * GLA_SUB + j] = 1.0
    return jnp.asarray(sel, BF16)


def _gla(gqk, gv, gr, gg, hnorm):
    lp = gqk.shape[0]
    assert GLA_HEADS * GLA_SUB <= LANES

    def row(width):
        return pl.BlockSpec((GLA_STEP, width), lambda i: (i, 0))

    return pl.pallas_call(
        _gla_kernel,
        grid=(lp // GLA_STEP,),
        in_specs=[row(2 * GLA_QK), row(GLA_WIDTH), row(GLA_WIDTH), row(GLA_QK), _const_spec((1, GLA_DV)),
                  _const_spec((GLA_SUB * GLA_QK, LANES))],
        out_specs=row(GLA_WIDTH),
        out_shape=jax.ShapeDtypeStruct((lp, GLA_WIDTH), BF16),
        scratch_shapes=[pltpu.VMEM((GLA_QK, GLA_DV), F32)],
        compiler_params=pltpu.CompilerParams(dimension_semantics=("arbitrary",),
                                             vmem_limit_bytes=VMEM_LIMIT),
        name="gla",
    )(gqk, gv, gr, gg, hnorm, _gla_head_selector())


def _key_to_float(u):
    bits = jnp.where(u >= 0, u, u ^ jnp.int32(0x7FFFFFFF))
    return lax.bitcast_convert_type(bits, F32)


def _float_to_key(f):
    bits = lax.bitcast_convert_type(f, jnp.int32)
    return jnp.where(bits >= 0, bits, bits ^ jnp.int32(0x7FFFFFFF))


_KEY_NEG_INF = np.int32(np.uint32(0xFF800000) ^ np.uint32(0x7FFFFFFF))


def _tile_loop(n, body, carry, unroll):
    trips = n // unroll

    def group(u, c):
        for r in range(unroll):
            c = body(u * unroll + r, c)
        return c

    carry = lax.fori_loop(0, trips, group, carry)
    return lax.fori_loop(trips * unroll, n, body, carry)


def _fold_rows(x, op):
    parts = [x[r:r + SUBLANES, :] for r in range(0, x.shape[0], SUBLANES)]
    while len(parts) > 1:
        parts = [op(parts[a], parts[a + 1]) if a + 1 < len(parts) else parts[a] for a in range(0, len(parts), 2)]
    return parts[0]


def _dsa_block(qt_ref, iqt_ref, iwt_ref, k_ref, vt_ref, ik_ref, o_ref, sc_ref, bias_ref, acc_ref, rhs_ref,
               sa_ref, sb_ref, gm_ref, k_sel, l_tok):
    i = pl.program_id(0)
    n_tiles = (i * QBLK + QBLK + KTILE - 1) // KTILE
    last = n_tiles - 1
    q_pos = i * QBLK + lax.broadcasted_iota(jnp.int32, (KSUB, QBLK), 1)
    key_row = lax.broadcasted_iota(jnp.int32, (KSUB, QBLK), 0)

    def visible(ks):
        return (ks + key_row) <= q_pos

    def peeled(body, carry):
        carry = _tile_loop(last, functools.partial(body, masked=False), carry, 4)
        return body(last, carry, masked=True)

    iqt = iqt_ref[...]
    iq_all = jnp.concatenate([iqt[h * IDX_HD:(h + 1) * IDX_HD, :] for h in range(IDX_HEADS)], axis=1)
    iw = iwt_ref[MISC_IW:MISC_IW + IDX_HEADS, :]
    gm_ref[...] = jnp.full(gm_ref.shape, -jnp.inf, F32)

    def score_tile(t, carry, masked):
        for sub in range(KTILE // KSUB):
            ks = pl.multiple_of(t * KTILE + sub * KSUB, KSUB)
            lg = _dot(ik_ref[pl.ds(ks, KSUB), :], iq_all)
            sc = iw[0:1, :] * jnp.maximum(lg[:, 0:QBLK], 0.0)
            for h in range(1, IDX_HEADS):
                sc = sc + iw[h:h + 1, :] * jnp.maximum(lg[:, h * QBLK:(h + 1) * QBLK], 0.0)
            if masked:
                sc = jnp.where(visible(ks), sc, -jnp.inf)
            sc_ref[pl.ds(ks, KSUB), :] = sc
            grp = slice((sub % 2) * KSUB, (sub % 2 + 1) * KSUB)
            gm_ref[grp, :] = jnp.maximum(gm_ref[grp, :], sc)
        return carry

    peeled(score_tile, 0)

    kk = float(k_sel)

    def count(pred):
        def body(t, acc):
            ks = pl.multiple_of(t * KTILE, KTILE)
            return acc + _fold_rows(jnp.where(pred(sc_ref[pl.ds(ks, KTILE), :]), 1.0, 0.0), jnp.add)
        acc = _tile_loop(n_tiles, body, jnp.zeros((SUBLANES, QBLK), F32), 4)
        return jnp.sum(acc, axis=0, keepdims=True)

    def count_ge(cand):
        return count(lambda x: x >= cand)

    gm = gm_ref[...]
    lo = jnp.maximum(_float_to_key(jnp.min(_fold_rows(gm, jnp.minimum), axis=0, keepdims=True)), _KEY_NEG_INF)
    hi = _float_to_key(jnp.max(_fold_rows(gm, jnp.maximum), axis=0, keepdims=True))
    n_lo = count_ge(_key_to_float(lo))
    n_lo = jnp.where(q_pos[0:1, :] < l_tok, n_lo, 0.0)

    def settled(lo, hi, n_lo):
        return jnp.logical_or(n_lo <= kk, lo >= hi)

    def any_open(lo, hi, n_lo):
        return jnp.max(jnp.where(settled(lo, hi, n_lo), 0.0, 1.0))

    def midpoint(lo, hi):
        return (lo | hi) - ((lo ^ hi) >> 1)

    def probe_at(key, lo, hi, n_lo):
        open_ = jnp.logical_not(settled(lo, hi, n_lo))
        cnt = count_ge(_key_to_float(key))
        up = jnp.logical_and(open_, cnt >= kk)
        down = jnp.logical_and(open_, cnt < kk)
        return jnp.where(up, key, lo), jnp.where(down, key - 1, hi), jnp.where(up, cnt, n_lo), cnt, up, down

    def halve(lo, hi, n_lo):
        return probe_at(midpoint(lo, hi), lo, hi, n_lo)[:3]

    log_k = float(np.log(k_sel))

    def interpolate(st):
        lo, hi, n_lo, g_lo, g_hi, side = st
        f_lo, f_hi = _key_to_float(lo), _key_to_float(hi)
        guess = f_lo + (g_lo - log_k) / jnp.maximum(g_lo - g_hi, 1e-9) * (f_hi - f_lo)
        usable = jnp.logical_and(lo > _KEY_NEG_INF, jnp.abs(guess) < jnp.inf)
        key = jnp.where(usable, jnp.clip(_float_to_key(guess), lo + 1, hi), midpoint(lo, hi))
        lo, hi, n_lo, cnt, up, down = probe_at(key, lo, hi, n_lo)
        g_cnt = jnp.log(jnp.maximum(cnt, 0.5))
        g_hi = jnp.where(down, g_cnt, jnp.where(jnp.logical_and(up, side > 0.0), 0.5 * (g_hi + log_k), g_hi))
        g_lo = jnp.where(up, g_cnt, jnp.where(jnp.logical_and(down, side < 0.0), 0.5 * (g_lo + log_k), g_lo))
        side = jnp.where(up, 1.0, jnp.where(down, -1.0, side))
        return lo, hi, n_lo, g_lo, g_hi, side

    def checked_halve(st):
        lo, hi, n_lo = halve(*halve(*st[:3]))
        return lo, hi, n_lo, any_open(lo, hi, n_lo), st[4] + 1

    lo, hi, n_lo, _, _, _ = lax.fori_loop(
        0, INTERPOLATED_PROBES, lambda _, st: interpolate(st),
        (lo, hi, n_lo, jnp.log(jnp.maximum(n_lo, 0.5)), jnp.full((1, QBLK), float(np.log(0.5)), F32),
         jnp.zeros((1, QBLK), F32)))
    lo, hi, n_lo = lax.fori_loop(0, UNCHECKED_HALVINGS, lambda _, st: halve(*st), (lo, hi, n_lo))
    lo, hi, n_lo, _, _ = lax.while_loop(lambda st: jnp.logical_and(st[3] > 0.0, st[4] < 17), checked_halve,
                                        (lo, hi, n_lo, any_open(lo, hi, n_lo), jnp.int32(0)))
    few = n_lo < kk
    lo_f = _key_to_float(lo)

    def min_ge_body(t, acc):
        ks = pl.multiple_of(t * KTILE, KTILE)
        x = sc_ref[pl.ds(ks, KTILE), :]
        return jnp.minimum(acc, _fold_rows(jnp.where(x >= lo_f, x, jnp.inf), jnp.minimum))

    thr = jnp.min(_tile_loop(n_tiles, min_ge_body, jnp.full((SUBLANES, QBLK), jnp.inf, F32), 4),
                  axis=0, keepdims=True)
    thr = jnp.where(few, -jnp.inf, thr)
    has_ties = jnp.max(jnp.where(few, 0.0, n_lo)) > kk

    def bias_fast(t, carry, masked):
        for sub in range(KTILE // KSUB):
            ks = pl.multiple_of(t * KTILE + sub * KSUB, KSUB)
            keep = sc_ref[pl.ds(ks, KSUB), :] >= thr
            if masked:
                keep = jnp.logical_and(keep, visible(ks))
            bias_ref[pl.ds(ks, KSUB), :] = jnp.where(keep, 0.0, NEG_BIG).astype(BF16)
        return carry

    def bias_ties(t, run, masked, budget, tri):
        for sub in range(KTILE // KSUB):
            ks = pl.multiple_of(t * KTILE + sub * KSUB, KSUB)
            x = sc_ref[pl.ds(ks, KSUB), :]
            above = x > thr
            tie = x == thr
            tie_f = jnp.where(tie, 1.0, 0.0)
            rank = run + _dot(tri, tie_f.astype(BF16))
            keep = jnp.logical_or(above, jnp.logical_and(tie, rank <= budget))
            if masked:
                keep = jnp.logical_and(keep, visible(ks))
            bias_ref[pl.ds(ks, KSUB), :] = jnp.where(keep, 0.0, NEG_BIG).astype(BF16)
            run = run + jnp.sum(tie_f, axis=0, keepdims=True)
        return run

    @pl.when(jnp.logical_not(has_ties))
    def _():
        peeled(bias_fast, 0)

    @pl.when(has_ties)
    def _():
        budget = kk - count(lambda x: x > thr)
        tri = (lax.broadcasted_iota(jnp.int32, (KSUB, KSUB), 1) <= key_row).astype(BF16)
        peeled(functools.partial(bias_ties, budget=budget, tri=tri), jnp.zeros((1, QBLK), F32))

    qt = qt_ref[...]
    eye = (lax.broadcasted_iota(jnp.int32, (QBLK, QBLK), 0)
           == lax.broadcasted_iota(jnp.int32, (QBLK, QBLK), 1)).astype(BF16)
    zero = jnp.zeros((DSA_HD, QBLK), BF16)
    n_pairs = DSA_HEADS // 2
    for c in range(n_pairs):
        qa = qt[(2 * c) * DSA_HD:(2 * c + 1) * DSA_HD, :]
        qb = qt[(2 * c + 1) * DSA_HD:(2 * c + 2) * DSA_HD, :]
        rhs_ref[c] = jnp.concatenate([jnp.concatenate([qa, zero], axis=1),
                                      jnp.concatenate([zero, qb], axis=1),
                                      jnp.concatenate([eye, eye], axis=1)], axis=0)
    acc_ref[...] = jnp.zeros(acc_ref.shape, F32)
    ones_rows = jnp.ones((ACC_ROWS - DSA_HD, ATILE), BF16)

    def qk_stage(ks, s_ref):
        bias = bias_ref[pl.ds(ks, ATILE), :]
        tile_max = []
        for c in range(n_pairs):
            lhs = jnp.concatenate([k_ref[pl.ds(ks, ATILE), c * 2 * DSA_HD:(c + 1) * 2 * DSA_HD], bias], axis=1)
            s = _dot(lhs, rhs_ref[c])
            s_ref[c] = s
            tile_max.append(_fold_rows(s, jnp.maximum))
        return tuple(tile_max)

    def pv_stage(ks, s_ref, tile_max, ms):
        new_ms = []
        for c in range(n_pairs):
            m_new = jnp.maximum(ms[c], jnp.max(tile_max[c], axis=0, keepdims=True))
            alpha = jnp.exp(ms[c] - m_new)
            p = jnp.exp(s_ref[c] - m_new).astype(BF16)
            for hh in range(2):
                h = 2 * c + hh
                rows = slice(h * ACC_ROWS, (h + 1) * ACC_ROWS)
                vt_aug = jnp.concatenate([vt_ref[h * DSA_HD:(h + 1) * DSA_HD, pl.ds(ks, ATILE)], ones_rows], axis=0)
                pv = _dot(vt_aug, p[:, hh * QBLK:(hh + 1) * QBLK])
                acc_ref[rows, :] = alpha[:, hh * QBLK:(hh + 1) * QBLK] * acc_ref[rows, :] + pv
            new_ms.append(m_new)
        return tuple(new_ms)

    def attn_tile(t, carry):
        max_a, ms = carry
        k0 = pl.multiple_of(t * KTILE, KTILE)
        k1 = pl.multiple_of(t * KTILE + ATILE, ATILE)
        k2 = pl.multiple_of(t * KTILE + KTILE, KTILE)
        max_b = qk_stage(k1, sb_ref)
        ms = pv_stage(k0, sa_ref, max_a, ms)
        max_a = qk_stage(k2, sa_ref)
        ms = pv_stage(k1, sb_ref, max_b, ms)
        return max_a, ms

    ms = tuple(jnp.full((1, 2 * QBLK), NEG_BIG, F32) for _ in range(n_pairs))
    max_a, ms = _tile_loop(last, attn_tile, (qk_stage(0, sa_ref), ms), 4)
    k0 = pl.multiple_of(last * KTILE, KTILE)
    k1 = pl.multiple_of(last * KTILE + ATILE, ATILE)
    max_b = qk_stage(k1, sb_ref)
    ms = pv_stage(k0, sa_ref, max_a, ms)
    pv_stage(k1, sb_ref, max_b, ms)
    outs = []
    for h in range(DSA_HEADS):
        a = acc_ref[h * ACC_ROWS:(h + 1) * ACC_ROWS, :]
        outs.append(a[:DSA_HD, :] / a[DSA_HD:DSA_HD + 1, :])
    o_ref[...] = jnp.concatenate(outs, axis=0).T.astype(o_ref.dtype)


def _dsa_kernel(qt_ref, iqt_ref, iwt_ref, k_ref, vt_ref, ik_ref, o_ref, sc_ref, bias_ref, acc_ref, rhs_ref,
                sa_ref, sb_ref, gm_ref, *, k_sel, l_tok):
    has_tokens = pl.program_id(0) * QBLK < l_tok

    @pl.when(has_tokens)
    def _():
        _dsa_block(qt_ref, iqt_ref, iwt_ref, k_ref, vt_ref, ik_ref, o_ref, sc_ref, bias_ref, acc_ref, rhs_ref,
                   sa_ref, sb_ref, gm_ref, k_sel, l_tok)

    @pl.when(jnp.logical_not(has_tokens))
    def _():
        o_ref[...] = jnp.zeros(o_ref.shape, o_ref.dtype)


def _dsa(dqt, dk, dvt, iqt, ik, iwt, k_sel, l_tok):
    lp = dk.shape[0]

    def col(height):
        return pl.BlockSpec((height, QBLK), lambda i: (0, i))

    return pl.pallas_call(
        functools.partial(_dsa_kernel, k_sel=k_sel, l_tok=l_tok),
        grid=(lp // QBLK,),
        in_specs=[col(DSA_WIDTH), col(IDX_WIDTH), col(LANES), _const_spec((lp, DSA_WIDTH)),
                  _const_spec((DSA_WIDTH, lp)), _const_spec((lp, IDX_HD))],
        out_specs=pl.BlockSpec((QBLK, DSA_WIDTH), lambda i: (i, 0)),
        out_shape=jax.ShapeDtypeStruct((lp, DSA_WIDTH), BF16),
        scratch_shapes=[pltpu.VMEM((lp, QBLK), F32), pltpu.VMEM((lp, QBLK), BF16),
                        pltpu.VMEM((DSA_HEADS * ACC_ROWS, QBLK), F32),
                        pltpu.VMEM((DSA_HEADS // 2, 2 * LANES, 2 * QBLK), BF16),
                        pltpu.VMEM((DSA_HEADS // 2, ATILE, 2 * QBLK), F32),
                        pltpu.VMEM((DSA_HEADS // 2, ATILE, 2 * QBLK), F32),
                        pltpu.VMEM((N_GROUPS, QBLK), F32)],
        compiler_params=pltpu.CompilerParams(dimension_semantics=("parallel",),
                                             vmem_limit_bytes=VMEM_LIMIT),
        name="dsa",
    )(dqt, iqt, iwt, dk, dvt, ik)


def _out_ffn2_kernel(h_ref, og_ref, od_ref, wo_ref, norm_ref, wg_ref, wu_ref, wd_ref, fnorm_ref, o_ref, *,
                     ff_chunk):
    h2 = h_ref[...] + _dot(og_ref[...], wo_ref[:GLA_WIDTH, :]) + _dot(od_ref[...], wo_ref[GLA_WIDTH:, :])
    h3 = _swiglu_half(h2, norm_ref, wg_ref, wu_ref, wd_ref, ff_chunk)
    o_ref[...] = _rmsnorm(h3, fnorm_ref[...])


def _out_ffn2(h, o_gla, o_dsa, wo, norm, wg, wu, wd, fnorm):
    lp, d = h.shape
    d_ff = wg.shape[1]

    def row(width):
        return pl.BlockSpec((ROW_TILE, width), lambda i: (i, 0))

    return pl.pallas_call(
        functools.partial(_out_ffn2_kernel, ff_chunk=_ffn_chunk(d_ff)),
        grid=(lp // ROW_TILE,),
        in_specs=[row(d), row(GLA_WIDTH), row(DSA_WIDTH), _const_spec((GLA_WIDTH + DSA_WIDTH, d)),
                  _const_spec((1, d)), _const_spec((d, d_ff)), _const_spec((d, d_ff)), _const_spec((d_ff, d)),
                  _const_spec((1, d))],
        out_specs=row(d),
        out_shape=jax.ShapeDtypeStruct((lp, d), F32),
        compiler_params=pltpu.CompilerParams(dimension_semantics=("parallel",),
                                             vmem_limit_bytes=VMEM_LIMIT),
        name="out_ffn2",
    )(h, o_gla, o_dsa, wo, norm, wg, wu, wd, fnorm)


def _rope_tables(lp):
    pos = jnp.arange(lp, dtype=F32)
    inv_freq = 1.0 / (ROPE_THETA ** (jnp.arange(0, DSA_HD, 2, dtype=F32) / DSA_HD))
    ang = pos[:, None] * inv_freq[None, :]
    cos = jnp.cos(ang)
    sin = jnp.sin(ang)
    cos2 = jnp.concatenate([cos, cos, cos, cos], axis=-1)
    sin2s = jnp.concatenate([-sin, sin, -sin, sin], axis=-1)
    return cos2, sin2s


def _layer(h, k_sel, l_tok, ffn1_norm, ffn1_w_gate, ffn1_w_up, ffn1_w_down, mix_norm, w_in, gla_w_a2, gla_b_a,
           gla_head_norm, w_out, ffn2_norm, ffn2_w_gate, ffn2_w_up, ffn2_w_down, final_norm, cos2, sin2s):
    d = h.shape[1]
    h1 = _ffn1(h, ffn1_norm.reshape(1, d), ffn1_w_gate.astype(BF16), ffn1_w_up.astype(BF16),
               ffn1_w_down.astype(BF16))

    sizes = (GLA_QK, GLA_QK, GLA_WIDTH, GLA_WIDTH, GLA_LOWRANK, DSA_WIDTH, DSA_WIDTH, DSA_WIDTH, IDX_WIDTH,
             IDX_HD, IDX_HEADS)
    offs = np.concatenate([[0], np.cumsum(sizes)])
    col = lambda n: w_in[:, offs[n]:offs[n + 1]]
    pad = jnp.zeros((d, LANES - IDX_HD - IDX_HEADS - GLA_LOWRANK), w_in.dtype)
    w_all = jnp.concatenate([col(0), col(1), col(2), col(3), col(5), col(6), col(7), col(8),
                             col(9), col(10), col(4), pad], axis=1).astype(BF16)
    wa2p = jnp.zeros((LANES, GLA_QK), F32).at[MISC_GA:MISC_GA + GLA_LOWRANK, :].set(gla_w_a2).astype(BF16)

    gqk, gv, gr, gg, dqt, dk, dvt, iqt, ik, iwt = _in_proj(
        h1, mix_norm.reshape(1, d), w_all, wa2p, gla_b_a.reshape(1, GLA_QK), cos2, sin2s)
    o_gla = _gla(gqk, gv, gr, gg, gla_head_norm.reshape(1, GLA_DV))
    o_dsa = _dsa(dqt, dk, dvt, iqt, ik, iwt, k_sel, l_tok)
    return _out_ffn2(h1, o_gla, o_dsa, w_out.astype(BF16), ffn2_norm.reshape(1, d), ffn2_w_gate.astype(BF16),
                     ffn2_w_up.astype(BF16), ffn2_w_down.astype(BF16), final_norm.reshape(1, d))


def kernel(x, meta_tokens, ffn1_norm, ffn1_w_gate, ffn1_w_up, ffn1_w_down, mix_norm, w_in, gla_w_a2, gla_b_a,
           gla_head_norm, w_out, ffn2_norm, ffn2_w_gate, ffn2_w_up, ffn2_w_down, final_norm):
    batch, seq, d = x.shape
    depth = ffn1_norm.shape[0]
    assert depth == 1, "the final norm is fused into the layer's last kernel"
    k_sel = min(TOPK_MAX, seq // 4)
    l_tok = seq + N_META
    lp = -(-l_tok // ROW_PAD) * ROW_PAD
    cos2, sin2s = _rope_tables(lp)
    outs = []
    for bi in range(batch):
        h = jnp.concatenate([meta_tokens.astype(x.dtype), x[bi], jnp.zeros((lp - l_tok, d), x.dtype)], axis=0)
        h = _layer(h, k_sel, l_tok, ffn1_norm[0], ffn1_w_gate[0], ffn1_w_up[0], ffn1_w_down[0], mix_norm[0],
                   w_in[0], gla_w_a2[0], gla_b_a[0], gla_head_norm[0], w_out[0], ffn2_norm[0], ffn2_w_gate[0],
                   ffn2_w_up[0], ffn2_w_down[0], final_norm, cos2, sin2s)
        outs.append(h[N_META:l_tok])
    return jnp.stack(outs, axis=0)
```

```python
import functools

import numpy as np
import jax
import jax.numpy as jnp
from jax import lax
from jax.experimental import pallas as pl
from jax.experimental.pallas import tpu as pltpu

N_META = 16
EPS = 1e-6
ROPE_THETA = 10000.0
GLA_HEADS = 4
GLA_DK = 64
GLA_DV = 128
GLA_LOWRANK = 16
GLA_GATE_NORM = 16.0
DSA_HEADS = 8
DSA_HD = 64
IDX_HEADS = 8
IDX_HD = 64
TOPK_MAX = 256

GLA_QK = GLA_HEADS * GLA_DK
GLA_WIDTH = GLA_HEADS * GLA_DV
DSA_WIDTH = DSA_HEADS * DSA_HD
IDX_WIDTH = IDX_HEADS * IDX_HD

LANES = 128
SUBLANES = 8
ROW_TILE = 384
QBLK = 128
KSUB = 128
KTILE = 512
ATILE = KTILE // 2
N_GROUPS = 2 * KSUB
INTERPOLATED_PROBES = 8
UNCHECKED_HALVINGS = 4
GLA_STEP = 128
GLA_SUB = 16
ROW_PAD = 1536
ACC_ROWS = DSA_HD + 16
VMEM_LIMIT = 60 * 1024 * 1024
NEG_BIG = -1e30

MISC_IK = 0
MISC_IW = IDX_HD
MISC_GA = IDX_HD + IDX_HEADS

F32 = jnp.float32
BF16 = jnp.bfloat16


def _dot(a, b):
    return jnp.dot(a, b, preferred_element_type=F32)


def _rmsnorm(x, g):
    return x * lax.rsqrt(jnp.mean(x * x, axis=-1, keepdims=True) + EPS) * g


def _const_spec(shape):
    return pl.BlockSpec(shape, lambda *_: (0,) * len(shape), pipeline_mode=pl.Buffered(1))


def _swiglu_half(x, norm_ref, wg_ref, wu_ref, wd_ref, ff_chunk):
    hn = _rmsnorm(x, norm_ref[...]).astype(BF16)
    d_ff = wg_ref.shape[1]
    acc = jnp.zeros(x.shape, F32)
    for c in range(d_ff // ff_chunk):
        sl = slice(c * ff_chunk, (c + 1) * ff_chunk)
        g = _dot(hn, wg_ref[:, sl])
        u = _dot(hn, wu_ref[:, sl])
        a = (g * jax.nn.sigmoid(g) * u).astype(BF16)
        acc = acc + _dot(a, wd_ref[sl, :])
    return x + 0.5 * acc


def _ffn1_kernel(h_ref, norm_ref, wg_ref, wu_ref, wd_ref, o_ref, *, ff_chunk):
    o_ref[...] = _swiglu_half(h_ref[...], norm_ref, wg_ref, wu_ref, wd_ref, ff_chunk)


def _ffn_chunk(d_ff):
    if d_ff <= 4096:
        return d_ff
    for c in (2048, 1024, 512, 256):
        if d_ff % c == 0:
            return c
    return d_ff


def _ffn1(h, norm, wg, wu, wd):
    lp, d = h.shape
    d_ff = wg.shape[1]
    row = pl.BlockSpec((ROW_TILE, d), lambda i: (i, 0))
    return pl.pallas_call(
        functools.partial(_ffn1_kernel, ff_chunk=_ffn_chunk(d_ff)),
        grid=(lp // ROW_TILE,),
        in_specs=[row, _const_spec((1, d)), _const_spec((d, d_ff)), _const_spec((d, d_ff)),
                  _const_spec((d_ff, d))],
        out_specs=row,
        out_shape=jax.ShapeDtypeStruct((lp, d), F32),
        compiler_params=pltpu.CompilerParams(dimension_semantics=("parallel",),
                                             vmem_limit_bytes=VMEM_LIMIT),
        name="ffn1",
    )(h, norm, wg, wu, wd)


def _rope(x, cos, sin_signed, first_half):
    parts = []
    for c in range(x.shape[1] // LANES):
        xc = x[:, c * LANES:(c + 1) * LANES]
        rot = jnp.where(first_half, pltpu.roll(xc, LANES - 32, 1), pltpu.roll(xc, 32, 1))
        parts.append(xc * cos + rot * sin_signed)
    return parts[0] if len(parts) == 1 else jnp.concatenate(parts, axis=1)


def _log_sigmoid(x):
    return jnp.minimum(x, 0.0) - jnp.log(1.0 + jnp.exp(-jnp.abs(x)))


def _in_proj_kernel(h_ref, norm_ref, w_ref, wa2_ref, ba_ref, cos_ref, sin_ref,
                    gqk_ref, gv_ref, gr_ref, gg_ref, dqt_ref, dk_ref, dvt_ref, iqt_ref, ik_ref, iwt_ref):
    hn = _rmsnorm(h_ref[...], norm_ref[...]).astype(BF16)
    cos = cos_ref[...]
    sin_s = sin_ref[...]
    lane = lax.broadcasted_iota(jnp.int32, cos.shape, 1)
    first_half = (lane % DSA_HD) < (DSA_HD // 2)

    def proj(c0, width):
        return _dot(hn, w_ref[:, c0:c0 + width])

    c = 0
    gqk_ref[...] = proj(c, 2 * GLA_QK); c += 2 * GLA_QK
    gv_ref[...] = proj(c, GLA_WIDTH); c += GLA_WIDTH
    gr_ref[...] = proj(c, GLA_WIDTH); c += GLA_WIDTH
    dqt_ref[...] = (_rope(proj(c, DSA_WIDTH), cos, sin_s, first_half) * (DSA_HD ** -0.5)).T.astype(BF16)
    c += DSA_WIDTH
    dk_ref[...] = _rope(proj(c, DSA_WIDTH), cos, sin_s, first_half).astype(BF16); c += DSA_WIDTH
    dvt_ref[...] = proj(c, DSA_WIDTH).T.astype(BF16); c += DSA_WIDTH
    iqt_ref[...] = (_rope(proj(c, IDX_WIDTH), cos, sin_s, first_half) * (IDX_HD ** -0.5)).T.astype(BF16)
    c += IDX_WIDTH
    misc = proj(c, LANES)
    ik_ref[...] = _rope(misc, cos, sin_s, first_half)[:, MISC_IK:MISC_IK + IDX_HD].astype(BF16)
    iwt_ref[...] = (misc * (IDX_HEADS ** -0.5)).T
    pre = _dot(misc.astype(BF16), wa2_ref[...]) + ba_ref[...]
    gg_ref[...] = _log_sigmoid(pre) * (1.0 / GLA_GATE_NORM)


def _in_proj(h, norm, w_all, wa2p, ba, cos2, sin2s):
    lp, d = h.shape
    ncols = w_all.shape[1]

    def row(width):
        return pl.BlockSpec((ROW_TILE, width), lambda i: (i, 0))

    def col(height):
        return pl.BlockSpec((height, ROW_TILE), lambda i: (0, i))

    outs = [
        (row(2 * GLA_QK), (lp, 2 * GLA_QK), F32), (row(GLA_WIDTH), (lp, GLA_WIDTH), F32),
        (row(GLA_WIDTH), (lp, GLA_WIDTH), F32), (row(GLA_QK), (lp, GLA_QK), F32),
        (col(DSA_WIDTH), (DSA_WIDTH, lp), BF16), (row(DSA_WIDTH), (lp, DSA_WIDTH), BF16),
        (col(DSA_WIDTH), (DSA_WIDTH, lp), BF16), (col(IDX_WIDTH), (IDX_WIDTH, lp), BF16),
        (row(IDX_HD), (lp, IDX_HD), BF16), (col(LANES), (LANES, lp), F32),
    ]
    return pl.pallas_call(
        _in_proj_kernel,
        grid=(lp // ROW_TILE,),
        in_specs=[row(d), _const_spec((1, d)), _const_spec((d, ncols)), _const_spec((LANES, GLA_QK)),
                  _const_spec((1, GLA_QK)), row(LANES), row(LANES)],
        out_specs=[spec for spec, _, _ in outs],
        out_shape=[jax.ShapeDtypeStruct(shape, dt) for _, shape, dt in outs],
        compiler_params=pltpu.CompilerParams(dimension_semantics=("parallel",),
                                             vmem_limit_bytes=VMEM_LIMIT),
        name="in_proj",
    )(h, norm, w_all, wa2p, ba, cos2, sin2s)


def _gla_kernel(qk_ref, v_ref, r_ref, g_ref, hnorm_ref, esel_ref, o_ref, s_ref):
    n_sub = GLA_STEP // GLA_SUB

    @pl.when(pl.program_id(0) == 0)
    def _():
        s_ref[...] = jnp.zeros(s_ref.shape, F32)

    q = qk_ref[:, :GLA_QK] * (GLA_DK ** -0.5)
    k = qk_ref[:, GLA_QK:]
    v = v_ref[...]
    g = g_ref[...]

    ri = lax.broadcasted_iota(jnp.int32, (GLA_STEP, GLA_STEP), 0)
    ci = lax.broadcasted_iota(jnp.int32, (GLA_STEP, GLA_STEP), 1)
    tri = ((ri // GLA_SUB == ci // GLA_SUB) & (ci <= ri)).astype(F32)
    b = jnp.dot(tri, g, preferred_element_type=F32, precision=lax.Precision.HIGHEST)

    b3 = b.reshape(n_sub, GLA_SUB, GLA_QK)
    k3 = k.reshape(n_sub, GLA_SUB, GLA_QK)
    b_last = jnp.broadcast_to(b3[:, GLA_SUB - 1:GLA_SUB, :], b3.shape).reshape(GLA_STEP, GLA_QK)

    pos = lax.broadcasted_iota(jnp.int32, (GLA_STEP, GLA_QK), 0) % GLA_SUB
    t_cols = []
    for j in range(GLA_SUB):
        kj = jnp.broadcast_to(k3[:, j:j + 1, :], k3.shape).reshape(GLA_STEP, GLA_QK)
        bj = jnp.broadcast_to(b3[:, j:j + 1, :], b3.shape).reshape(GLA_STEP, GLA_QK)
        e = jnp.where(pos >= j, b - bj, -jnp.inf)
        t_cols.append((q * kj * jnp.exp(e)).astype(BF16))
    a = _dot(jnp.concatenate(t_cols, axis=1), esel_ref[...]).astype(BF16)
    head_of_lane = lax.broadcasted_iota(jnp.int32, (GLA_SUB, GLA_WIDTH), 1) // GLA_DV
    o_intra = []
    for c in range(n_sub):
        rows = slice(c * GLA_SUB, (c + 1) * GLA_SUB)
        v_c = v[rows, :]
        v_heads = jnp.concatenate([jnp.where(head_of_lane == h, v_c, 0.0) for h in range(GLA_HEADS)]
                                  + [jnp.zeros((LANES - GLA_HEADS * GLA_SUB, GLA_WIDTH), F32)], axis=0)
        o_intra.append(_dot(a[rows, :], v_heads.astype(BF16)))
    o = jnp.concatenate(o_intra, axis=0)

    qe = q * jnp.exp(b)
    kd = k * jnp.exp(b_last - b)
    kd_t = kd.T
    dec_t = jnp.exp(b).T
    lane_head = lax.broadcasted_iota(jnp.int32, (GLA_SUB, GLA_QK), 1) // GLA_DK
    s = s_ref[...]
    o_inter = []
    for c in range(n_sub):
        rows = slice(c * GLA_SUB, (c + 1) * GLA_SUB)
        qe_c = qe[rows, :]
        q_stack = jnp.concatenate(
            [jnp.where(lane_head == h, qe_c, 0.0) for h in range(GLA_HEADS)], axis=0).astype(BF16)
        r_stack = _dot(q_stack, s.astype(BF16))
        o_inter.append(jnp.concatenate(
            [r_stack[h * GLA_SUB:(h + 1) * GLA_SUB, :] for h in range(GLA_HEADS)], axis=1))
        kd_c = kd_t[:, rows].astype(BF16)
        v_c = v[rows, :].astype(BF16)
        u_c = jnp.concatenate(
            [_dot(kd_c[h * GLA_DK:(h + 1) * GLA_DK, :], v_c[:, h * GLA_DV:(h + 1) * GLA_DV])
             for h in range(GLA_HEADS)], axis=0)
        last = c * GLA_SUB + GLA_SUB - 1
        s = dec_t[:, last:last + 1] * s + u_c
    s_ref[...] = s
    o = o + jnp.concatenate(o_inter, axis=0)

    r = r_ref[...]
    gate = r * jax.nn.sigmoid(r)
    hn = hnorm_ref[...]
    outs = []
    for h in range(GLA_HEADS):
        oh = o[:, h * GLA_DV:(h + 1) * GLA_DV]
        outs.append(_rmsnorm(oh, hn))
    o_ref[...] = (jnp.concatenate(outs, axis=1) * gate).astype(o_ref.dtype)


def _gla_head_selector():
    r = np.arange(GLA_SUB * GLA_QK)
    j, h = r // GLA_QK, (r % GLA_QK) // GLA_DK
    sel = np.zeros((GLA_SUB * GLA_QK, LANES), np.float32)
    sel[r, h * GLA_SUB + j] = 1.0
    return jnp.asarray(sel, BF16)


def _gla(gqk, gv, gr, gg, hnorm):
    lp = gqk.shape[0]
    assert GLA_HEADS * GLA_SUB <= LANES

    def row(width):
        return pl.BlockSpec((GLA_STEP, width), lambda i: (i, 0))

    return pl.pallas_call(
        _gla_kernel,
        grid=(lp // GLA_STEP,),
        in_specs=[row(2 * GLA_QK), row(GLA_WIDTH), row(GLA_WIDTH), row(GLA_QK), _const_spec((1, GLA_DV)),
                  _const_spec((GLA_SUB * GLA_QK, LANES))],
        out_specs=row(GLA_WIDTH),
        out_shape=jax.ShapeDtypeStruct((lp, GLA_WIDTH), BF16),
        scratch_shapes=[pltpu.VMEM((GLA_QK, GLA_DV), F32)],
        compiler_params=pltpu.CompilerParams(dimension_semantics=("arbitrary",),
                                             vmem_limit_bytes=VMEM_LIMIT),
        name="gla",
    )(gqk, gv, gr, gg, hnorm, _gla_head_selector())


def _key_to_float(u):
    bits = jnp.where(u >= 0, u, u ^ jnp.int32(0x7FFFFFFF))
    return lax.bitcast_convert_type(bits, F32)


def _float_to_key(f):
    bits = lax.bitcast_convert_type(f, jnp.int32)
    return jnp.where(bits >= 0, bits, bits ^ jnp.int32(0x7FFFFFFF))


_KEY_NEG_INF = np.int32(np.uint32(0xFF800000) ^ np.uint32(0x7FFFFFFF))


def _tile_loop(n, body, carry, unroll):
    trips = n // unroll

    def group(u, c):
        for r in range(unroll):
            c = body(u * unroll + r, c)
        return c

    carry = lax.fori_loop(0, trips, group, carry)
    return lax.fori_loop(trips * unroll, n, body, carry)


def _fold_rows(x, op):
    parts = [x[r:r + SUBLANES, :] for r in range(0, x.shape[0], SUBLANES)]
    while len(parts) > 1:
        parts = [op(parts[a], parts[a + 1]) if a + 1 < len(parts) else parts[a] for a in range(0, len(parts), 2)]
    return parts[0]


def _dsa_block(qt_ref, iqt_ref, iwt_ref, k_ref, vt_ref, ik_ref, o_ref, sc_ref, bias_ref, acc_ref, rhs_ref,
               sa_ref, sb_ref, gm_ref, k_sel, l_tok):
    i = pl.program_id(0)
    n_tiles = (i * QBLK + QBLK + KTILE - 1) // KTILE
    last = n_tiles - 1
    q_pos = i * QBLK + lax.broadcasted_iota(jnp.int32, (KSUB, QBLK), 1)
    key_row = lax.broadcasted_iota(jnp.int32, (KSUB, QBLK), 0)

    def visible(ks):
        return (ks + key_row) <= q_pos

    def peeled(body, carry):
        carry = _tile_loop(last, functools.partial(body, masked=False), carry, 4)
        return body(last, carry, masked=True)

    iqt = iqt_ref[...]
    iq_all = jnp.concatenate([iqt[h * IDX_HD:(h + 1) * IDX_HD, :] for h in range(IDX_HEADS)], axis=1)
    iw = iwt_ref[MISC_IW:MISC_IW + IDX_HEADS, :]
    gm_ref[...] = jnp.full(gm_ref.shape, -jnp.inf, F32)

    def score_tile(t, carry, masked):
        for sub in range(KTILE // KSUB):
            ks = pl.multiple_of(t * KTILE + sub * KSUB, KSUB)
            lg = _dot(ik_ref[pl.ds(ks, KSUB), :], iq_all)
            sc = iw[0:1, :] * jnp.maximum(lg[:, 0:QBLK], 0.0)
            for h in range(1, IDX_HEADS):
                sc = sc + iw[h:h + 1, :] * jnp.maximum(lg[:, h * QBLK:(h + 1) * QBLK], 0.0)
            if masked:
                sc = jnp.where(visible(ks), sc, -jnp.inf)
            sc_ref[pl.ds(ks, KSUB), :] = sc
            grp = slice((sub % 2) * KSUB, (sub % 2 + 1) * KSUB)
            gm_ref[grp, :] = jnp.maximum(gm_ref[grp, :], sc)
        return carry

    peeled(score_tile, 0)

    kk = float(k_sel)

    def count(pred):
        def body(t, acc):
            ks = pl.multiple_of(t * KTILE, KTILE)
            return acc + _fold_rows(jnp.where(pred(sc_ref[pl.ds(ks, KTILE), :]), 1.0, 0.0), jnp.add)
        acc = _tile_loop(n_tiles, body, jnp.zeros((SUBLANES, QBLK), F32), 4)
        return jnp.sum(acc, axis=0, keepdims=True)

    def count_ge(cand):
        return count(lambda x: x >= cand)

    gm = gm_ref[...]
    lo = jnp.maximum(_float_to_key(jnp.min(_fold_rows(gm, jnp.minimum), axis=0, keepdims=True)), _KEY_NEG_INF)
    hi = _float_to_key(jnp.max(_fold_rows(gm, jnp.maximum), axis=0, keepdims=True))
    n_lo = count_ge(_key_to_float(lo))
    n_lo = jnp.where(q_pos[0:1, :] < l_tok, n_lo, 0.0)

    def settled(lo, hi, n_lo):
        return jnp.logical_or(n_lo <= kk, lo >= hi)

    def any_open(lo, hi, n_lo):
        return jnp.max(jnp.where(settled(lo, hi, n_lo), 0.0, 1.0))

    def midpoint(lo, hi):
        return (lo | hi) - ((lo ^ hi) >> 1)

    def probe_at(key, lo, hi, n_lo):
        open_ = jnp.logical_not(settled(lo, hi, n_lo))
        cnt = count_ge(_key_to_float(key))
        up = jnp.logical_and(open_, cnt >= kk)
        down = jnp.logical_and(open_, cnt < kk)
        return jnp.where(up, key, lo), jnp.where(down, key - 1, hi), jnp.where(up, cnt, n_lo), cnt, up, down

    def halve(lo, hi, n_lo):
        return probe_at(midpoint(lo, hi), lo, hi, n_lo)[:3]

    log_k = float(np.log(k_sel))

    def interpolate(st):
        lo, hi, n_lo, g_lo, g_hi, side = st
        f_lo, f_hi = _key_to_float(lo), _key_to_float(hi)
        guess = f_lo + (g_lo - log_k) / jnp.maximum(g_lo - g_hi, 1e-9) * (f_hi - f_lo)
        usable = jnp.logical_and(lo > _KEY_NEG_INF, jnp.abs(guess) < jnp.inf)
        key = jnp.where(usable, jnp.clip(_float_to_key(guess), lo + 1, hi), midpoint(lo, hi))
        lo, hi, n_lo, cnt, up, down = probe_at(key, lo, hi, n_lo)
        g_cnt = jnp.log(jnp.maximum(cnt, 0.5))
        g_hi = jnp.where(down, g_cnt, jnp.where(jnp.logical_and(up, side > 0.0), 0.5 * (g_hi + log_k), g_hi))
        g_lo = jnp.where(up, g_cnt, jnp.where(jnp.logical_and(down, side < 0.0), 0.5 * (g_lo + log_k), g_lo))
        side = jnp.where(up, 1.0, jnp.where(down, -1.0, side))
        return lo, hi, n_lo, g_lo, g_hi, side

    def checked_halve(st):
        lo, hi, n_lo = halve(*halve(*st[:3]))
        return lo, hi, n_lo, any_open(lo, hi, n_lo), st[4] + 1

    lo, hi, n_lo, _, _, _ = lax.fori_loop(
        0, INTERPOLATED_PROBES, lambda _, st: interpolate(st),
        (lo, hi, n_lo, jnp.log(jnp.maximum(n_lo, 0.5)), jnp.full((1, QBLK), float(np.log(0.5)), F32),
         jnp.zeros((1, QBLK), F32)))
    lo, hi, n_lo = lax.fori_loop(0, UNCHECKED_HALVINGS, lambda _, st: halve(*st), (lo, hi, n_lo))
    lo, hi, n_lo, _, _ = lax.while_loop(lambda st: jnp.logical_and(st[3] > 0.0, st[4] < 17), checked_halve,
                                        (lo, hi, n_lo, any_open(lo, hi, n_lo), jnp.int32(0)))
    few = n_lo < kk
    lo_f = _key_to_float(lo)

    def min_ge_body(t, acc):
        ks = pl.multiple_of(t * KTILE, KTILE)
        x = sc_ref[pl.ds(ks, KTILE), :]
        return jnp.minimum(acc, _fold_rows(jnp.where(x >= lo_f, x, jnp.inf), jnp.minimum))

    thr = jnp.min(_tile_loop(n_tiles, min_ge_body, jnp.full((SUBLANES, QBLK), jnp.inf, F32), 4),
                  axis=0, keepdims=True)
    thr = jnp.where(few, -jnp.inf, thr)
    has_ties = jnp.max(jnp.where(few, 0.0, n_lo)) > kk

    def bias_fast(t, carry, masked):
        for sub in range(KTILE // KSUB):
            ks = pl.multiple_of(t * KTILE + sub * KSUB, KSUB)
            keep = sc_ref[pl.ds(ks, KSUB), :] >= thr
            if masked:
                keep = jnp.logical_and(keep, visible(ks))
            bias_ref[pl.ds(ks, KSUB), :] = jnp.where(keep, 0.0, NEG_BIG).astype(BF16)
        return carry

    def bias_ties(t, run, masked, budget, tri):
        for sub in range(KTILE // KSUB):
            ks = pl.multiple_of(t * KTILE + sub * KSUB, KSUB)
            x = sc_ref[pl.ds(ks, KSUB), :]
            above = x > thr
            tie = x == thr
            tie_f = jnp.where(tie, 1.0, 0.0)
            rank = run + _dot(tri, tie_f.astype(BF16))
            keep = jnp.logical_or(above, jnp.logical_and(tie, rank <= budget))
            if masked:
                keep = jnp.logical_and(keep, visible(ks))
            bias_ref[pl.ds(ks, KSUB), :] = jnp.where(keep, 0.0, NEG_BIG).astype(BF16)
            run = run + jnp.sum(tie_f, axis=0, keepdims=True)
        return run

    @pl.when(jnp.logical_not(has_ties))
    def _():
        peeled(bias_fast, 0)

    @pl.when(has_ties)
    def _():
        budget = kk - count(lambda x: x > thr)
        tri = (lax.broadcasted_iota(jnp.int32, (KSUB, KSUB), 1) <= key_row).astype(BF16)
        peeled(functools.partial(bias_ties, budget=budget, tri=tri), jnp.zeros((1, QBLK), F32))

    qt = qt_ref[...]
    eye = (lax.broadcasted_iota(jnp.int32, (QBLK, QBLK), 0)
           == lax.broadcasted_iota(jnp.int32, (QBLK, QBLK), 1)).astype(BF16)
    zero = jnp.zeros((DSA_HD, QBLK), BF16)
    n_pairs = DSA_HEADS // 2
    for c in range(n_pairs):
        qa = qt[(2 * c) * DSA_HD:(2 * c + 1) * DSA_HD, :]
        qb = qt[(2 * c + 1) * DSA_HD:(2 * c + 2) * DSA_HD, :]
        rhs_ref[c] = jnp.concatenate([jnp.concatenate([qa, zero], axis=1),
                                      jnp.concatenate([zero, qb], axis=1),
                                      jnp.concatenate([eye, eye], axis=1)], axis=0)
    acc_ref[...] = jnp.zeros(acc_ref.shape, F32)
    ones_rows = jnp.ones((ACC_ROWS - DSA_HD, ATILE), BF16)

    def qk_stage(ks, s_ref):
        bias = bias_ref[pl.ds(ks, ATILE), :]
        tile_max = []
        for c in range(n_pairs):
            lhs = jnp.concatenate([k_ref[pl.ds(ks, ATILE), c * 2 * DSA_HD:(c + 1) * 2 * DSA_HD], bias], axis=1)
            s = _dot(lhs, rhs_ref[c])
            s_ref[c] = s
            tile_max.append(_fold_rows(s, jnp.maximum))
        return tuple(tile_max)

    def pv_stage(ks, s_ref, tile_max, ms):
        new_ms = []
        for c in range(n_pairs):
            m_new = jnp.maximum(ms[c], jnp.max(tile_max[c], axis=0, keepdims=True))
            alpha = jnp.exp(ms[c] - m_new)
            p = jnp.exp(s_ref[c] - m_new).astype(BF16)
            for hh in range(2):
                h = 2 * c + hh
                rows = slice(h * ACC_ROWS, (h + 1) * ACC_ROWS)
                vt_aug = jnp.concatenate([vt_ref[h * DSA_HD:(h + 1) * DSA_HD, pl.ds(ks, ATILE)], ones_rows], axis=0)
                pv = _dot(vt_aug, p[:, hh * QBLK:(hh + 1) * QBLK])
                acc_ref[rows, :] = alpha[:, hh * QBLK:(hh + 1) * QBLK] * acc_ref[rows, :] + pv
            new_ms.append(m_new)
        return tuple(new_ms)

    def attn_tile(t, carry):
        max_a, ms = carry
        k0 = pl.multiple_of(t * KTILE, KTILE)
        k1 = pl.multiple_of(t * KTILE + ATILE, ATILE)
        k2 = pl.multiple_of(t * KTILE + KTILE, KTILE)
        max_b = qk_stage(k1, sb_ref)
        ms = pv_stage(k0, sa_ref, max_a, ms)
        max_a = qk_stage(k2, sa_ref)
        ms = pv_stage(k1, sb_ref, max_b, ms)
        return max_a, ms

    ms = tuple(jnp.full((1, 2 * QBLK), NEG_BIG, F32) for _ in range(n_pairs))
    max_a, ms = _tile_loop(last, attn_tile, (qk_stage(0, sa_ref), ms), 4)
    k0 = pl.multiple_of(last * KTILE, KTILE)
    k1 = pl.multiple_of(last * KTILE + ATILE, ATILE)
    max_b = qk_stage(k1, sb_ref)
    ms = pv_stage(k0, sa_ref, max_a, ms)
    pv_stage(k1, sb_ref, max_b, ms)
    outs = []
    for h in range(DSA_HEADS):
        a = acc_ref[h * ACC_ROWS:(h + 1) * ACC_ROWS, :]
        outs.append(a[:DSA_HD, :] / a[DSA_HD:DSA_HD + 1, :])
    o_ref[...] = jnp.concatenate(outs, axis=0).T.astype(o_ref.dtype)


def _dsa_kernel(qt_ref, iqt_ref, iwt_ref, k_ref, vt_ref, ik_ref, o_ref, sc_ref, bias_ref, acc_ref, rhs_ref,
                sa_ref, sb_ref, gm_ref, *, k_sel, l_tok):
    has_tokens = pl.program_id(0) * QBLK < l_tok

    @pl.when(has_tokens)
    def _():
        _dsa_block(qt_ref, iqt_ref, iwt_ref, k_ref, vt_ref, ik_ref, o_ref, sc_ref, bias_ref, acc_ref, rhs_ref,
                   sa_ref, sb_ref, gm_ref, k_sel, l_tok)

    @pl.when(jnp.logical_not(has_tokens))
    def _():
        o_ref[...] = jnp.zeros(o_ref.shape, o_ref.dtype)


def _dsa(dqt, dk, dvt, iqt, ik, iwt, k_sel, l_tok):
    lp = dk.shape[0]

    def col(height):
        return pl.BlockSpec((height, QBLK), lambda i: (0, i))

    return pl.pallas_call(
        functools.partial(_dsa_kernel, k_sel=k_sel, l_tok=l_tok),
        grid=(lp // QBLK,),
        in_specs=[col(DSA_WIDTH), col(IDX_WIDTH), col(LANES), _const_spec((lp, DSA_WIDTH)),
                  _const_spec((DSA_WIDTH, lp)), _const_spec((lp, IDX_HD))],
        out_specs=pl.BlockSpec((QBLK, DSA_WIDTH), lambda i: (i, 0)),
        out_shape=jax.ShapeDtypeStruct((lp, DSA_WIDTH), BF16),
        scratch_shapes=[pltpu.VMEM((lp, QBLK), F32), pltpu.VMEM((lp, QBLK), BF16),
                        pltpu.VMEM((DSA_HEADS * ACC_ROWS, QBLK), F32),
                        pltpu.VMEM((DSA_HEADS // 2, 2 * LANES, 2 * QBLK), BF16),
                        pltpu.VMEM((DSA_HEADS // 2, ATILE, 2 * QBLK), F32),
                        pltpu.VMEM((DSA_HEADS // 2, ATILE, 2 * QBLK), F32),
                        pltpu.VMEM((N_GROUPS, QBLK), F32)],
        compiler_params=pltpu.CompilerParams(dimension_semantics=("parallel",),
                                             vmem_limit_bytes=VMEM_LIMIT),
        name="dsa",
    )(dqt, iqt, iwt, dk, dvt, ik)


def _out_ffn2_kernel(h_ref, ht_ref, og_ref, ogt_ref, od_ref, odt_ref, wo_ref, norm_ref, wg_ref, wu_ref, wd_ref,
                     fnorm_ref, o_ref, *, ff_chunk):
    def shifted(main_ref, tail_ref):
        return jnp.concatenate([main_ref[N_META:, :], tail_ref[...]], axis=0)

    h2 = (shifted(h_ref, ht_ref) + _dot(shifted(og_ref, ogt_ref), wo_ref[:GLA_WIDTH, :])
          + _dot(shifted(od_ref, odt_ref), wo_ref[GLA_WIDTH:, :]))
    h3 = _swiglu_half(h2, norm_ref, wg_ref, wu_ref, wd_ref, ff_chunk)
    o_ref[...] = _rmsnorm(h3, fnorm_ref[...])


def _out_ffn2(h, o_gla, o_dsa, wo, norm, wg, wu, wd, fnorm, seq):
    lp, d = h.shape
    d_ff = wg.shape[1]
    n_steps = -(-seq // ROW_TILE)
    assert ROW_TILE % N_META == 0 and n_steps * ROW_TILE + N_META <= lp
    tail_step = ROW_TILE // N_META

    def row(width):
        return pl.BlockSpec((ROW_TILE, width), lambda i: (i, 0))

    def tail(width):
        return pl.BlockSpec((N_META, width), lambda i: ((i + 1) * tail_step, 0))

    return pl.pallas_call(
        functools.partial(_out_ffn2_kernel, ff_chunk=_ffn_chunk(d_ff)),
        grid=(n_steps,),
        in_specs=[row(d), tail(d), row(GLA_WIDTH), tail(GLA_WIDTH), row(DSA_WIDTH), tail(DSA_WIDTH),
                  _const_spec((GLA_WIDTH + DSA_WIDTH, d)), _const_spec((1, d)), _const_spec((d, d_ff)),
                  _const_spec((d, d_ff)), _const_spec((d_ff, d)), _const_spec((1, d))],
        out_specs=row(d),
        out_shape=jax.ShapeDtypeStruct((seq, d), F32),
        compiler_params=pltpu.CompilerParams(dimension_semantics=("parallel",),
                                             vmem_limit_bytes=VMEM_LIMIT),
        name="out_ffn2",
    )(h, h, o_gla, o_gla, o_dsa, o_dsa, wo, norm, wg, wu, wd, fnorm)


def _rope_tables(lp):
    pos = jnp.arange(lp, dtype=F32)
    inv_freq = 1.0 / (ROPE_THETA ** (jnp.arange(0, DSA_HD, 2, dtype=F32) / DSA_HD))
    ang = pos[:, None] * inv_freq[None, :]
    cos = jnp.cos(ang)
    sin = jnp.sin(ang)
    cos2 = jnp.concatenate([cos, cos, cos, cos], axis=-1)
    sin2s = jnp.concatenate([-sin, sin, -sin, sin], axis=-1)
    return cos2, sin2s


def _layer(h, k_sel, l_tok, ffn1_norm, ffn1_w_gate, ffn1_w_up, ffn1_w_down, mix_norm, w_in, gla_w_a2, gla_b_a,
           gla_head_norm, w_out, ffn2_norm, ffn2_w_gate, ffn2_w_up, ffn2_w_down, final_norm, cos2, sin2s):
    d = h.shape[1]
    h1 = _ffn1(h, ffn1_norm.reshape(1, d), ffn1_w_gate.astype(BF16), ffn1_w_up.astype(BF16),
               ffn1_w_down.astype(BF16))

    sizes = (GLA_QK, GLA_QK, GLA_WIDTH, GLA_WIDTH, GLA_LOWRANK, DSA_WIDTH, DSA_WIDTH, DSA_WIDTH, IDX_WIDTH,
             IDX_HD, IDX_HEADS)
    offs = np.concatenate([[0], np.cumsum(sizes)])
    col = lambda n: w_in[:, offs[n]:offs[n + 1]]
    pad = jnp.zeros((d, LANES - IDX_HD - IDX_HEADS - GLA_LOWRANK), w_in.dtype)
    w_all = jnp.concatenate([col(0), col(1), col(2), col(3), col(5), col(6), col(7), col(8),
                             col(9), col(10), col(4), pad], axis=1).astype(BF16)
    wa2p = jnp.zeros((LANES, GLA_QK), F32).at[MISC_GA:MISC_GA + GLA_LOWRANK, :].set(gla_w_a2).astype(BF16)

    gqk, gv, gr, gg, dqt, dk, dvt, iqt, ik, iwt = _in_proj(
        h1, mix_norm.reshape(1, d), w_all, wa2p, gla_b_a.reshape(1, GLA_QK), cos2, sin2s)
    o_gla = _gla(gqk, gv, gr, gg, gla_head_norm.reshape(1, GLA_DV))
    o_dsa = _dsa(dqt, dk, dvt, iqt, ik, iwt, k_sel, l_tok)
    return _out_ffn2(h1, o_gla, o_dsa, w_out.astype(BF16), ffn2_norm.reshape(1, d), ffn2_w_gate.astype(BF16),
                     ffn2_w_up.astype(BF16), ffn2_w_down.astype(BF16), final_norm.reshape(1, d), l_tok - N_META)


def kernel(x, meta_tokens, ffn1_norm, ffn1_w_gate, ffn1_w_up, ffn1_w_down, mix_norm, w_in, gla_w_a2, gla_b_a,
           gla_head_norm, w_out, ffn2_norm, ffn2_w_gate, ffn2_w_up, ffn2_w_down, final_norm):
    batch, seq, d = x.shape
    depth = ffn1_norm.shape[0]
    assert depth == 1, "the final norm is fused into the layer's last kernel"
    k_sel = min(TOPK_MAX, seq // 4)
    l_tok = seq + N_META
    lp = -(-max(l_tok, -(-seq // ROW_TILE) * ROW_TILE + N_META) // ROW_PAD) * ROW_PAD
    cos2, sin2s = _rope_tables(lp)
    outs = []
    for bi in range(batch):
        h = lax.dynamic_update_slice(jnp.pad(x[bi], ((N_META, lp - l_tok), (0, 0))), meta_tokens.astype(x.dtype),
                                     (0, 0))
        outs.append(_layer(h, k_sel, l_tok, ffn1_norm[0], ffn1_w_gate[0], ffn1_w_up[0], ffn1_w_down[0],
                           mix_norm[0], w_in[0], gla_w_a2[0], gla_b_a[0], gla_head_norm[0], w_out[0], ffn2_norm[0],
                           ffn2_w_gate[0], ffn2_w_up[0], ffn2_w_down[0], final_norm, cos2, sin2s))
    return jnp.stack(outs, axis=0)
```

```python
import functools

import numpy as np
import jax
import jax.numpy as jnp
from jax import lax
from jax.experimental import pallas as pl
from jax.experimental.pallas import tpu as pltpu

N_META = 16
EPS = 1e-6
ROPE_THETA = 10000.0
GLA_HEADS = 4
GLA_DK = 64
GLA_DV = 128
GLA_LOWRANK = 16
GLA_GATE_NORM = 16.0
DSA_HEADS = 8
DSA_HD = 64
IDX_HEADS = 8
IDX_HD = 64
TOPK_MAX = 256

GLA_QK = GLA_HEADS * GLA_DK
GLA_WIDTH = GLA_HEADS * GLA_DV
DSA_WIDTH = DSA_HEADS * DSA_HD
IDX_WIDTH = IDX_HEADS * IDX_HD

LANES = 128
SUBLANES = 8
ROW_TILE = 384
QBLK = 128
KSUB = 128
KTILE = 512
ATILE = KTILE // 2
N_GROUPS = 2 * KSUB
TOP_R = 6
INTERPOLATED_PROBES = 8
UNCHECKED_HALVINGS = 4
GLA_STEP = 128
GLA_SUB = 16
ROW_PAD = 1536
ACC_ROWS = DSA_HD + 16
VMEM_LIMIT = 60 * 1024 * 1024
NEG_BIG = -1e30

MISC_IK = 0
MISC_IW = IDX_HD
MISC_GA = IDX_HD + IDX_HEADS

F32 = jnp.float32
BF16 = jnp.bfloat16


def _dot(a, b):
    return jnp.dot(a, b, preferred_element_type=F32)


def _rmsnorm(x, g):
    return x * lax.rsqrt(jnp.mean(x * x, axis=-1, keepdims=True) + EPS) * g


def _const_spec(shape):
    return pl.BlockSpec(shape, lambda *_: (0,) * len(shape), pipeline_mode=pl.Buffered(1))


def _swiglu_half(x, norm_ref, wg_ref, wu_ref, wd_ref, ff_chunk):
    hn = _rmsnorm(x, norm_ref[...]).astype(BF16)
    d_ff = wg_ref.shape[1]
    acc = jnp.zeros(x.shape, F32)
    for c in range(d_ff // ff_chunk):
        sl = slice(c * ff_chunk, (c + 1) * ff_chunk)
        g = _dot(hn, wg_ref[:, sl])
        u = _dot(hn, wu_ref[:, sl])
        a = (g * jax.nn.sigmoid(g) * u).astype(BF16)
        acc = acc + _dot(a, wd_ref[sl, :])
    return x + 0.5 * acc


def _ffn1_kernel(h_ref, norm_ref, wg_ref, wu_ref, wd_ref, o_ref, *, ff_chunk):
    o_ref[...] = _swiglu_half(h_ref[...], norm_ref, wg_ref, wu_ref, wd_ref, ff_chunk)


def _ffn_chunk(d_ff):
    if d_ff <= 4096:
        return d_ff
    for c in (2048, 1024, 512, 256):
        if d_ff % c == 0:
            return c
    return d_ff


def _ffn1(h, norm, wg, wu, wd):
    lp, d = h.shape
    d_ff = wg.shape[1]
    row = pl.BlockSpec((ROW_TILE, d), lambda i: (i, 0))
    return pl.pallas_call(
        functools.partial(_ffn1_kernel, ff_chunk=_ffn_chunk(d_ff)),
        grid=(lp // ROW_TILE,),
        in_specs=[row, _const_spec((1, d)), _const_spec((d, d_ff)), _const_spec((d, d_ff)),
                  _const_spec((d_ff, d))],
        out_specs=row,
        out_shape=jax.ShapeDtypeStruct((lp, d), F32),
        compiler_params=pltpu.CompilerParams(dimension_semantics=("parallel",),
                                             vmem_limit_bytes=VMEM_LIMIT),
        name="ffn1",
    )(h, norm, wg, wu, wd)


def _rope(x, cos, sin_signed, first_half):
    parts = []
    for c in range(x.shape[1] // LANES):
        xc = x[:, c * LANES:(c + 1) * LANES]
        rot = jnp.where(first_half, pltpu.roll(xc, LANES - 32, 1), pltpu.roll(xc, 32, 1))
        parts.append(xc * cos + rot * sin_signed)
    return parts[0] if len(parts) == 1 else jnp.concatenate(parts, axis=1)


def _log_sigmoid(x):
    return jnp.minimum(x, 0.0) - jnp.log(1.0 + jnp.exp(-jnp.abs(x)))


def _in_proj_kernel(h_ref, norm_ref, w_ref, wa2_ref, ba_ref, cos_ref, sin_ref,
                    gqk_ref, gv_ref, gr_ref, gg_ref, dqt_ref, dk_ref, dvt_ref, iqt_ref, ik_ref, iwt_ref):
    hn = _rmsnorm(h_ref[...], norm_ref[...]).astype(BF16)
    cos = cos_ref[...]
    sin_s = sin_ref[...]
    lane = lax.broadcasted_iota(jnp.int32, cos.shape, 1)
    first_half = (lane % DSA_HD) < (DSA_HD // 2)

    def proj(c0, width):
        return _dot(hn, w_ref[:, c0:c0 + width])

    c = 0
    gqk_ref[...] = proj(c, 2 * GLA_QK); c += 2 * GLA_QK
    gv_ref[...] = proj(c, GLA_WIDTH); c += GLA_WIDTH
    gr_ref[...] = proj(c, GLA_WIDTH); c += GLA_WIDTH
    dqt_ref[...] = (_rope(proj(c, DSA_WIDTH), cos, sin_s, first_half) * (DSA_HD ** -0.5)).T.astype(BF16)
    c += DSA_WIDTH
    dk_ref[...] = _rope(proj(c, DSA_WIDTH), cos, sin_s, first_half).astype(BF16); c += DSA_WIDTH
    dvt_ref[...] = proj(c, DSA_WIDTH).T.astype(BF16); c += DSA_WIDTH
    iqt_ref[...] = (_rope(proj(c, IDX_WIDTH), cos, sin_s, first_half) * (IDX_HD ** -0.5)).T.astype(BF16)
    c += IDX_WIDTH
    misc = proj(c, LANES)
    ik_ref[...] = _rope(misc, cos, sin_s, first_half)[:, MISC_IK:MISC_IK + IDX_HD].astype(BF16)
    iwt_ref[...] = (misc * (IDX_HEADS ** -0.5)).T
    pre = _dot(misc.astype(BF16), wa2_ref[...]) + ba_ref[...]
    gg_ref[...] = _log_sigmoid(pre) * (1.0 / GLA_GATE_NORM)


def _in_proj(h, norm, w_all, wa2p, ba, cos2, sin2s):
    lp, d = h.shape
    ncols = w_all.shape[1]

    def row(width):
        return pl.BlockSpec((ROW_TILE, width), lambda i: (i, 0))

    def col(height):
        return pl.BlockSpec((height, ROW_TILE), lambda i: (0, i))

    outs = [
        (row(2 * GLA_QK), (lp, 2 * GLA_QK), F32), (row(GLA_WIDTH), (lp, GLA_WIDTH), F32),
        (row(GLA_WIDTH), (lp, GLA_WIDTH), F32), (row(GLA_QK), (lp, GLA_QK), F32),
        (col(DSA_WIDTH), (DSA_WIDTH, lp), BF16), (row(DSA_WIDTH), (lp, DSA_WIDTH), BF16),
        (col(DSA_WIDTH), (DSA_WIDTH, lp), BF16), (col(IDX_WIDTH), (IDX_WIDTH, lp), BF16),
        (row(IDX_HD), (lp, IDX_HD), BF16), (col(LANES), (LANES, lp), F32),
    ]
    return pl.pallas_call(
        _in_proj_kernel,
        grid=(lp // ROW_TILE,),
        in_specs=[row(d), _const_spec((1, d)), _const_spec((d, ncols)), _const_spec((LANES, GLA_QK)),
                  _const_spec((1, GLA_QK)), row(LANES), row(LANES)],
        out_specs=[spec for spec, _, _ in outs],
        out_shape=[jax.ShapeDtypeStruct(shape, dt) for _, shape, dt in outs],
        compiler_params=pltpu.CompilerParams(dimension_semantics=("parallel",),
                                             vmem_limit_bytes=VMEM_LIMIT),
        name="in_proj",
    )(h, norm, w_all, wa2p, ba, cos2, sin2s)


def _gla_kernel(qk_ref, v_ref, r_ref, g_ref, hnorm_ref, esel_ref, o_ref, s_ref):
    n_sub = GLA_STEP // GLA_SUB

    @pl.when(pl.program_id(0) == 0)
    def _():
        s_ref[...] = jnp.zeros(s_ref.shape, F32)

    q = qk_ref[:, :GLA_QK] * (GLA_DK ** -0.5)
    k = qk_ref[:, GLA_QK:]
    v = v_ref[...]
    g = g_ref[...]

    ri = lax.broadcasted_iota(jnp.int32, (GLA_STEP, GLA_STEP), 0)
    ci = lax.broadcasted_iota(jnp.int32, (GLA_STEP, GLA_STEP), 1)
    tri = ((ri // GLA_SUB == ci // GLA_SUB) & (ci <= ri)).astype(F32)
    b = jnp.dot(tri, g, preferred_element_type=F32, precision=lax.Precision.HIGHEST)

    b3 = b.reshape(n_sub, GLA_SUB, GLA_QK)
    k3 = k.reshape(n_sub, GLA_SUB, GLA_QK)
    b_last = jnp.broadcast_to(b3[:, GLA_SUB - 1:GLA_SUB, :], b3.shape).reshape(GLA_STEP, GLA_QK)

    pos = lax.broadcasted_iota(jnp.int32, (GLA_STEP, GLA_QK), 0) % GLA_SUB
    t_cols = []
    for j in range(GLA_SUB):
        kj = jnp.broadcast_to(k3[:, j:j + 1, :], k3.shape).reshape(GLA_STEP, GLA_QK)
        bj = jnp.broadcast_to(b3[:, j:j + 1, :], b3.shape).reshape(GLA_STEP, GLA_QK)
        e = jnp.where(pos >= j, b - bj, -jnp.inf)
        t_cols.append((q * kj * jnp.exp(e)).astype(BF16))
    a = _dot(jnp.concatenate(t_cols, axis=1), esel_ref[...]).astype(BF16)
    head_of_lane = lax.broadcasted_iota(jnp.int32, (GLA_SUB, GLA_WIDTH), 1) // GLA_DV
    o_intra = []
    for c in range(n_sub):
        rows = slice(c * GLA_SUB, (c + 1) * GLA_SUB)
        v_c = v[rows, :]
        v_heads = jnp.concatenate([jnp.where(head_of_lane == h, v_c, 0.0) for h in range(GLA_HEADS)]
                                  + [jnp.zeros((LANES - GLA_HEADS * GLA_SUB, GLA_WIDTH), F32)], axis=0)
        o_intra.append(_dot(a[rows, :], v_heads.astype(BF16)))
    o = jnp.concatenate(o_intra, axis=0)

    qe = q * jnp.exp(b)
    kd = k * jnp.exp(b_last - b)
    kd_t = kd.T
    dec_t = jnp.exp(b).T
    lane_head = lax.broadcasted_iota(jnp.int32, (GLA_SUB, GLA_QK), 1) // GLA_DK
    s = s_ref[...]
    o_inter = []
    for c in range(n_sub):
        rows = slice(c * GLA_SUB, (c + 1) * GLA_SUB)
        qe_c = qe[rows, :]
        q_stack = jnp.concatenate(
            [jnp.where(lane_head == h, qe_c, 0.0) for h in range(GLA_HEADS)], axis=0).astype(BF16)
        r_stack = _dot(q_stack, s.astype(BF16))
        o_inter.append(jnp.concatenate(
            [r_stack[h * GLA_SUB:(h + 1) * GLA_SUB, :] for h in range(GLA_HEADS)], axis=1))
        kd_c = kd_t[:, rows].astype(BF16)
        v_c = v[rows, :].astype(BF16)
        u_c = jnp.concatenate(
            [_dot(kd_c[h * GLA_DK:(h + 1) * GLA_DK, :], v_c[:, h * GLA_DV:(h + 1) * GLA_DV])
             for h in range(GLA_HEADS)], axis=0)
        last = c * GLA_SUB + GLA_SUB - 1
        s = dec_t[:, last:last + 1] * s + u_c
    s_ref[...] = s
    o = o + jnp.concatenate(o_inter, axis=0)

    r = r_ref[...]
    gate = r * jax.nn.sigmoid(r)
    hn = hnorm_ref[...]
    outs = []
    for h in range(GLA_HEADS):
        oh = o[:, h * GLA_DV:(h + 1) * GLA_DV]
        outs.append(_rmsnorm(oh, hn))
    o_ref[...] = (jnp.concatenate(outs, axis=1) * gate).astype(o_ref.dtype)


def _gla_head_selector():
    r = np.arange(GLA_SUB * GLA_QK)
    j, h = r // GLA_QK, (r % GLA_QK) // GLA_DK
    sel = np.zeros((GLA_SUB * GLA_QK, LANES), np.float32)
    sel[r, h * GLA_SUB + j] = 1.0
    return jnp.asarray(sel, BF16)


def _gla(gqk, gv, gr, gg, hnorm):
    lp = gqk.shape[0]
    assert GLA_HEADS * GLA_SUB <= LANES

    def row(width):
        return pl.BlockSpec((GLA_STEP, width), lambda i: (i, 0))

    return pl.pallas_call(
        _gla_kernel,
        grid=(lp // GLA_STEP,),
        in_specs=[row(2 * GLA_QK), row(GLA_WIDTH), row(GLA_WIDTH), row(GLA_QK), _const_spec((1, GLA_DV)),
                  _const_spec((GLA_SUB * GLA_QK, LANES))],
        out_specs=row(GLA_WIDTH),
        out_shape=jax.ShapeDtypeStruct((lp, GLA_WIDTH), BF16),
        scratch_shapes=[pltpu.VMEM((GLA_QK, GLA_DV), F32)],
        compiler_params=pltpu.CompilerParams(dimension_semantics=("arbitrary",),
                                             vmem_limit_bytes=VMEM_LIMIT),
        name="gla",
    )(gqk, gv, gr, gg, hnorm, _gla_head_selector())


def _key_to_float(u):
    bits = jnp.where(u >= 0, u, u ^ jnp.int32(0x7FFFFFFF))
    return lax.bitcast_convert_type(bits, F32)


def _float_to_key(f):
    bits = lax.bitcast_convert_type(f, jnp.int32)
    return jnp.where(bits >= 0, bits, bits ^ jnp.int32(0x7FFFFFFF))


_KEY_NEG_INF = np.int32(np.uint32(0xFF800000) ^ np.uint32(0x7FFFFFFF))


def _tile_loop(n, body, carry, unroll):
    trips = n // unroll

    def group(u, c):
        for r in range(unroll):
            c = body(u * unroll + r, c)
        return c

    carry = lax.fori_loop(0, trips, group, carry)
    return lax.fori_loop(trips * unroll, n, body, carry)


def _fold_rows(x, op):
    parts = [x[r:r + SUBLANES, :] for r in range(0, x.shape[0], SUBLANES)]
    while len(parts) > 1:
        parts = [op(parts[a], parts[a + 1]) if a + 1 < len(parts) else parts[a] for a in range(0, len(parts), 2)]
    return parts[0]


def _dsa_block(qt_ref, iqt_ref, iwt_ref, k_ref, vt_ref, ik_ref, o_ref, sc_ref, bias_ref, acc_ref, rhs_ref,
               sa_ref, sb_ref, top_ref, k_sel):
    i = pl.program_id(0)
    n_tiles = (i * QBLK + QBLK + KTILE - 1) // KTILE
    last = n_tiles - 1
    q_pos = i * QBLK + lax.broadcasted_iota(jnp.int32, (KSUB, QBLK), 1)
    key_row = lax.broadcasted_iota(jnp.int32, (KSUB, QBLK), 0)

    def visible(ks):
        return (ks + key_row) <= q_pos

    def peeled(body, carry):
        carry = _tile_loop(last, functools.partial(body, masked=False), carry, 4)
        return body(last, carry, masked=True)

    iqt = iqt_ref[...]
    iq_all = jnp.concatenate([iqt[h * IDX_HD:(h + 1) * IDX_HD, :] for h in range(IDX_HEADS)], axis=1)
    iw = iwt_ref[MISC_IW:MISC_IW + IDX_HEADS, :]
    top_ref[...] = jnp.full(top_ref.shape, -jnp.inf, F32)

    def score_tile(t, carry, masked):
        for sub in range(KTILE // KSUB):
            ks = pl.multiple_of(t * KTILE + sub * KSUB, KSUB)
            lg = _dot(ik_ref[pl.ds(ks, KSUB), :], iq_all)
            sc = iw[0:1, :] * jnp.maximum(lg[:, 0:QBLK], 0.0)
            for h in range(1, IDX_HEADS):
                sc = sc + iw[h:h + 1, :] * jnp.maximum(lg[:, h * QBLK:(h + 1) * QBLK], 0.0)
            if masked:
                sc = jnp.where(visible(ks), sc, -jnp.inf)
            sc_ref[pl.ds(ks, KSUB), :] = sc
            rest = sc
            for r in range(TOP_R):
                rows = slice(r * N_GROUPS + (sub % 2) * KSUB, r * N_GROUPS + (sub % 2 + 1) * KSUB)
                held = top_ref[rows, :]
                top_ref[rows, :] = jnp.maximum(held, rest)
                if r + 1 < TOP_R:
                    rest = jnp.minimum(held, rest)
        return carry

    peeled(score_tile, 0)

    kk = float(k_sel)

    def lane_sum(x):
        return jnp.sum(_fold_rows(x, jnp.add), axis=0, keepdims=True)

    def count_scores(pred):
        def body(t, acc):
            ks = pl.multiple_of(t * KTILE, KTILE)
            return acc + _fold_rows(jnp.where(pred(sc_ref[pl.ds(ks, KTILE), :]), 1.0, 0.0), jnp.add)
        return jnp.sum(_tile_loop(n_tiles, body, jnp.zeros((SUBLANES, QBLK), F32), 4), axis=0, keepdims=True)

    def count_ge(cand):
        return lane_sum(jnp.where(top_ref[...] >= cand, 1.0, 0.0))

    best = top_ref[0:N_GROUPS, :]
    lo = jnp.maximum(_float_to_key(jnp.min(_fold_rows(best, jnp.minimum), axis=0, keepdims=True)), _KEY_NEG_INF)
    hi = _float_to_key(jnp.max(_fold_rows(best, jnp.maximum), axis=0, keepdims=True))
    n_lo = count_ge(_key_to_float(lo))

    def settled(lo, hi, n_lo):
        return jnp.logical_or(n_lo <= kk, lo >= hi)

    def any_open(lo, hi, n_lo):
        return jnp.max(jnp.where(settled(lo, hi, n_lo), 0.0, 1.0))

    def midpoint(lo, hi):
        return (lo | hi) - ((lo ^ hi) >> 1)

    def probe_at(key, lo, hi, n_lo):
        open_ = jnp.logical_not(settled(lo, hi, n_lo))
        cnt = count_ge(_key_to_float(key))
        up = jnp.logical_and(open_, cnt >= kk)
        down = jnp.logical_and(open_, cnt < kk)
        return jnp.where(up, key, lo), jnp.where(down, key - 1, hi), jnp.where(up, cnt, n_lo), cnt, up, down

    def halve(lo, hi, n_lo):
        return probe_at(midpoint(lo, hi), lo, hi, n_lo)[:3]

    log_k = float(np.log(k_sel))

    def interpolate(st):
        lo, hi, n_lo, g_lo, g_hi, side = st
        f_lo, f_hi = _key_to_float(lo), _key_to_float(hi)
        guess = f_lo + (g_lo - log_k) / jnp.maximum(g_lo - g_hi, 1e-9) * (f_hi - f_lo)
        usable = jnp.logical_and(lo > _KEY_NEG_INF, jnp.abs(guess) < jnp.inf)
        key = jnp.where(usable, jnp.clip(_float_to_key(guess), lo + 1, hi), midpoint(lo, hi))
        lo, hi, n_lo, cnt, up, down = probe_at(key, lo, hi, n_lo)
        g_cnt = jnp.log(jnp.maximum(cnt, 0.5))
        g_hi = jnp.where(down, g_cnt, jnp.where(jnp.logical_and(up, side > 0.0), 0.5 * (g_hi + log_k), g_hi))
        g_lo = jnp.where(up, g_cnt, jnp.where(jnp.logical_and(down, side < 0.0), 0.5 * (g_lo + log_k), g_lo))
        side = jnp.where(up, 1.0, jnp.where(down, -1.0, side))
        return lo, hi, n_lo, g_lo, g_hi, side

    def checked_halve(st):
        lo, hi, n_lo = halve(*halve(*st[:3]))
        return lo, hi, n_lo, any_open(lo, hi, n_lo), st[4] + 1

    lo, hi, n_lo, _, _, _ = lax.fori_loop(
        0, INTERPOLATED_PROBES, lambda _, st: interpolate(st),
        (lo, hi, n_lo, jnp.log(jnp.maximum(n_lo, 0.5)), jnp.full((1, QBLK), float(np.log(0.5)), F32),
         jnp.zeros((1, QBLK), F32)))
    lo, hi, n_lo = lax.fori_loop(0, UNCHECKED_HALVINGS, lambda _, st: halve(*st), (lo, hi, n_lo))
    lo, hi, n_lo, _, _ = lax.while_loop(lambda st: jnp.logical_and(st[3] > 0.0, st[4] < 17), checked_halve,
                                        (lo, hi, n_lo, any_open(lo, hi, n_lo), jnp.int32(0)))
    tops = top_ref[...]
    thr = jnp.min(_fold_rows(jnp.where(tops >= _key_to_float(lo), tops, jnp.inf), jnp.minimum), axis=0,
                  keepdims=True)
    n_ge = count_scores(lambda x: x >= thr)

    def next_above(v):
        def body(t, carry):
            low, cnt = carry
            ks = pl.multiple_of(t * KTILE, KTILE)
            x = sc_ref[pl.ds(ks, KTILE), :]
            above = x > v
            return (jnp.minimum(low, _fold_rows(jnp.where(above, x, jnp.inf), jnp.minimum)),
                    cnt + _fold_rows(jnp.where(above, 1.0, 0.0), jnp.add))
        low, cnt = _tile_loop(n_tiles, body, (jnp.full((SUBLANES, QBLK), jnp.inf, F32),
                                              jnp.zeros((SUBLANES, QBLK), F32)), 4)
        return jnp.min(low, axis=0, keepdims=True), jnp.sum(cnt, axis=0, keepdims=True)

    def any_climbing(n_ge, stuck):
        return jnp.max(jnp.where(jnp.logical_or(n_ge <= kk, stuck > 0.0), 0.0, 1.0))

    def climb(st):
        thr, n_ge, stuck, _, it = st
        nxt, n_nxt = next_above(thr)
        go = jnp.logical_and(jnp.logical_and(n_ge > kk, stuck == 0.0), n_nxt >= kk)
        stuck = jnp.where(jnp.logical_and(n_ge > kk, n_nxt < kk), 1.0, stuck)
        thr, n_ge = jnp.where(go, nxt, thr), jnp.where(go, n_nxt, n_ge)
        return thr, n_ge, stuck, any_climbing(n_ge, stuck), it + 1

    stuck = jnp.zeros((1, QBLK), F32)
    thr, n_ge, _, _, _ = lax.while_loop(
        lambda st: jnp.logical_and(st[3] > 0.0, st[4] <= n_tiles * KTILE), climb,
        (thr, n_ge, stuck, any_climbing(n_ge, stuck), jnp.int32(0)))
    has_ties = jnp.max(n_ge) > kk

    def bias_fast(t, carry, masked):
        for sub in range(KTILE // KSUB):
            ks = pl.multiple_of(t * KTILE + sub * KSUB, KSUB)
            keep = sc_ref[pl.ds(ks, KSUB), :] >= thr
            if masked:
                keep = jnp.logical_and(keep, visible(ks))
            bias_ref[pl.ds(ks, KSUB), :] = jnp.where(keep, 0.0, NEG_BIG).astype(BF16)
        return carry

    def bias_ties(t, run, masked, budget, tri):
        for sub in range(KTILE // KSUB):
            ks = pl.multiple_of(t * KTILE + sub * KSUB, KSUB)
            x = sc_ref[pl.ds(ks, KSUB), :]
            above = x > thr
            tie = x == thr
            tie_f = jnp.where(tie, 1.0, 0.0)
            rank = run + _dot(tri, tie_f.astype(BF16))
            keep = jnp.logical_or(above, jnp.logical_and(tie, rank <= budget))
            if masked:
                keep = jnp.logical_and(keep, visible(ks))
            bias_ref[pl.ds(ks, KSUB), :] = jnp.where(keep, 0.0, NEG_BIG).astype(BF16)
            run = run + jnp.sum(tie_f, axis=0, keepdims=True)
        return run

    @pl.when(jnp.logical_not(has_ties))
    def _():
        peeled(bias_fast, 0)

    @pl.when(has_ties)
    def _():
        budget = kk - count_scores(lambda x: x > thr)
        tri = (lax.broadcasted_iota(jnp.int32, (KSUB, KSUB), 1) <= key_row).astype(BF16)
        peeled(functools.partial(bias_ties, budget=budget, tri=tri), jnp.zeros((1, QBLK), F32))

    qt = qt_ref[...]
    eye = (lax.broadcasted_iota(jnp.int32, (QBLK, QBLK), 0)
           == lax.broadcasted_iota(jnp.int32, (QBLK, QBLK), 1)).astype(BF16)
    zero = jnp.zeros((DSA_HD, QBLK), BF16)
    n_pairs = DSA_HEADS // 2
    for c in range(n_pairs):
        qa = qt[(2 * c) * DSA_HD:(2 * c + 1) * DSA_HD, :]
        qb = qt[(2 * c + 1) * DSA_HD:(2 * c + 2) * DSA_HD, :]
        rhs_ref[c] = jnp.concatenate([jnp.concatenate([qa, zero], axis=1),
                                      jnp.concatenate([zero, qb], axis=1),
                                      jnp.concatenate([eye, eye], axis=1)], axis=0)
    acc_ref[...] = jnp.zeros(acc_ref.shape, F32)
    ones_rows = jnp.ones((ACC_ROWS - DSA_HD, ATILE), BF16)

    def qk_stage(ks, s_ref):
        bias = bias_ref[pl.ds(ks, ATILE), :]
        tile_max = []
        for c in range(n_pairs):
            lhs = jnp.concatenate([k_ref[pl.ds(ks, ATILE), c * 2 * DSA_HD:(c + 1) * 2 * DSA_HD], bias], axis=1)
            s = _dot(lhs, rhs_ref[c])
            s_ref[c] = s
            tile_max.append(_fold_rows(s, jnp.maximum))
        return tuple(tile_max)

    def pv_stage(ks, s_ref, tile_max, ms):
        new_ms = []
        for c in range(n_pairs):
            m_new = jnp.maximum(ms[c], jnp.max(tile_max[c], axis=0, keepdims=True))
            alpha = jnp.exp(ms[c] - m_new)
            p = jnp.exp(s_ref[c] - m_new).astype(BF16)
            for hh in range(2):
                h = 2 * c + hh
                rows = slice(h * ACC_ROWS, (h + 1) * ACC_ROWS)
                vt_aug = jnp.concatenate([vt_ref[h * DSA_HD:(h + 1) * DSA_HD, pl.ds(ks, ATILE)], ones_rows], axis=0)
                pv = _dot(vt_aug, p[:, hh * QBLK:(hh + 1) * QBLK])
                acc_ref[rows, :] = alpha[:, hh * QBLK:(hh + 1) * QBLK] * acc_ref[rows, :] + pv
            new_ms.append(m_new)
        return tuple(new_ms)

    def attn_tile(t, carry):
        max_a, ms = carry
        k0 = pl.multiple_of(t * KTILE, KTILE)
        k1 = pl.multiple_of(t * KTILE + ATILE, ATILE)
        k2 = pl.multiple_of(t * KTILE + KTILE, KTILE)
        max_b = qk_stage(k1, sb_ref)
        ms = pv_stage(k0, sa_ref, max_a, ms)
        max_a = qk_stage(k2, sa_ref)
        ms = pv_stage(k1, sb_ref, max_b, ms)
        return max_a, ms

    ms = tuple(jnp.full((1, 2 * QBLK), NEG_BIG, F32) for _ in range(n_pairs))
    max_a, ms = _tile_loop(last, attn_tile, (qk_stage(0, sa_ref), ms), 4)
    k0 = pl.multiple_of(last * KTILE, KTILE)
    k1 = pl.multiple_of(last * KTILE + ATILE, ATILE)
    max_b = qk_stage(k1, sb_ref)
    ms = pv_stage(k0, sa_ref, max_a, ms)
    pv_stage(k1, sb_ref, max_b, ms)
    outs = []
    for h in range(DSA_HEADS):
        a = acc_ref[h * ACC_ROWS:(h + 1) * ACC_ROWS, :]
        outs.append(a[:DSA_HD, :] / a[DSA_HD:DSA_HD + 1, :])
    o_ref[...] = jnp.concatenate(outs, axis=0).T.astype(o_ref.dtype)


def _dsa_kernel(qt_ref, iqt_ref, iwt_ref, k_ref, vt_ref, ik_ref, o_ref, sc_ref, bias_ref, acc_ref, rhs_ref,
                sa_ref, sb_ref, top_ref, *, k_sel, l_tok):
    has_tokens = pl.program_id(0) * QBLK < l_tok

    @pl.when(has_tokens)
    def _():
        _dsa_block(qt_ref, iqt_ref, iwt_ref, k_ref, vt_ref, ik_ref, o_ref, sc_ref, bias_ref, acc_ref, rhs_ref,
                   sa_ref, sb_ref, top_ref, k_sel)

    @pl.when(jnp.logical_not(has_tokens))
    def _():
        o_ref[...] = jnp.zeros(o_ref.shape, o_ref.dtype)


def _dsa(dqt, dk, dvt, iqt, ik, iwt, k_sel, l_tok):
    lp = dk.shape[0]

    def col(height):
        return pl.BlockSpec((height, QBLK), lambda i: (0, i))

    return pl.pallas_call(
        functools.partial(_dsa_kernel, k_sel=k_sel, l_tok=l_tok),
        grid=(lp // QBLK,),
        in_specs=[col(DSA_WIDTH), col(IDX_WIDTH), col(LANES), _const_spec((lp, DSA_WIDTH)),
                  _const_spec((DSA_WIDTH, lp)), _const_spec((lp, IDX_HD))],
        out_specs=pl.BlockSpec((QBLK, DSA_WIDTH), lambda i: (i, 0)),
        out_shape=jax.ShapeDtypeStruct((lp, DSA_WIDTH), BF16),
        scratch_shapes=[pltpu.VMEM((lp, QBLK), F32), pltpu.VMEM((lp, QBLK), BF16),
                        pltpu.VMEM((DSA_HEADS * ACC_ROWS, QBLK), F32),
                        pltpu.VMEM((DSA_HEADS // 2, 2 * LANES, 2 * QBLK), BF16),
                        pltpu.VMEM((DSA_HEADS // 2, ATILE, 2 * QBLK), F32),
                        pltpu.VMEM((DSA_HEADS // 2, ATILE, 2 * QBLK), F32),
                        pltpu.VMEM((TOP_R * N_GROUPS, QBLK), F32)],
        compiler_params=pltpu.CompilerParams(dimension_semantics=("parallel",),
                                             vmem_limit_bytes=VMEM_LIMIT),
        name="dsa",
    )(dqt, iqt, iwt, dk, dvt, ik)


def _out_ffn2_kernel(h_ref, ht_ref, og_ref, ogt_ref, od_ref, odt_ref, wo_ref, norm_ref, wg_ref, wu_ref, wd_ref,
                     fnorm_ref, o_ref, *, ff_chunk):
    def shifted(main_ref, tail_ref):
        return jnp.concatenate([main_ref[N_META:, :], tail_ref[...]], axis=0)

    h2 = (shifted(h_ref, ht_ref) + _dot(shifted(og_ref, ogt_ref), wo_ref[:GLA_WIDTH, :])
          + _dot(shifted(od_ref, odt_ref), wo_ref[GLA_WIDTH:, :]))
    h3 = _swiglu_half(h2, norm_ref, wg_ref, wu_ref, wd_ref, ff_chunk)
    o_ref[...] = _rmsnorm(h3, fnorm_ref[...])


def _out_ffn2(h, o_gla, o_dsa, wo, norm, wg, wu, wd, fnorm, seq):
    lp, d = h.shape
    d_ff = wg.shape[1]
    n_steps = -(-seq // ROW_TILE)
    assert ROW_TILE % N_META == 0 and n_steps * ROW_TILE + N_META <= lp
    tail_step = ROW_TILE // N_META

    def row(width):
        return pl.BlockSpec((ROW_TILE, width), lambda i: (i, 0))

    def tail(width):
        return pl.BlockSpec((N_META, width), lambda i: ((i + 1) * tail_step, 0))

    return pl.pallas_call(
        functools.partial(_out_ffn2_kernel, ff_chunk=_ffn_chunk(d_ff)),
        grid=(n_steps,),
        in_specs=[row(d), tail(d), row(GLA_WIDTH), tail(GLA_WIDTH), row(DSA_WIDTH), tail(DSA_WIDTH),
                  _const_spec((GLA_WIDTH + DSA_WIDTH, d)), _const_spec((1, d)), _const_spec((d, d_ff)),
                  _const_spec((d, d_ff)), _const_spec((d_ff, d)), _const_spec((1, d))],
        out_specs=row(d),
        out_shape=jax.ShapeDtypeStruct((seq, d), F32),
        compiler_params=pltpu.CompilerParams(dimension_semantics=("parallel",),
                                             vmem_limit_bytes=VMEM_LIMIT),
        name="out_ffn2",
    )(h, h, o_gla, o_gla, o_dsa, o_dsa, wo, norm, wg, wu, wd, fnorm)


def _rope_tables(lp):
    pos = jnp.arange(lp, dtype=F32)
    inv_freq = 1.0 / (ROPE_THETA ** (jnp.arange(0, DSA_HD, 2, dtype=F32) / DSA_HD))
    ang = pos[:, None] * inv_freq[None, :]
    cos = jnp.cos(ang)
    sin = jnp.sin(ang)
    cos2 = jnp.concatenate([cos, cos, cos, cos], axis=-1)
    sin2s = jnp.concatenate([-sin, sin, -sin, sin], axis=-1)
    return cos2, sin2s


def _layer(h, k_sel, l_tok, ffn1_norm, ffn1_w_gate, ffn1_w_up, ffn1_w_down, mix_norm, w_in, gla_w_a2, gla_b_a,
           gla_head_norm, w_out, ffn2_norm, ffn2_w_gate, ffn2_w_up, ffn2_w_down, final_norm, cos2, sin2s):
    d = h.shape[1]
    h1 = _ffn1(h, ffn1_norm.reshape(1, d), ffn1_w_gate.astype(BF16), ffn1_w_up.astype(BF16),
               ffn1_w_down.astype(BF16))

    sizes = (GLA_QK, GLA_QK, GLA_WIDTH, GLA_WIDTH, GLA_LOWRANK, DSA_WIDTH, DSA_WIDTH, DSA_WIDTH, IDX_WIDTH,
             IDX_HD, IDX_HEADS)
    offs = np.concatenate([[0], np.cumsum(sizes)])
    col = lambda n: w_in[:, offs[n]:offs[n + 1]]
    pad = jnp.zeros((d, LANES - IDX_HD - IDX_HEADS - GLA_LOWRANK), w_in.dtype)
    w_all = jnp.concatenate([col(0), col(1), col(2), col(3), col(5), col(6), col(7), col(8),
                             col(9), col(10), col(4), pad], axis=1).astype(BF16)
    wa2p = jnp.zeros((LANES, GLA_QK), F32).at[MISC_GA:MISC_GA + GLA_LOWRANK, :].set(gla_w_a2).astype(BF16)

    gqk, gv, gr, gg, dqt, dk, dvt, iqt, ik, iwt = _in_proj(
        h1, mix_norm.reshape(1, d), w_all, wa2p, gla_b_a.reshape(1, GLA_QK), cos2, sin2s)
    o_gla = _gla(gqk, gv, gr, gg, gla_head_norm.reshape(1, GLA_DV))
    o_dsa = _dsa(dqt, dk, dvt, iqt, ik, iwt, k_sel, l_tok)
    return _out_ffn2(h1, o_gla, o_dsa, w_out.astype(BF16), ffn2_norm.reshape(1, d), ffn2_w_gate.astype(BF16),
                     ffn2_w_up.astype(BF16), ffn2_w_down.astype(BF16), final_norm.reshape(1, d), l_tok - N_META)


def kernel(x, meta_tokens, ffn1_norm, ffn1_w_gate, ffn1_w_up, ffn1_w_down, mix_norm, w_in, gla_w_a2, gla_b_a,
           gla_head_norm, w_out, ffn2_norm, ffn2_w_gate, ffn2_w_up, ffn2_w_down, final_norm):
    batch, seq, d = x.shape
    depth = ffn1_norm.shape[0]
    assert depth == 1, "the final norm is fused into the layer's last kernel"
    k_sel = min(TOPK_MAX, seq // 4)
    l_tok = seq + N_META
    lp = -(-max(l_tok, -(-seq // ROW_TILE) * ROW_TILE + N_META) // ROW_PAD) * ROW_PAD
    cos2, sin2s = _rope_tables(lp)
    outs = []
    for bi in range(batch):
        h = lax.dynamic_update_slice(jnp.pad(x[bi], ((N_META, lp - l_tok), (0, 0))), meta_tokens.astype(x.dtype),
                                     (0, 0))
        outs.append(_layer(h, k_sel, l_tok, ffn1_norm[0], ffn1_w_gate[0], ffn1_w_up[0], ffn1_w_down[0],
                           mix_norm[0], w_in[0], gla_w_a2[0], gla_b_a[0], gla_head_norm[0], w_out[0], ffn2_norm[0],
                           ffn2_w_gate[0], ffn2_w_up[0], ffn2_w_down[0], final_norm, cos2, sin2s))
    return jnp.stack(outs, axis=0)
```

```python
import functools

import numpy as np
import jax
import jax.numpy as jnp
from jax import lax
from jax.experimental import pallas as pl
from jax.experimental.pallas import tpu as pltpu

N_META = 16
EPS = 1e-6
ROPE_THETA = 10000.0
GLA_HEADS = 4
GLA_DK = 64
GLA_DV = 128
GLA_LOWRANK = 16
GLA_GATE_NORM = 16.0
DSA_HEADS = 8
DSA_HD = 64
IDX_HEADS = 8
IDX_HD = 64
TOPK_MAX = 256

GLA_QK = GLA_HEADS * GLA_DK
GLA_WIDTH = GLA_HEADS * GLA_DV
DSA_WIDTH = DSA_HEADS * DSA_HD
IDX_WIDTH = IDX_HEADS * IDX_HD

LANES = 128
SUBLANES = 8
ROW_TILE = 384
QBLK = 128
KSUB = 128
KTILE = 512
ATILE = KTILE // 2
N_GROUPS = 2 * KSUB
TOP_R = 6
INTERPOLATED_PROBES = 8
UNCHECKED_HALVINGS = 4
GLA_STEP = 128
GLA_SUB = 16
ROW_PAD = 1536
ACC_ROWS = DSA_HD + 16
VMEM_LIMIT = 60 * 1024 * 1024
NEG_BIG = -1e30

MISC_IK = 0
MISC_IW = IDX_HD
MISC_GA = IDX_HD + IDX_HEADS

F32 = jnp.float32
BF16 = jnp.bfloat16


def _dot(a, b):
    return jnp.dot(a, b, preferred_element_type=F32)


def _rmsnorm(x, g):
    return x * lax.rsqrt(jnp.mean(x * x, axis=-1, keepdims=True) + EPS) * g


def _const_spec(shape):
    return pl.BlockSpec(shape, lambda *_: (0,) * len(shape), pipeline_mode=pl.Buffered(1))


def _swiglu_half(x, norm_ref, wg_ref, wu_ref, wd_ref, ff_chunk):
    hn = _rmsnorm(x, norm_ref[...]).astype(BF16)
    d_ff = wg_ref.shape[1]
    acc = jnp.zeros(x.shape, F32)
    for c in range(d_ff // ff_chunk):
        sl = slice(c * ff_chunk, (c + 1) * ff_chunk)
        g = _dot(hn, wg_ref[:, sl])
        u = _dot(hn, wu_ref[:, sl])
        a = (g * jax.nn.sigmoid(g) * u).astype(BF16)
        acc = acc + _dot(a, wd_ref[sl, :])
    return x + 0.5 * acc


def _ffn1_kernel(h_ref, meta_ref, norm_ref, wg_ref, wu_ref, wd_ref, o_ref, *, ff_chunk):
    x = h_ref[...]
    rows = lax.broadcasted_iota(jnp.int32, (x.shape[0], 1), 0)
    meta = jnp.concatenate([meta_ref[...], jnp.zeros((x.shape[0] - N_META, x.shape[1]), x.dtype)], axis=0)
    x = jnp.where(jnp.logical_and(pl.program_id(0) == 0, rows < N_META), meta, x)
    o_ref[...] = _swiglu_half(x, norm_ref, wg_ref, wu_ref, wd_ref, ff_chunk)


def _ffn_chunk(d_ff):
    if d_ff <= 4096:
        return d_ff
    for c in (2048, 1024, 512, 256):
        if d_ff % c == 0:
            return c
    return d_ff


def _ffn1(h, meta, norm, wg, wu, wd):
    lp, d = h.shape
    d_ff = wg.shape[1]
    row = pl.BlockSpec((ROW_TILE, d), lambda i: (i, 0))
    return pl.pallas_call(
        functools.partial(_ffn1_kernel, ff_chunk=_ffn_chunk(d_ff)),
        grid=(lp // ROW_TILE,),
        in_specs=[row, _const_spec((N_META, d)), _const_spec((1, d)), _const_spec((d, d_ff)),
                  _const_spec((d, d_ff)), _const_spec((d_ff, d))],
        out_specs=row,
        out_shape=jax.ShapeDtypeStruct((lp, d), F32),
        compiler_params=pltpu.CompilerParams(dimension_semantics=("parallel",),
                                             vmem_limit_bytes=VMEM_LIMIT),
        name="ffn1",
    )(h, meta, norm, wg, wu, wd)


def _rope(x, cos, sin_signed, first_half):
    parts = []
    for c in range(x.shape[1] // LANES):
        xc = x[:, c * LANES:(c + 1) * LANES]
        rot = jnp.where(first_half, pltpu.roll(xc, LANES - 32, 1), pltpu.roll(xc, 32, 1))
        parts.append(xc * cos + rot * sin_signed)
    return parts[0] if len(parts) == 1 else jnp.concatenate(parts, axis=1)


def _log_sigmoid(x):
    return jnp.minimum(x, 0.0) - jnp.log(1.0 + jnp.exp(-jnp.abs(x)))


def _in_proj_kernel(h_ref, norm_ref, w_ref, wa2_ref, ba_ref, cos_ref, sin_ref,
                    gqk_ref, gv_ref, gr_ref, gg_ref, dqt_ref, dk_ref, dvt_ref, iqt_ref, ik_ref, iwt_ref):
    hn = _rmsnorm(h_ref[...], norm_ref[...]).astype(BF16)
    cos = cos_ref[...]
    sin_s = sin_ref[...]
    lane = lax.broadcasted_iota(jnp.int32, cos.shape, 1)
    first_half = (lane % DSA_HD) < (DSA_HD // 2)

    def proj(c0, width):
        return _dot(hn, w_ref[:, c0:c0 + width])

    c = 0
    gqk_ref[...] = proj(c, 2 * GLA_QK); c += 2 * GLA_QK
    gv_ref[...] = proj(c, GLA_WIDTH); c += GLA_WIDTH
    gr_ref[...] = proj(c, GLA_WIDTH); c += GLA_WIDTH
    dqt_ref[...] = (_rope(proj(c, DSA_WIDTH), cos, sin_s, first_half) * (DSA_HD ** -0.5)).T.astype(BF16)
    c += DSA_WIDTH
    dk_ref[...] = _rope(proj(c, DSA_WIDTH), cos, sin_s, first_half).astype(BF16); c += DSA_WIDTH
    dvt_ref[...] = proj(c, DSA_WIDTH).T.astype(BF16); c += DSA_WIDTH
    iqt_ref[...] = (_rope(proj(c, IDX_WIDTH), cos, sin_s, first_half) * (IDX_HD ** -0.5)).T.astype(BF16)
    c += IDX_WIDTH
    misc = proj(c, LANES)
    ik_ref[...] = _rope(misc, cos, sin_s, first_half)[:, MISC_IK:MISC_IK + IDX_HD].astype(BF16)
    iwt_ref[...] = (misc * (IDX_HEADS ** -0.5)).T
    pre = _dot(misc.astype(BF16), wa2_ref[...]) + ba_ref[...]
    gg_ref[...] = _log_sigmoid(pre) * (1.0 / GLA_GATE_NORM)


def _in_proj(h, norm, w_all, wa2p, ba, cos2, sin2s):
    lp, d = h.shape
    ncols = w_all.shape[1]

    def row(width):
        return pl.BlockSpec((ROW_TILE, width), lambda i: (i, 0))

    def col(height):
        return pl.BlockSpec((height, ROW_TILE), lambda i: (0, i))

    outs = [
        (row(2 * GLA_QK), (lp, 2 * GLA_QK), F32), (row(GLA_WIDTH), (lp, GLA_WIDTH), F32),
        (row(GLA_WIDTH), (lp, GLA_WIDTH), F32), (row(GLA_QK), (lp, GLA_QK), F32),
        (col(DSA_WIDTH), (DSA_WIDTH, lp), BF16), (row(DSA_WIDTH), (lp, DSA_WIDTH), BF16),
        (col(DSA_WIDTH), (DSA_WIDTH, lp), BF16), (col(IDX_WIDTH), (IDX_WIDTH, lp), BF16),
        (row(IDX_HD), (lp, IDX_HD), BF16), (col(LANES), (LANES, lp), F32),
    ]
    return pl.pallas_call(
        _in_proj_kernel,
        grid=(lp // ROW_TILE,),
        in_specs=[row(d), _const_spec((1, d)), _const_spec((d, ncols)), _const_spec((LANES, GLA_QK)),
                  _const_spec((1, GLA_QK)), row(LANES), row(LANES)],
        out_specs=[spec for spec, _, _ in outs],
        out_shape=[jax.ShapeDtypeStruct(shape, dt) for _, shape, dt in outs],
        compiler_params=pltpu.CompilerParams(dimension_semantics=("parallel",),
                                             vmem_limit_bytes=VMEM_LIMIT),
        name="in_proj",
    )(h, norm, w_all, wa2p, ba, cos2, sin2s)


def _gla_kernel(qk_ref, v_ref, r_ref, g_ref, hnorm_ref, esel_ref, o_ref, s_ref):
    n_sub = GLA_STEP // GLA_SUB

    @pl.when(pl.program_id(0) == 0)
    def _():
        s_ref[...] = jnp.zeros(s_ref.shape, F32)

    q = qk_ref[:, :GLA_QK] * (GLA_DK ** -0.5)
    k = qk_ref[:, GLA_QK:]
    v = v_ref[...]
    g = g_ref[...]

    ri = lax.broadcasted_iota(jnp.int32, (GLA_STEP, GLA_STEP), 0)
    ci = lax.broadcasted_iota(jnp.int32, (GLA_STEP, GLA_STEP), 1)
    tri = ((ri // GLA_SUB == ci // GLA_SUB) & (ci <= ri)).astype(F32)
    b = jnp.dot(tri, g, preferred_element_type=F32, precision=lax.Precision.HIGHEST)

    b3 = b.reshape(n_sub, GLA_SUB, GLA_QK)
    k3 = k.reshape(n_sub, GLA_SUB, GLA_QK)
    b_last = jnp.broadcast_to(b3[:, GLA_SUB - 1:GLA_SUB, :], b3.shape).reshape(GLA_STEP, GLA_QK)

    pos = lax.broadcasted_iota(jnp.int32, (GLA_STEP, GLA_QK), 0) % GLA_SUB
    t_cols = []
    for j in range(GLA_SUB):
        kj = jnp.broadcast_to(k3[:, j:j + 1, :], k3.shape).reshape(GLA_STEP, GLA_QK)
        bj = jnp.broadcast_to(b3[:, j:j + 1, :], b3.shape).reshape(GLA_STEP, GLA_QK)
        e = jnp.where(pos >= j, b - bj, -jnp.inf)
        t_cols.append((q * kj * jnp.exp(e)).astype(BF16))
    a = _dot(jnp.concatenate(t_cols, axis=1), esel_ref[...]).astype(BF16)
    head_of_lane = lax.broadcasted_iota(jnp.int32, (GLA_SUB, GLA_WIDTH), 1) // GLA_DV
    o_intra = []
    for c in range(n_sub):
        rows = slice(c * GLA_SUB, (c + 1) * GLA_SUB)
        v_c = v[rows, :]
        v_heads = jnp.concatenate([jnp.where(head_of_lane == h, v_c, 0.0) for h in range(GLA_HEADS)]
                                  + [jnp.zeros((LANES - GLA_HEADS * GLA_SUB, GLA_WIDTH), F32)], axis=0)
        o_intra.append(_dot(a[rows, :], v_heads.astype(BF16)))
    o = jnp.concatenate(o_intra, axis=0)

    qe = q * jnp.exp(b)
    kd = k * jnp.exp(b_last - b)
    kd_t = kd.T
    dec_t = jnp.exp(b).T
    lane_head = lax.broadcasted_iota(jnp.int32, (GLA_SUB, GLA_QK), 1) // GLA_DK
    s = s_ref[...]
    o_inter = []
    for c in range(n_sub):
        rows = slice(c * GLA_SUB, (c + 1) * GLA_SUB)
        qe_c = qe[rows, :]
        q_stack = jnp.concatenate(
            [jnp.where(lane_head == h, qe_c, 0.0) for h in range(GLA_HEADS)], axis=0).astype(BF16)
        r_stack = _dot(q_stack, s.astype(BF16))
        o_inter.append(jnp.concatenate(
            [r_stack[h * GLA_SUB:(h + 1) * GLA_SUB, :] for h in range(GLA_HEADS)], axis=1))
        kd_c = kd_t[:, rows].astype(BF16)
        v_c = v[rows, :].astype(BF16)
        u_c = jnp.concatenate(
            [_dot(kd_c[h * GLA_DK:(h + 1) * GLA_DK, :], v_c[:, h * GLA_DV:(h + 1) * GLA_DV])
             for h in range(GLA_HEADS)], axis=0)
        last = c * GLA_SUB + GLA_SUB - 1
        s = dec_t[:, last:last + 1] * s + u_c
    s_ref[...] = s
    o = o + jnp.concatenate(o_inter, axis=0)

    r = r_ref[...]
    gate = r * jax.nn.sigmoid(r)
    hn = hnorm_ref[...]
    outs = []
    for h in range(GLA_HEADS):
        oh = o[:, h * GLA_DV:(h + 1) * GLA_DV]
        outs.append(_rmsnorm(oh, hn))
    o_ref[...] = (jnp.concatenate(outs, axis=1) * gate).astype(o_ref.dtype)


def _gla_head_selector():
    r = np.arange(GLA_SUB * GLA_QK)
    j, h = r // GLA_QK, (r % GLA_QK) // GLA_DK
    sel = np.zeros((GLA_SUB * GLA_QK, LANES), np.float32)
    sel[r, h * GLA_SUB + j] = 1.0
    return jnp.asarray(sel, BF16)


def _gla(gqk, gv, gr, gg, hnorm):
    lp = gqk.shape[0]
    assert GLA_HEADS * GLA_SUB <= LANES

    def row(width):
        return pl.BlockSpec((GLA_STEP, width), lambda i: (i, 0))

    return pl.pallas_call(
        _gla_kernel,
        grid=(lp // GLA_STEP,),
        in_specs=[row(2 * GLA_QK), row(GLA_WIDTH), row(GLA_WIDTH), row(GLA_QK), _const_spec((1, GLA_DV)),
                  _const_spec((GLA_SUB * GLA_QK, LANES))],
        out_specs=row(GLA_WIDTH),
        out_shape=jax.ShapeDtypeStruct((lp, GLA_WIDTH), BF16),
        scratch_shapes=[pltpu.VMEM((GLA_QK, GLA_DV), F32)],
        compiler_params=pltpu.CompilerParams(dimension_semantics=("arbitrary",),
                                             vmem_limit_bytes=VMEM_LIMIT),
        name="gla",
    )(gqk, gv, gr, gg, hnorm, _gla_head_selector())


def _key_to_float(u):
    bits = jnp.where(u >= 0, u, u ^ jnp.int32(0x7FFFFFFF))
    return lax.bitcast_convert_type(bits, F32)


def _float_to_key(f):
    bits = lax.bitcast_convert_type(f, jnp.int32)
    return jnp.where(bits >= 0, bits, bits ^ jnp.int32(0x7FFFFFFF))


_KEY_NEG_INF = np.int32(np.uint32(0xFF800000) ^ np.uint32(0x7FFFFFFF))


def _tile_loop(n, body, carry, unroll):
    start = 0
    for width in ((unroll,) if isinstance(unroll, int) else unroll):
        trips = (n - start) // width

        def group(u, c, width=width, start=start):
            for r in range(width):
                c = body(start + u * width + r, c)
            return c

        carry = lax.fori_loop(0, trips, group, carry)
        start = start + trips * width
    return lax.fori_loop(start, n, body, carry)


def _fold_rows(x, op):
    parts = [x[r:r + SUBLANES, :] for r in range(0, x.shape[0], SUBLANES)]
    while len(parts) > 1:
        parts = [op(parts[a], parts[a + 1]) if a + 1 < len(parts) else parts[a] for a in range(0, len(parts), 2)]
    return parts[0]


def _dsa_block(qt_ref, iqt_ref, iwt_ref, k_ref, vt_ref, ik_ref, o_ref, sc_ref, bias_ref, acc_ref, rhs_ref,
               sa_ref, sb_ref, top_ref, k_sel):
    i = pl.program_id(0)
    n_tiles = (i * QBLK + QBLK + KTILE - 1) // KTILE
    last = n_tiles - 1
    q_pos = i * QBLK + lax.broadcasted_iota(jnp.int32, (KSUB, QBLK), 1)
    key_row = lax.broadcasted_iota(jnp.int32, (KSUB, QBLK), 0)

    def visible(ks):
        return (ks + key_row) <= q_pos

    def peeled(body, carry):
        carry = _tile_loop(last, functools.partial(body, masked=False), carry, (4, 2))
        return body(last, carry, masked=True)

    iqt = iqt_ref[...]
    iq_all = jnp.concatenate([iqt[h * IDX_HD:(h + 1) * IDX_HD, :] for h in range(IDX_HEADS)], axis=1)
    iw = iwt_ref[MISC_IW:MISC_IW + IDX_HEADS, :]
    top_ref[...] = jnp.full(top_ref.shape, -jnp.inf, F32)

    def score_tile(t, carry, masked):
        for sub in range(KTILE // KSUB):
            ks = pl.multiple_of(t * KTILE + sub * KSUB, KSUB)
            lg = _dot(ik_ref[pl.ds(ks, KSUB), :], iq_all)
            sc = iw[0:1, :] * jnp.maximum(lg[:, 0:QBLK], 0.0)
            for h in range(1, IDX_HEADS):
                sc = sc + iw[h:h + 1, :] * jnp.maximum(lg[:, h * QBLK:(h + 1) * QBLK], 0.0)
            if masked:
                sc = jnp.where(visible(ks), sc, -jnp.inf)
            sc_ref[pl.ds(ks, KSUB), :] = sc
            rest = sc
            for r in range(TOP_R):
                rows = slice(r * N_GROUPS + (sub % 2) * KSUB, r * N_GROUPS + (sub % 2 + 1) * KSUB)
                held = top_ref[rows, :]
                top_ref[rows, :] = jnp.maximum(held, rest)
                if r + 1 < TOP_R:
                    rest = jnp.minimum(held, rest)
        return carry

    peeled(score_tile, 0)

    kk = float(k_sel)

    def lane_sum(x):
        return jnp.sum(_fold_rows(x, jnp.add), axis=0, keepdims=True)

    def count_scores(pred):
        def body(t, acc):
            ks = pl.multiple_of(t * KTILE, KTILE)
            return acc + _fold_rows(jnp.where(pred(sc_ref[pl.ds(ks, KTILE), :]), 1.0, 0.0), jnp.add)
        return jnp.sum(_tile_loop(n_tiles, body, jnp.zeros((SUBLANES, QBLK), F32), 4), axis=0, keepdims=True)

    def count_ge(cand):
        return lane_sum(jnp.where(top_ref[...] >= cand, 1.0, 0.0))

    best = top_ref[0:N_GROUPS, :]
    lo = jnp.maximum(_float_to_key(jnp.min(_fold_rows(best, jnp.minimum), axis=0, keepdims=True)), _KEY_NEG_INF)
    hi = _float_to_key(jnp.max(_fold_rows(best, jnp.maximum), axis=0, keepdims=True))
    n_lo = count_ge(_key_to_float(lo))

    def settled(lo, hi, n_lo):
        return jnp.logical_or(n_lo <= kk, lo >= hi)

    def any_open(lo, hi, n_lo):
        return jnp.max(jnp.where(settled(lo, hi, n_lo), 0.0, 1.0))

    def midpoint(lo, hi):
        return (lo | hi) - ((lo ^ hi) >> 1)

    def probe_at(key, lo, hi, n_lo):
        open_ = jnp.logical_not(settled(lo, hi, n_lo))
        cnt = count_ge(_key_to_float(key))
        up = jnp.logical_and(open_, cnt >= kk)
        down = jnp.logical_and(open_, cnt < kk)
        return jnp.where(up, key, lo), jnp.where(down, key - 1, hi), jnp.where(up, cnt, n_lo), cnt, up, down

    def halve(lo, hi, n_lo):
        return probe_at(midpoint(lo, hi), lo, hi, n_lo)[:3]

    log_k = float(np.log(k_sel))

    def interpolate(st):
        lo, hi, n_lo, g_lo, g_hi, side = st
        f_lo, f_hi = _key_to_float(lo), _key_to_float(hi)
        guess = f_lo + (g_lo - log_k) / jnp.maximum(g_lo - g_hi, 1e-9) * (f_hi - f_lo)
        usable = jnp.logical_and(lo > _KEY_NEG_INF, jnp.abs(guess) < jnp.inf)
        key = jnp.where(usable, jnp.clip(_float_to_key(guess), lo + 1, hi), midpoint(lo, hi))
        lo, hi, n_lo, cnt, up, down = probe_at(key, lo, hi, n_lo)
        g_cnt = jnp.log(jnp.maximum(cnt, 0.5))
        g_hi = jnp.where(down, g_cnt, jnp.where(jnp.logical_and(up, side > 0.0), 0.5 * (g_hi + log_k), g_hi))
        g_lo = jnp.where(up, g_cnt, jnp.where(jnp.logical_and(down, side < 0.0), 0.5 * (g_lo + log_k), g_lo))
        side = jnp.where(up, 1.0, jnp.where(down, -1.0, side))
        return lo, hi, n_lo, g_lo, g_hi, side

    def checked_halve(st):
        lo, hi, n_lo = halve(*halve(*st[:3]))
        return lo, hi, n_lo, any_open(lo, hi, n_lo), st[4] + 1

    lo, hi, n_lo, _, _, _ = lax.fori_loop(
        0, INTERPOLATED_PROBES, lambda _, st: interpolate(st),
        (lo, hi, n_lo, jnp.log(jnp.maximum(n_lo, 0.5)), jnp.full((1, QBLK), float(np.log(0.5)), F32),
         jnp.zeros((1, QBLK), F32)))
    lo, hi, n_lo = lax.fori_loop(0, UNCHECKED_HALVINGS, lambda _, st: halve(*st), (lo, hi, n_lo))
    lo, hi, n_lo, _, _ = lax.while_loop(lambda st: jnp.logical_and(st[3] > 0.0, st[4] < 17), checked_halve,
                                        (lo, hi, n_lo, any_open(lo, hi, n_lo), jnp.int32(0)))
    tops = top_ref[...]
    thr = jnp.min(_fold_rows(jnp.where(tops >= _key_to_float(lo), tops, jnp.inf), jnp.minimum), axis=0,
                  keepdims=True)
    n_ge = count_scores(lambda x: x >= thr)

    def next_above(v):
        def body(t, carry):
            low, cnt = carry
            ks = pl.multiple_of(t * KTILE, KTILE)
            x = sc_ref[pl.ds(ks, KTILE), :]
            above = x > v
            return (jnp.minimum(low, _fold_rows(jnp.where(above, x, jnp.inf), jnp.minimum)),
                    cnt + _fold_rows(jnp.where(above, 1.0, 0.0), jnp.add))
        low, cnt = _tile_loop(n_tiles, body, (jnp.full((SUBLANES, QBLK), jnp.inf, F32),
                                              jnp.zeros((SUBLANES, QBLK), F32)), 4)
        return jnp.min(low, axis=0, keepdims=True), jnp.sum(cnt, axis=0, keepdims=True)

    def any_climbing(n_ge, stuck):
        return jnp.max(jnp.where(jnp.logical_or(n_ge <= kk, stuck > 0.0), 0.0, 1.0))

    def climb(st):
        thr, n_ge, stuck, _, it = st
        nxt, n_nxt = next_above(thr)
        go = jnp.logical_and(jnp.logical_and(n_ge > kk, stuck == 0.0), n_nxt >= kk)
        stuck = jnp.where(jnp.logical_and(n_ge > kk, n_nxt < kk), 1.0, stuck)
        thr, n_ge = jnp.where(go, nxt, thr), jnp.where(go, n_nxt, n_ge)
        return thr, n_ge, stuck, any_climbing(n_ge, stuck), it + 1

    stuck = jnp.zeros((1, QBLK), F32)
    thr, n_ge, _, _, _ = lax.while_loop(
        lambda st: jnp.logical_and(st[3] > 0.0, st[4] <= n_tiles * KTILE), climb,
        (thr, n_ge, stuck, any_climbing(n_ge, stuck), jnp.int32(0)))
    has_ties = jnp.max(n_ge) > kk

    def bias_fast(t, carry, masked):
        for sub in range(KTILE // KSUB):
            ks = pl.multiple_of(t * KTILE + sub * KSUB, KSUB)
            keep = sc_ref[pl.ds(ks, KSUB), :] >= thr
            if masked:
                keep = jnp.logical_and(keep, visible(ks))
            bias_ref[pl.ds(ks, KSUB), :] = jnp.where(keep, 0.0, NEG_BIG).astype(BF16)
        return carry

    def bias_ties(t, run, masked, budget, tri):
        for sub in range(KTILE // KSUB):
            ks = pl.multiple_of(t * KTILE + sub * KSUB, KSUB)
            x = sc_ref[pl.ds(ks, KSUB), :]
            above = x > thr
            tie = x == thr
            tie_f = jnp.where(tie, 1.0, 0.0)
            rank = run + _dot(tri, tie_f.astype(BF16))
            keep = jnp.logical_or(above, jnp.logical_and(tie, rank <= budget))
            if masked:
                keep = jnp.logical_and(keep, visible(ks))
            bias_ref[pl.ds(ks, KSUB), :] = jnp.where(keep, 0.0, NEG_BIG).astype(BF16)
            run = run + jnp.sum(tie_f, axis=0, keepdims=True)
        return run

    @pl.when(jnp.logical_not(has_ties))
    def _():
        peeled(bias_fast, 0)

    @pl.when(has_ties)
    def _():
        budget = kk - count_scores(lambda x: x > thr)
        tri = (lax.broadcasted_iota(jnp.int32, (KSUB, KSUB), 1) <= key_row).astype(BF16)
        peeled(functools.partial(bias_ties, budget=budget, tri=tri), jnp.zeros((1, QBLK), F32))

    qt = qt_ref[...]
    eye = (lax.broadcasted_iota(jnp.int32, (QBLK, QBLK), 0)
           == lax.broadcasted_iota(jnp.int32, (QBLK, QBLK), 1)).astype(BF16)
    zero = jnp.zeros((DSA_HD, QBLK), BF16)
    n_pairs = DSA_HEADS // 2
    for c in range(n_pairs):
        qa = qt[(2 * c) * DSA_HD:(2 * c + 1) * DSA_HD, :]
        qb = qt[(2 * c + 1) * DSA_HD:(2 * c + 2) * DSA_HD, :]
        rhs_ref[c] = jnp.concatenate([jnp.concatenate([qa, zero], axis=1),
                                      jnp.concatenate([zero, qb], axis=1),
                                      jnp.concatenate([eye, eye], axis=1)], axis=0)
    acc_ref[...] = jnp.zeros(acc_ref.shape, F32)
    ones_rows = jnp.ones((ACC_ROWS - DSA_HD, ATILE), BF16)

    def qk_stage(ks, s_ref):
        bias = bias_ref[pl.ds(ks, ATILE), :]
        tile_max = []
        for c in range(n_pairs):
            lhs = jnp.concatenate([k_ref[pl.ds(ks, ATILE), c * 2 * DSA_HD:(c + 1) * 2 * DSA_HD], bias], axis=1)
            s = _dot(lhs, rhs_ref[c])
            s_ref[c] = s
            tile_max.append(_fold_rows(s, jnp.maximum))
        return tuple(tile_max)

    def pv_stage(ks, s_ref, tile_max, ms):
        new_ms = []
        for c in range(n_pairs):
            m_new = jnp.maximum(ms[c], jnp.max(tile_max[c], axis=0, keepdims=True))
            alpha = jnp.exp(ms[c] - m_new)
            p = jnp.exp(s_ref[c] - m_new).astype(BF16)
            for hh in range(2):
                h = 2 * c + hh
                rows = slice(h * ACC_ROWS, (h + 1) * ACC_ROWS)
                vt_aug = jnp.concatenate([vt_ref[h * DSA_HD:(h + 1) * DSA_HD, pl.ds(ks, ATILE)], ones_rows], axis=0)
                pv = _dot(vt_aug, p[:, hh * QBLK:(hh + 1) * QBLK])
                acc_ref[rows, :] = alpha[:, hh * QBLK:(hh + 1) * QBLK] * acc_ref[rows, :] + pv
            new_ms.append(m_new)
        return tuple(new_ms)

    def attn_tile(t, carry):
        max_a, ms = carry
        k0 = pl.multiple_of(t * KTILE, KTILE)
        k1 = pl.multiple_of(t * KTILE + ATILE, ATILE)
        k2 = pl.multiple_of(t * KTILE + KTILE, KTILE)
        max_b = qk_stage(k1, sb_ref)
        ms = pv_stage(k0, sa_ref, max_a, ms)
        max_a = qk_stage(k2, sa_ref)
        ms = pv_stage(k1, sb_ref, max_b, ms)
        return max_a, ms

    ms = tuple(jnp.full((1, 2 * QBLK), NEG_BIG, F32) for _ in range(n_pairs))
    max_a, ms = _tile_loop(last, attn_tile, (qk_stage(0, sa_ref), ms), (4, 2))
    k0 = pl.multiple_of(last * KTILE, KTILE)
    k1 = pl.multiple_of(last * KTILE + ATILE, ATILE)
    max_b = qk_stage(k1, sb_ref)
    ms = pv_stage(k0, sa_ref, max_a, ms)
    pv_stage(k1, sb_ref, max_b, ms)
    outs = []
    for h in range(DSA_HEADS):
        a = acc_ref[h * ACC_ROWS:(h + 1) * ACC_ROWS, :]
        outs.append(a[:DSA_HD, :] / a[DSA_HD:DSA_HD + 1, :])
    o_ref[...] = jnp.concatenate(outs, axis=0).T.astype(o_ref.dtype)


def _dsa_kernel(qt_ref, iqt_ref, iwt_ref, k_ref, vt_ref, ik_ref, o_ref, sc_ref, bias_ref, acc_ref, rhs_ref,
                sa_ref, sb_ref, top_ref, *, k_sel, l_tok):
    has_tokens = pl.program_id(0) * QBLK < l_tok

    @pl.when(has_tokens)
    def _():
        _dsa_block(qt_ref, iqt_ref, iwt_ref, k_ref, vt_ref, ik_ref, o_ref, sc_ref, bias_ref, acc_ref, rhs_ref,
                   sa_ref, sb_ref, top_ref, k_sel)

    @pl.when(jnp.logical_not(has_tokens))
    def _():
        o_ref[...] = jnp.zeros(o_ref.shape, o_ref.dtype)


def _dsa(dqt, dk, dvt, iqt, ik, iwt, k_sel, l_tok):
    lp = dk.shape[0]

    def col(height):
        return pl.BlockSpec((height, QBLK), lambda i: (0, i))

    return pl.pallas_call(
        functools.partial(_dsa_kernel, k_sel=k_sel, l_tok=l_tok),
        grid=(lp // QBLK,),
        in_specs=[col(DSA_WIDTH), col(IDX_WIDTH), col(LANES), _const_spec((lp, DSA_WIDTH)),
                  _const_spec((DSA_WIDTH, lp)), _const_spec((lp, IDX_HD))],
        out_specs=pl.BlockSpec((QBLK, DSA_WIDTH), lambda i: (i, 0)),
        out_shape=jax.ShapeDtypeStruct((lp, DSA_WIDTH), BF16),
        scratch_shapes=[pltpu.VMEM((lp, QBLK), F32), pltpu.VMEM((lp, QBLK), BF16),
                        pltpu.VMEM((DSA_HEADS * ACC_ROWS, QBLK), F32),
                        pltpu.VMEM((DSA_HEADS // 2, 2 * LANES, 2 * QBLK), BF16),
                        pltpu.VMEM((DSA_HEADS // 2, ATILE, 2 * QBLK), F32),
                        pltpu.VMEM((DSA_HEADS // 2, ATILE, 2 * QBLK), F32),
                        pltpu.VMEM((TOP_R * N_GROUPS, QBLK), F32)],
        compiler_params=pltpu.CompilerParams(dimension_semantics=("parallel",),
                                             vmem_limit_bytes=VMEM_LIMIT),
        name="dsa",
    )(dqt, iqt, iwt, dk, dvt, ik)


def _out_ffn2_kernel(h_ref, ht_ref, og_ref, ogt_ref, od_ref, odt_ref, wo_ref, norm_ref, wg_ref, wu_ref, wd_ref,
                     fnorm_ref, o_ref, *, ff_chunk):
    def shifted(main_ref, tail_ref):
        return jnp.concatenate([main_ref[N_META:, :], tail_ref[...]], axis=0)

    h2 = (shifted(h_ref, ht_ref) + _dot(shifted(og_ref, ogt_ref), wo_ref[:GLA_WIDTH, :])
          + _dot(shifted(od_ref, odt_ref), wo_ref[GLA_WIDTH:, :]))
    h3 = _swiglu_half(h2, norm_ref, wg_ref, wu_ref, wd_ref, ff_chunk)
    o_ref[...] = _rmsnorm(h3, fnorm_ref[...])


def _out_ffn2(h, o_gla, o_dsa, wo, norm, wg, wu, wd, fnorm, seq):
    lp, d = h.shape
    d_ff = wg.shape[1]
    n_steps = -(-seq // ROW_TILE)
    assert ROW_TILE % N_META == 0 and n_steps * ROW_TILE + N_META <= lp
    tail_step = ROW_TILE // N_META

    def row(width):
        return pl.BlockSpec((ROW_TILE, width), lambda i: (i, 0))

    def tail(width):
        return pl.BlockSpec((N_META, width), lambda i: ((i + 1) * tail_step, 0))

    return pl.pallas_call(
        functools.partial(_out_ffn2_kernel, ff_chunk=_ffn_chunk(d_ff)),
        grid=(n_steps,),
        in_specs=[row(d), tail(d), row(GLA_WIDTH), tail(GLA_WIDTH), row(DSA_WIDTH), tail(DSA_WIDTH),
                  _const_spec((GLA_WIDTH + DSA_WIDTH, d)), _const_spec((1, d)), _const_spec((d, d_ff)),
                  _const_spec((d, d_ff)), _const_spec((d_ff, d)), _const_spec((1, d))],
        out_specs=row(d),
        out_shape=jax.ShapeDtypeStruct((seq, d), F32),
        compiler_params=pltpu.CompilerParams(dimension_semantics=("parallel",),
                                             vmem_limit_bytes=VMEM_LIMIT),
        name="out_ffn2",
    )(h, h, o_gla, o_gla, o_dsa, o_dsa, wo, norm, wg, wu, wd, fnorm)


def _rope_tables(lp):
    pos = jnp.arange(lp, dtype=F32)
    inv_freq = 1.0 / (ROPE_THETA ** (jnp.arange(0, DSA_HD, 2, dtype=F32) / DSA_HD))
    ang = pos[:, None] * inv_freq[None, :]
    cos = jnp.cos(ang)
    sin = jnp.sin(ang)
    cos2 = jnp.concatenate([cos, cos, cos, cos], axis=-1)
    sin2s = jnp.concatenate([-sin, sin, -sin, sin], axis=-1)
    return cos2, sin2s


def _layer(h, meta, k_sel, l_tok, ffn1_norm, ffn1_w_gate, ffn1_w_up, ffn1_w_down, mix_norm, w_in, gla_w_a2, gla_b_a,
           gla_head_norm, w_out, ffn2_norm, ffn2_w_gate, ffn2_w_up, ffn2_w_down, final_norm, cos2, sin2s):
    d = h.shape[1]
    h1 = _ffn1(h, meta, ffn1_norm.reshape(1, d), ffn1_w_gate.astype(BF16), ffn1_w_up.astype(BF16),
               ffn1_w_down.astype(BF16))

    sizes = (GLA_QK, GLA_QK, GLA_WIDTH, GLA_WIDTH, GLA_LOWRANK, DSA_WIDTH, DSA_WIDTH, DSA_WIDTH, IDX_WIDTH,
             IDX_HD, IDX_HEADS)
    offs = np.concatenate([[0], np.cumsum(sizes)])
    col = lambda n: w_in[:, offs[n]:offs[n + 1]]
    pad = jnp.zeros((d, LANES - IDX_HD - IDX_HEADS - GLA_LOWRANK), w_in.dtype)
    w_all = jnp.concatenate([col(0), col(1), col(2), col(3), col(5), col(6), col(7), col(8),
                             col(9), col(10), col(4), pad], axis=1).astype(BF16)
    wa2p = jnp.zeros((LANES, GLA_QK), F32).at[MISC_GA:MISC_GA + GLA_LOWRANK, :].set(gla_w_a2).astype(BF16)

    gqk, gv, gr, gg, dqt, dk, dvt, iqt, ik, iwt = _in_proj(
        h1, mix_norm.reshape(1, d), w_all, wa2p, gla_b_a.reshape(1, GLA_QK), cos2, sin2s)
    o_gla = _gla(gqk, gv, gr, gg, gla_head_norm.reshape(1, GLA_DV))
    o_dsa = _dsa(dqt, dk, dvt, iqt, ik, iwt, k_sel, l_tok)
    return _out_ffn2(h1, o_gla, o_dsa, w_out.astype(BF16), ffn2_norm.reshape(1, d), ffn2_w_gate.astype(BF16),
                     ffn2_w_up.astype(BF16), ffn2_w_down.astype(BF16), final_norm.reshape(1, d), l_tok - N_META)


def kernel(x, meta_tokens, ffn1_norm, ffn1_w_gate, ffn1_w_up, ffn1_w_down, mix_norm, w_in, gla_w_a2, gla_b_a,
           gla_head_norm, w_out, ffn2_norm, ffn2_w_gate, ffn2_w_up, ffn2_w_down, final_norm):
    batch, seq, d = x.shape
    depth = ffn1_norm.shape[0]
    assert depth == 1, "the final norm is fused into the layer's last kernel"
    k_sel = min(TOPK_MAX, seq // 4)
    l_tok = seq + N_META
    lp = -(-max(l_tok, -(-seq // ROW_TILE) * ROW_TILE + N_META) // ROW_PAD) * ROW_PAD
    cos2, sin2s = _rope_tables(lp)
    outs = []
    for bi in range(batch):
        h = jnp.pad(x[bi], ((N_META, lp - l_tok), (0, 0)))
        outs.append(_layer(h, meta_tokens.astype(x.dtype), k_sel, l_tok, ffn1_norm[0], ffn1_w_gate[0], ffn1_w_up[0], ffn1_w_down[0],
                           mix_norm[0], w_in[0], gla_w_a2[0], gla_b_a[0], gla_head_norm[0], w_out[0], ffn2_norm[0],
                           ffn2_w_gate[0], ffn2_w_up[0], ffn2_w_down[0], final_norm, cos2, sin2s))
    return jnp.stack(outs, axis=0)
```

```python
import functools

import numpy as np
import jax
import jax.numpy as jnp
from jax import lax
from jax.experimental import pallas as pl
from jax.experimental.pallas import tpu as pltpu

N_META = 16
EPS = 1e-6
ROPE_THETA = 10000.0
GLA_HEADS = 4
GLA_DK = 64
GLA_DV = 128
GLA_LOWRANK = 16
GLA_GATE_NORM = 16.0
DSA_HEADS = 8
DSA_HD = 64
IDX_HEADS = 8
IDX_HD = 64
TOPK_MAX = 256

GLA_QK = GLA_HEADS * GLA_DK
GLA_WIDTH = GLA_HEADS * GLA_DV
DSA_WIDTH = DSA_HEADS * DSA_HD
IDX_WIDTH = IDX_HEADS * IDX_HD

LANES = 128
SUBLANES = 8
ROW_TILE = 384
QBLK = 128
KSUB = 128
KTILE = 512
ATILE = KTILE // 2
N_GROUPS = 2 * KSUB
TOP_R = 6
INTERPOLATED_PROBES = 8
UNCHECKED_HALVINGS = 4
GLA_STEP = 128
GLA_SUB = 16
ROW_PAD = 1536
ACC_ROWS = DSA_HD + 16
VMEM_LIMIT = 60 * 1024 * 1024
NEG_BIG = -1e30

MISC_IK = 0
MISC_IW = IDX_HD
MISC_GA = IDX_HD + IDX_HEADS

F32 = jnp.float32
BF16 = jnp.bfloat16


def _dot(a, b):
    return jnp.dot(a, b, preferred_element_type=F32)


def _rmsnorm(x, g):
    return x * lax.rsqrt(jnp.mean(x * x, axis=-1, keepdims=True) + EPS) * g


def _const_spec(shape):
    return pl.BlockSpec(shape, lambda *_: (0,) * len(shape), pipeline_mode=pl.Buffered(1))


def _swiglu_half(x, norm_ref, wg_ref, wu_ref, wd_ref, ff_chunk):
    hn = _rmsnorm(x, norm_ref[...]).astype(BF16)
    d_ff = wg_ref.shape[1]
    acc = jnp.zeros(x.shape, F32)
    for c in range(d_ff // ff_chunk):
        sl = slice(c * ff_chunk, (c + 1) * ff_chunk)
        g = _dot(hn, wg_ref[:, sl])
        u = _dot(hn, wu_ref[:, sl])
        a = (g * jax.nn.sigmoid(g) * u).astype(BF16)
        acc = acc + _dot(a, wd_ref[sl, :])
    return x + 0.5 * acc


def _ffn1_kernel(h_ref, meta_ref, norm_ref, wg_ref, wu_ref, wd_ref, o_ref, *, ff_chunk):
    x = h_ref[...]
    rows = lax.broadcasted_iota(jnp.int32, (x.shape[0], 1), 0)
    meta = jnp.concatenate([meta_ref[...], jnp.zeros((x.shape[0] - N_META, x.shape[1]), x.dtype)], axis=0)
    x = jnp.where(jnp.logical_and(pl.program_id(0) == 0, rows < N_META), meta, x)
    o_ref[...] = _swiglu_half(x, norm_ref, wg_ref, wu_ref, wd_ref, ff_chunk)


def _ffn_chunk(d_ff):
    if d_ff <= 4096:
        return d_ff
    for c in (2048, 1024, 512, 256):
        if d_ff % c == 0:
            return c
    return d_ff


def _ffn1(h, meta, norm, wg, wu, wd):
    lp, d = h.shape
    d_ff = wg.shape[1]
    row = pl.BlockSpec((ROW_TILE, d), lambda i: (i, 0))
    return pl.pallas_call(
        functools.partial(_ffn1_kernel, ff_chunk=_ffn_chunk(d_ff)),
        grid=(lp // ROW_TILE,),
        in_specs=[row, _const_spec((N_META, d)), _const_spec((1, d)), _const_spec((d, d_ff)),
                  _const_spec((d, d_ff)), _const_spec((d_ff, d))],
        out_specs=row,
        out_shape=jax.ShapeDtypeStruct((lp, d), F32),
        compiler_params=pltpu.CompilerParams(dimension_semantics=("parallel",),
                                             vmem_limit_bytes=VMEM_LIMIT),
        name="ffn1",
    )(h, meta, norm, wg, wu, wd)


def _rope(x, cos, sin_signed, first_half):
    parts = []
    for c in range(x.shape[1] // LANES):
        xc = x[:, c * LANES:(c + 1) * LANES]
        rot = jnp.where(first_half, pltpu.roll(xc, LANES - 32, 1), pltpu.roll(xc, 32, 1))
        parts.append(xc * cos + rot * sin_signed)
    return parts[0] if len(parts) == 1 else jnp.concatenate(parts, axis=1)


def _log_sigmoid(x):
    return jnp.minimum(x, 0.0) - jnp.log(1.0 + jnp.exp(-jnp.abs(x)))


def _in_proj_kernel(h_ref, norm_ref, w_ref, wa2_ref, ba_ref, cos_ref, sin_ref,
                    gqk_ref, gv_ref, gr_ref, gg_ref, dqt_ref, dk_ref, dvt_ref, iqt_ref, ik_ref, iwt_ref):
    hn = _rmsnorm(h_ref[...], norm_ref[...]).astype(BF16)
    cos = cos_ref[...]
    sin_s = sin_ref[...]
    lane = lax.broadcasted_iota(jnp.int32, cos.shape, 1)
    first_half = (lane % DSA_HD) < (DSA_HD // 2)

    def proj(c0, width):
        return _dot(hn, w_ref[:, c0:c0 + width])

    c = 0
    gqk_ref[...] = proj(c, 2 * GLA_QK); c += 2 * GLA_QK
    gv_ref[...] = proj(c, GLA_WIDTH); c += GLA_WIDTH
    gr_ref[...] = proj(c, GLA_WIDTH); c += GLA_WIDTH
    dqt_ref[...] = (_rope(proj(c, DSA_WIDTH), cos, sin_s, first_half) * (DSA_HD ** -0.5)).T.astype(BF16)
    c += DSA_WIDTH
    dk_ref[...] = _rope(proj(c, DSA_WIDTH), cos, sin_s, first_half).astype(BF16); c += DSA_WIDTH
    dvt_ref[...] = proj(c, DSA_WIDTH).T.astype(BF16); c += DSA_WIDTH
    iqt_ref[...] = (_rope(proj(c, IDX_WIDTH), cos, sin_s, first_half) * (IDX_HD ** -0.5)).T.astype(BF16)
    c += IDX_WIDTH
    misc = proj(c, LANES)
    ik_ref[...] = _rope(misc, cos, sin_s, first_half)[:, MISC_IK:MISC_IK + IDX_HD].astype(BF16)
    iwt_ref[...] = (misc * (IDX_HEADS ** -0.5)).T
    pre = _dot(misc.astype(BF16), wa2_ref[...]) + ba_ref[...]
    gg_ref[...] = _log_sigmoid(pre) * (1.0 / GLA_GATE_NORM)


def _in_proj(h, norm, w_all, wa2p, ba, cos2, sin2s):
    lp, d = h.shape
    ncols = w_all.shape[1]

    def row(width):
        return pl.BlockSpec((ROW_TILE, width), lambda i: (i, 0))

    def col(height):
        return pl.BlockSpec((height, ROW_TILE), lambda i: (0, i))

    outs = [
        (row(2 * GLA_QK), (lp, 2 * GLA_QK), F32), (row(GLA_WIDTH), (lp, GLA_WIDTH), F32),
        (row(GLA_WIDTH), (lp, GLA_WIDTH), F32), (row(GLA_QK), (lp, GLA_QK), F32),
        (col(DSA_WIDTH), (DSA_WIDTH, lp), BF16), (row(DSA_WIDTH), (lp, DSA_WIDTH), BF16),
        (col(DSA_WIDTH), (DSA_WIDTH, lp), BF16), (col(IDX_WIDTH), (IDX_WIDTH, lp), BF16),
        (row(IDX_HD), (lp, IDX_HD), BF16), (col(LANES), (LANES, lp), F32),
    ]
    return pl.pallas_call(
        _in_proj_kernel,
        grid=(lp // ROW_TILE,),
        in_specs=[row(d), _const_spec((1, d)), _const_spec((d, ncols)), _const_spec((LANES, GLA_QK)),
                  _const_spec((1, GLA_QK)), row(LANES), row(LANES)],
        out_specs=[spec for spec, _, _ in outs],
        out_shape=[jax.ShapeDtypeStruct(shape, dt) for _, shape, dt in outs],
        compiler_params=pltpu.CompilerParams(dimension_semantics=("parallel",),
                                             vmem_limit_bytes=VMEM_LIMIT),
        name="in_proj",
    )(h, norm, w_all, wa2p, ba, cos2, sin2s)


def _gla_kernel(qk_ref, v_ref, r_ref, g_ref, hnorm_ref, esel_ref, o_ref, s_ref):
    n_sub = GLA_STEP // GLA_SUB

    @pl.when(pl.program_id(0) == 0)
    def _():
        s_ref[...] = jnp.zeros(s_ref.shape, F32)

    q = qk_ref[:, :GLA_QK] * (GLA_DK ** -0.5)
    k = qk_ref[:, GLA_QK:]
    v = v_ref[...]
    g = g_ref[...]

    ri = lax.broadcasted_iota(jnp.int32, (GLA_STEP, GLA_STEP), 0)
    ci = lax.broadcasted_iota(jnp.int32, (GLA_STEP, GLA_STEP), 1)
    tri = ((ri // GLA_SUB == ci // GLA_SUB) & (ci <= ri)).astype(F32)
    b = jnp.dot(tri, g, preferred_element_type=F32, precision=lax.Precision.HIGHEST)

    b3 = b.reshape(n_sub, GLA_SUB, GLA_QK)
    k3 = k.reshape(n_sub, GLA_SUB, GLA_QK)
    b_last = jnp.broadcast_to(b3[:, GLA_SUB - 1:GLA_SUB, :], b3.shape).reshape(GLA_STEP, GLA_QK)

    pos = lax.broadcasted_iota(jnp.int32, (GLA_STEP, GLA_QK), 0) % GLA_SUB
    t_cols = []
    for j in range(GLA_SUB):
        kj = jnp.broadcast_to(k3[:, j:j + 1, :], k3.shape).reshape(GLA_STEP, GLA_QK)
        bj = jnp.broadcast_to(b3[:, j:j + 1, :], b3.shape).reshape(GLA_STEP, GLA_QK)
        e = jnp.where(pos >= j, b - bj, -jnp.inf)
        t_cols.append((q * kj * jnp.exp(e)).astype(BF16))
    a = _dot(jnp.concatenate(t_cols, axis=1), esel_ref[...]).astype(BF16)
    head_of_lane = lax.broadcasted_iota(jnp.int32, (GLA_SUB, GLA_WIDTH), 1) // GLA_DV
    o_intra = []
    for c in range(n_sub):
        rows = slice(c * GLA_SUB, (c + 1) * GLA_SUB)
        v_c = v[rows, :]
        v_heads = jnp.concatenate([jnp.where(head_of_lane == h, v_c, 0.0) for h in range(GLA_HEADS)]
                                  + [jnp.zeros((LANES - GLA_HEADS * GLA_SUB, GLA_WIDTH), F32)], axis=0)
        o_intra.append(_dot(a[rows, :], v_heads.astype(BF16)))
    o = jnp.concatenate(o_intra, axis=0)

    qe = q * jnp.exp(b)
    kd = k * jnp.exp(b_last - b)
    kd_t = kd.T
    dec_t = jnp.exp(b).T
    lane_head = lax.broadcasted_iota(jnp.int32, (GLA_SUB, GLA_QK), 1) // GLA_DK
    s = s_ref[...]
    o_inter = []
    for c in range(n_sub):
        rows = slice(c * GLA_SUB, (c + 1) * GLA_SUB)
        qe_c = qe[rows, :]
        q_stack = jnp.concatenate(
            [jnp.where(lane_head == h, qe_c, 0.0) for h in range(GLA_HEADS)], axis=0).astype(BF16)
        r_stack = _dot(q_stack, s.astype(BF16))
        o_inter.append(jnp.concatenate(
            [r_stack[h * GLA_SUB:(h + 1) * GLA_SUB, :] for h in range(GLA_HEADS)], axis=1))
        kd_c = kd_t[:, rows].astype(BF16)
        v_c = v[rows, :].astype(BF16)
        u_c = jnp.concatenate(
            [_dot(kd_c[h * GLA_DK:(h + 1) * GLA_DK, :], v_c[:, h * GLA_DV:(h + 1) * GLA_DV])
             for h in range(GLA_HEADS)], axis=0)
        last = c * GLA_SUB + GLA_SUB - 1
        s = dec_t[:, last:last + 1] * s + u_c
    s_ref[...] = s
    o = o + jnp.concatenate(o_inter, axis=0)

    r = r_ref[...]
    gate = r * jax.nn.sigmoid(r)
    hn = hnorm_ref[...]
    outs = []
    for h in range(GLA_HEADS):
        oh = o[:, h * GLA_DV:(h + 1) * GLA_DV]
        outs.append(_rmsnorm(oh, hn))
    o_ref[...] = (jnp.concatenate(outs, axis=1) * gate).astype(o_ref.dtype)


def _gla_head_selector():
    r = np.arange(GLA_SUB * GLA_QK)
    j, h = r // GLA_QK, (r % GLA_QK) // GLA_DK
    sel = np.zeros((GLA_SUB * GLA_QK, LANES), np.float32)
    sel[r, h * GLA_SUB + j] = 1.0
    return jnp.asarray(sel, BF16)


def _gla(gqk, gv, gr, gg, hnorm):
    lp = gqk.shape[0]
    assert GLA_HEADS * GLA_SUB <= LANES

    def row(width):
        return pl.BlockSpec((GLA_STEP, width), lambda i: (i, 0))

    return pl.pallas_call(
        _gla_kernel,
        grid=(lp // GLA_STEP,),
        in_specs=[row(2 * GLA_QK), row(GLA_WIDTH), row(GLA_WIDTH), row(GLA_QK), _const_spec((1, GLA_DV)),
                  _const_spec((GLA_SUB * GLA_QK, LANES))],
        out_specs=row(GLA_WIDTH),
        out_shape=jax.ShapeDtypeStruct((lp, GLA_WIDTH), BF16),
        scratch_shapes=[pltpu.VMEM((GLA_QK, GLA_DV), F32)],
        compiler_params=pltpu.CompilerParams(dimension_semantics=("arbitrary",),
                                             vmem_limit_bytes=VMEM_LIMIT),
        name="gla",
    )(gqk, gv, gr, gg, hnorm, _gla_head_selector())


def _key_to_float(u):
    bits = jnp.where(u >= 0, u, u ^ jnp.int32(0x7FFFFFFF))
    return lax.bitcast_convert_type(bits, F32)


def _float_to_key(f):
    bits = lax.bitcast_convert_type(f, jnp.int32)
    return jnp.where(bits >= 0, bits, bits ^ jnp.int32(0x7FFFFFFF))


_KEY_NEG_INF = np.int32(np.uint32(0xFF800000) ^ np.uint32(0x7FFFFFFF))


def _tile_loop(n, body, carry, unroll):
    start = 0
    for width in ((unroll,) if isinstance(unroll, int) else unroll):
        trips = (n - start) // width

        def group(u, c, width=width, start=start):
            for r in range(width):
                c = body(start + u * width + r, c)
            return c

        carry = lax.fori_loop(0, trips, group, carry)
        start = start + trips * width
    return lax.fori_loop(start, n, body, carry)


def _fold_rows(x, op):
    parts = [x[r:r + SUBLANES, :] for r in range(0, x.shape[0], SUBLANES)]
    while len(parts) > 1:
        parts = [op(parts[a], parts[a + 1]) if a + 1 < len(parts) else parts[a] for a in range(0, len(parts), 2)]
    return parts[0]


def _dsa_block(qt_ref, iqt_ref, iwt_ref, k_ref, vt_ref, ik_ref, o_ref, sc_ref, bias_ref, acc_ref, rhs_ref,
               sa_ref, sb_ref, top_ref, k_sel):
    i = pl.program_id(0)
    n_tiles = (i * QBLK + QBLK + KTILE - 1) // KTILE
    last = n_tiles - 1
    q_pos = i * QBLK + lax.broadcasted_iota(jnp.int32, (KSUB, QBLK), 1)
    key_row = lax.broadcasted_iota(jnp.int32, (KSUB, QBLK), 0)

    def visible(ks):
        return (ks + key_row) <= q_pos

    def peeled(body, carry):
        carry = _tile_loop(last, functools.partial(body, masked=False), carry, (4, 2))
        return body(last, carry, masked=True)

    iqt = iqt_ref[...]
    iq_all = jnp.concatenate([iqt[h * IDX_HD:(h + 1) * IDX_HD, :] for h in range(IDX_HEADS)], axis=1)
    iw = iwt_ref[MISC_IW:MISC_IW + IDX_HEADS, :]
    top_ref[...] = jnp.full(top_ref.shape, -jnp.inf, F32)

    def score_tile(t, carry, masked):
        for sub in range(KTILE // KSUB):
            ks = pl.multiple_of(t * KTILE + sub * KSUB, KSUB)
            lg = _dot(ik_ref[pl.ds(ks, KSUB), :], iq_all)
            sc = iw[0:1, :] * jnp.maximum(lg[:, 0:QBLK], 0.0)
            for h in range(1, IDX_HEADS):
                sc = sc + iw[h:h + 1, :] * jnp.maximum(lg[:, h * QBLK:(h + 1) * QBLK], 0.0)
            if masked:
                sc = jnp.where(visible(ks), sc, -jnp.inf)
            sc_ref[pl.ds(ks, KSUB), :] = sc
            rest = sc
            for r in range(TOP_R):
                rows = slice(r * N_GROUPS + (sub % 2) * KSUB, r * N_GROUPS + (sub % 2 + 1) * KSUB)
                held = top_ref[rows, :]
                top_ref[rows, :] = jnp.maximum(held, rest)
                if r + 1 < TOP_R:
                    rest = jnp.minimum(held, rest)
        return carry

    peeled(score_tile, 0)

    kk = float(k_sel)

    def lane_sum(x):
        return jnp.sum(_fold_rows(x, jnp.add), axis=0, keepdims=True)

    def count_scores(pred):
        def body(t, acc):
            ks = pl.multiple_of(t * KTILE, KTILE)
            return acc + _fold_rows(jnp.where(pred(sc_ref[pl.ds(ks, KTILE), :]), 1.0, 0.0), jnp.add)
        return jnp.sum(_tile_loop(n_tiles, body, jnp.zeros((SUBLANES, QBLK), F32), 4), axis=0, keepdims=True)

    def count_ge(cand):
        return lane_sum(jnp.where(top_ref[...] >= cand, 1.0, 0.0))

    best = top_ref[0:N_GROUPS, :]
    lo = jnp.maximum(_float_to_key(jnp.min(_fold_rows(best, jnp.minimum), axis=0, keepdims=True)), _KEY_NEG_INF)
    hi = _float_to_key(jnp.max(_fold_rows(best, jnp.maximum), axis=0, keepdims=True))
    n_lo = count_ge(_key_to_float(lo))

    def settled(lo, hi, n_lo):
        return jnp.logical_or(n_lo <= kk, lo >= hi)

    def any_open(lo, hi, n_lo):
        return jnp.max(jnp.where(settled(lo, hi, n_lo), 0.0, 1.0))

    def midpoint(lo, hi):
        return (lo | hi) - ((lo ^ hi) >> 1)

    def probe_at(key, lo, hi, n_lo):
        open_ = jnp.logical_not(settled(lo, hi, n_lo))
        cnt = count_ge(_key_to_float(key))
        up = jnp.logical_and(open_, cnt >= kk)
        down = jnp.logical_and(open_, cnt < kk)
        return jnp.where(up, key, lo), jnp.where(down, key - 1, hi), jnp.where(up, cnt, n_lo), cnt, up, down

    def halve(lo, hi, n_lo):
        return probe_at(midpoint(lo, hi), lo, hi, n_lo)[:3]

    log_k = float(np.log(k_sel))

    def interpolate(st):
        lo, hi, n_lo, g_lo, g_hi, side = st
        f_lo, f_hi = _key_to_float(lo), _key_to_float(hi)
        guess = f_lo + (g_lo - log_k) / jnp.maximum(g_lo - g_hi, 1e-9) * (f_hi - f_lo)
        usable = jnp.logical_and(lo > _KEY_NEG_INF, jnp.abs(guess) < jnp.inf)
        key = jnp.where(usable, jnp.clip(_float_to_key(guess), lo + 1, hi), midpoint(lo, hi))
        lo, hi, n_lo, cnt, up, down = probe_at(key, lo, hi, n_lo)
        g_cnt = jnp.log(jnp.maximum(cnt, 0.5))
        g_hi = jnp.where(down, g_cnt, jnp.where(jnp.logical_and(up, side > 0.0), 0.5 * (g_hi + log_k), g_hi))
        g_lo = jnp.where(up, g_cnt, jnp.where(jnp.logical_and(down, side < 0.0), 0.5 * (g_lo + log_k), g_lo))
        side = jnp.where(up, 1.0, jnp.where(down, -1.0, side))
        return lo, hi, n_lo, g_lo, g_hi, side

    def checked_halve(st):
        lo, hi, n_lo = halve(*halve(*st[:3]))
        return lo, hi, n_lo, any_open(lo, hi, n_lo), st[4] + 1

    lo, hi, n_lo, _, _, _ = lax.fori_loop(
        0, INTERPOLATED_PROBES, lambda _, st: interpolate(st),
        (lo, hi, n_lo, jnp.log(jnp.maximum(n_lo, 0.5)), jnp.full((1, QBLK), float(np.log(0.5)), F32),
         jnp.zeros((1, QBLK), F32)))
    lo, hi, n_lo = lax.fori_loop(0, UNCHECKED_HALVINGS, lambda _, st: halve(*st), (lo, hi, n_lo))
    lo, hi, n_lo, _, _ = lax.while_loop(lambda st: jnp.logical_and(st[3] > 0.0, st[4] < 17), checked_halve,
                                        (lo, hi, n_lo, any_open(lo, hi, n_lo), jnp.int32(0)))
    tops = top_ref[...]
    thr = jnp.min(_fold_rows(jnp.where(tops >= _key_to_float(lo), tops, jnp.inf), jnp.minimum), axis=0,
                  keepdims=True)
    n_ge = count_scores(lambda x: x >= thr)

    def next_above(v):
        def body(t, carry):
            low, cnt = carry
            ks = pl.multiple_of(t * KTILE, KTILE)
            x = sc_ref[pl.ds(ks, KTILE), :]
            above = x > v
            return (jnp.minimum(low, _fold_rows(jnp.where(above, x, jnp.inf), jnp.minimum)),
                    cnt + _fold_rows(jnp.where(above, 1.0, 0.0), jnp.add))
        low, cnt = _tile_loop(n_tiles, body, (jnp.full((SUBLANES, QBLK), jnp.inf, F32),
                                              jnp.zeros((SUBLANES, QBLK), F32)), 4)
        return jnp.min(low, axis=0, keepdims=True), jnp.sum(cnt, axis=0, keepdims=True)

    def any_climbing(n_ge, stuck):
        return jnp.max(jnp.where(jnp.logical_or(n_ge <= kk, stuck > 0.0), 0.0, 1.0))

    def climb(st):
        thr, n_ge, stuck, _, it = st
        nxt, n_nxt = next_above(thr)
        go = jnp.logical_and(jnp.logical_and(n_ge > kk, stuck == 0.0), n_nxt >= kk)
        stuck = jnp.where(jnp.logical_and(n_ge > kk, n_nxt < kk), 1.0, stuck)
        thr, n_ge = jnp.where(go, nxt, thr), jnp.where(go, n_nxt, n_ge)
        return thr, n_ge, stuck, any_climbing(n_ge, stuck), it + 1

    stuck = jnp.zeros((1, QBLK), F32)
    thr, n_ge, _, _, _ = lax.while_loop(
        lambda st: jnp.logical_and(st[3] > 0.0, st[4] <= n_tiles * KTILE), climb,
        (thr, n_ge, stuck, any_climbing(n_ge, stuck), jnp.int32(0)))
    has_ties = jnp.max(n_ge) > kk

    def bias_fast(t, carry, masked):
        for sub in range(KTILE // KSUB):
            ks = pl.multiple_of(t * KTILE + sub * KSUB, KSUB)
            keep = sc_ref[pl.ds(ks, KSUB), :] >= thr
            if masked:
                keep = jnp.logical_and(keep, visible(ks))
            bias_ref[pl.ds(ks, KSUB), :] = jnp.where(keep, 0.0, NEG_BIG).astype(BF16)
        return carry

    def bias_ties(t, run, masked, budget, tri):
        for sub in range(KTILE // KSUB):
            ks = pl.multiple_of(t * KTILE + sub * KSUB, KSUB)
            x = sc_ref[pl.ds(ks, KSUB), :]
            above = x > thr
            tie = x == thr
            tie_f = jnp.where(tie, 1.0, 0.0)
            rank = run + _dot(tri, tie_f.astype(BF16))
            keep = jnp.logical_or(above, jnp.logical_and(tie, rank <= budget))
            if masked:
                keep = jnp.logical_and(keep, visible(ks))
            bias_ref[pl.ds(ks, KSUB), :] = jnp.where(keep, 0.0, NEG_BIG).astype(BF16)
            run = run + jnp.sum(tie_f, axis=0, keepdims=True)
        return run

    @pl.when(jnp.logical_not(has_ties))
    def _():
        peeled(bias_fast, 0)

    @pl.when(has_ties)
    def _():
        budget = kk - count_scores(lambda x: x > thr)
        tri = (lax.broadcasted_iota(jnp.int32, (KSUB, KSUB), 1) <= key_row).astype(BF16)
        peeled(functools.partial(bias_ties, budget=budget, tri=tri), jnp.zeros((1, QBLK), F32))

    qt = qt_ref[...]
    eye = (lax.broadcasted_iota(jnp.int32, (QBLK, QBLK), 0)
           == lax.broadcasted_iota(jnp.int32, (QBLK, QBLK), 1)).astype(BF16)
    zero = jnp.zeros((DSA_HD, QBLK), BF16)
    n_pairs = DSA_HEADS // 2
    for c in range(n_pairs):
        qa = qt[(2 * c) * DSA_HD:(2 * c + 1) * DSA_HD, :]
        qb = qt[(2 * c + 1) * DSA_HD:(2 * c + 2) * DSA_HD, :]
        rhs_ref[c] = jnp.concatenate([jnp.concatenate([qa, zero], axis=1),
                                      jnp.concatenate([zero, qb], axis=1),
                                      jnp.concatenate([eye, eye], axis=1)], axis=0)
    acc_ref[...] = jnp.zeros(acc_ref.shape, F32)
    ones_rows = jnp.ones((ACC_ROWS - DSA_HD, ATILE), BF16)

    def qk_stage(ks, s_ref):
        bias = bias_ref[pl.ds(ks, ATILE), :]
        tile_max = []
        for c in range(n_pairs):
            lhs = jnp.concatenate([k_ref[pl.ds(ks, ATILE), c * 2 * DSA_HD:(c + 1) * 2 * DSA_HD], bias], axis=1)
            s = _dot(lhs, rhs_ref[c])
            s_ref[c] = s
            tile_max.append(_fold_rows(s, jnp.maximum))
        return tuple(tile_max)

    def pv_stage(ks, s_ref, tile_max, ms):
        new_ms = []
        for c in range(n_pairs):
            m_new = jnp.maximum(ms[c], jnp.max(tile_max[c], axis=0, keepdims=True))
            alpha = jnp.exp(ms[c] - m_new)
            p = jnp.exp(s_ref[c] - m_new).astype(BF16)
            for hh in range(2):
                h = 2 * c + hh
                rows = slice(h * ACC_ROWS, (h + 1) * ACC_ROWS)
                vt_aug = jnp.concatenate([vt_ref[h * DSA_HD:(h + 1) * DSA_HD, pl.ds(ks, ATILE)], ones_rows], axis=0)
                pv = _dot(vt_aug, p[:, hh * QBLK:(hh + 1) * QBLK])
                acc_ref[rows, :] = alpha[:, hh * QBLK:(hh + 1) * QBLK] * acc_ref[rows, :] + pv
            new_ms.append(m_new)
        return tuple(new_ms)

    def attn_tile(t, carry):
        max_a, ms = carry
        k0 = pl.multiple_of(t * KTILE, KTILE)
        k1 = pl.multiple_of(t * KTILE + ATILE, ATILE)
        k2 = pl.multiple_of(t * KTILE + KTILE, KTILE)
        max_b = qk_stage(k1, sb_ref)
        ms = pv_stage(k0, sa_ref, max_a, ms)
        max_a = qk_stage(k2, sa_ref)
        ms = pv_stage(k1, sb_ref, max_b, ms)
        return max_a, ms

    ms = tuple(jnp.full((1, 2 * QBLK), NEG_BIG, F32) for _ in range(n_pairs))
    max_a, ms = _tile_loop(last, attn_tile, (qk_stage(0, sa_ref), ms), (4, 2))
    k0 = pl.multiple_of(last * KTILE, KTILE)
    k1 = pl.multiple_of(last * KTILE + ATILE, ATILE)
    max_b = qk_stage(k1, sb_ref)
    ms = pv_stage(k0, sa_ref, max_a, ms)
    pv_stage(k1, sb_ref, max_b, ms)
    outs = []
    for h in range(DSA_HEADS):
        a = acc_ref[h * ACC_ROWS:(h + 1) * ACC_ROWS, :]
        outs.append(a[:DSA_HD, :] / a[DSA_HD:DSA_HD + 1, :])
    o_ref[...] = jnp.concatenate(outs, axis=0).T.astype(o_ref.dtype)


def _dsa_kernel(qt_ref, iqt_ref, iwt_ref, k_ref, vt_ref, ik_ref, o_ref, sc_ref, bias_ref, acc_ref, rhs_ref,
                sa_ref, sb_ref, top_ref, *, k_sel, l_tok):
    has_tokens = pl.program_id(0) * QBLK < l_tok

    @pl.when(has_tokens)
    def _():
        _dsa_block(qt_ref, iqt_ref, iwt_ref, k_ref, vt_ref, ik_ref, o_ref, sc_ref, bias_ref, acc_ref, rhs_ref,
                   sa_ref, sb_ref, top_ref, k_sel)

    @pl.when(jnp.logical_not(has_tokens))
    def _():
        o_ref[...] = jnp.zeros(o_ref.shape, o_ref.dtype)


def _dsa(dqt, dk, dvt, iqt, ik, iwt, k_sel, l_tok):
    lp = dk.shape[0]

    def col(height):
        return pl.BlockSpec((height, QBLK), lambda i: (0, i))

    return pl.pallas_call(
        functools.partial(_dsa_kernel, k_sel=k_sel, l_tok=l_tok),
        grid=(lp // QBLK,),
        in_specs=[col(DSA_WIDTH), col(IDX_WIDTH), col(LANES), _const_spec((lp, DSA_WIDTH)),
                  _const_spec((DSA_WIDTH, lp)), _const_spec((lp, IDX_HD))],
        out_specs=pl.BlockSpec((QBLK, DSA_WIDTH), lambda i: (i, 0)),
        out_shape=jax.ShapeDtypeStruct((lp, DSA_WIDTH), BF16),
        scratch_shapes=[pltpu.VMEM((lp, QBLK), F32), pltpu.VMEM((lp, QBLK), BF16),
                        pltpu.VMEM((DSA_HEADS * ACC_ROWS, QBLK), F32),
                        pltpu.VMEM((DSA_HEADS // 2, 2 * LANES, 2 * QBLK), BF16),
                        pltpu.VMEM((DSA_HEADS // 2, ATILE, 2 * QBLK), F32),
                        pltpu.VMEM((DSA_HEADS // 2, ATILE, 2 * QBLK), F32),
                        pltpu.VMEM((TOP_R * N_GROUPS, QBLK), F32)],
        compiler_params=pltpu.CompilerParams(dimension_semantics=("parallel",),
                                             vmem_limit_bytes=VMEM_LIMIT),
        name="dsa",
    )(dqt, iqt, iwt, dk, dvt, ik)


def _out_ffn2_kernel(h_ref, ht_ref, og_ref, ogt_ref, od_ref, odt_ref, wo_ref, norm_ref, wg_ref, wu_ref, wd_ref,
                     fnorm_ref, o_ref, *, ff_chunk):
    def shifted(main_ref, tail_ref):
        return jnp.concatenate([main_ref[N_META:, :], tail_ref[...]], axis=0)

    h2 = (shifted(h_ref, ht_ref) + _dot(shifted(og_ref, ogt_ref), wo_ref[:GLA_WIDTH, :])
          + _dot(shifted(od_ref, odt_ref), wo_ref[GLA_WIDTH:, :]))
    h3 = _swiglu_half(h2, norm_ref, wg_ref, wu_ref, wd_ref, ff_chunk)
    o_ref[...] = _rmsnorm(h3, fnorm_ref[...])


def _out_ffn2(h, o_gla, o_dsa, wo, norm, wg, wu, wd, fnorm, seq):
    lp, d = h.shape
    d_ff = wg.shape[1]
    n_steps = -(-seq // ROW_TILE)
    assert ROW_TILE % N_META == 0 and n_steps * ROW_TILE + N_META <= lp
    tail_step = ROW_TILE // N_META

    def row(width):
        return pl.BlockSpec((ROW_TILE, width), lambda i: (i, 0))

    def tail(width):
        return pl.BlockSpec((N_META, width), lambda i: ((i + 1) * tail_step, 0))

    return pl.pallas_call(
        functools.partial(_out_ffn2_kernel, ff_chunk=_ffn_chunk(d_ff)),
        grid=(n_steps,),
        in_specs=[row(d), tail(d), row(GLA_WIDTH), tail(GLA_WIDTH), row(DSA_WIDTH), tail(DSA_WIDTH),
                  _const_spec((GLA_WIDTH + DSA_WIDTH, d)), _const_spec((1, d)), _const_spec((d, d_ff)),
                  _const_spec((d, d_ff)), _const_spec((d_ff, d)), _const_spec((1, d))],
        out_specs=row(d),
        out_shape=jax.ShapeDtypeStruct((seq, d), F32),
        compiler_params=pltpu.CompilerParams(dimension_semantics=("parallel",),
                                             vmem_limit_bytes=VMEM_LIMIT),
        name="out_ffn2",
    )(h, h, o_gla, o_gla, o_dsa, o_dsa, wo, norm, wg, wu, wd, fnorm)


def _rope_tables(lp):
    pos = jnp.arange(lp, dtype=F32)
    inv_freq = 1.0 / (ROPE_THETA ** (jnp.arange(0, DSA_HD, 2, dtype=F32) / DSA_HD))
    ang = pos[:, None] * inv_freq[None, :]
    cos = jnp.cos(ang)
    sin = jnp.sin(ang)
    cos2 = jnp.concatenate([cos, cos, cos, cos], axis=-1)
    sin2s = jnp.concatenate([-sin, sin, -sin, sin], axis=-1)
    return cos2, sin2s


def _layer(h, meta, k_sel, l_tok, ffn1_norm, ffn1_w_gate, ffn1_w_up, ffn1_w_down, mix_norm, w_in, gla_w_a2, gla_b_a,
           gla_head_norm, w_out, ffn2_norm, ffn2_w_gate, ffn2_w_up, ffn2_w_down, final_norm, cos2, sin2s):
    d = h.shape[1]
    h1 = _ffn1(h, meta, ffn1_norm.reshape(1, d), ffn1_w_gate.astype(BF16), ffn1_w_up.astype(BF16),
               ffn1_w_down.astype(BF16))

    sizes = (GLA_QK, GLA_QK, GLA_WIDTH, GLA_WIDTH, GLA_LOWRANK, DSA_WIDTH, DSA_WIDTH, DSA_WIDTH, IDX_WIDTH,
             IDX_HD, IDX_HEADS)
    offs = np.concatenate([[0], np.cumsum(sizes)])
    w_bf = w_in.astype(BF16)
    w_all = jnp.concatenate([w_bf[:, :offs[4]], w_bf[:, offs[5]:], w_bf[:, offs[4]:offs[5]],
                             jnp.zeros((d, LANES - IDX_HD - IDX_HEADS - GLA_LOWRANK), BF16)], axis=1)
    wa2p = jnp.zeros((LANES, GLA_QK), F32).at[MISC_GA:MISC_GA + GLA_LOWRANK, :].set(gla_w_a2).astype(BF16)

    gqk, gv, gr, gg, dqt, dk, dvt, iqt, ik, iwt = _in_proj(
        h1, mix_norm.reshape(1, d), w_all, wa2p, gla_b_a.reshape(1, GLA_QK), cos2, sin2s)
    o_gla = _gla(gqk, gv, gr, gg, gla_head_norm.reshape(1, GLA_DV))
    o_dsa = _dsa(dqt, dk, dvt, iqt, ik, iwt, k_sel, l_tok)
    return _out_ffn2(h1, o_gla, o_dsa, w_out.astype(BF16), ffn2_norm.reshape(1, d), ffn2_w_gate.astype(BF16),
                     ffn2_w_up.astype(BF16), ffn2_w_down.astype(BF16), final_norm.reshape(1, d), l_tok - N_META)


def kernel(x, meta_tokens, ffn1_norm, ffn1_w_gate, ffn1_w_up, ffn1_w_down, mix_norm, w_in, gla_w_a2, gla_b_a,
           gla_head_norm, w_out, ffn2_norm, ffn2_w_gate, ffn2_w_up, ffn2_w_down, final_norm):
    batch, seq, d = x.shape
    depth = ffn1_norm.shape[0]
    assert depth == 1, "the final norm is fused into the layer's last kernel"
    k_sel = min(TOPK_MAX, seq // 4)
    l_tok = seq + N_META
    lp = -(-max(l_tok, -(-seq // ROW_TILE) * ROW_TILE + N_META) // ROW_PAD) * ROW_PAD
    cos2, sin2s = _rope_tables(lp)
    outs = []
    for bi in range(batch):
        h = jnp.pad(x[bi], ((N_META, lp - l_tok), (0, 0)))
        outs.append(_layer(h, meta_tokens.astype(x.dtype), k_sel, l_tok, ffn1_norm[0], ffn1_w_gate[0], ffn1_w_up[0], ffn1_w_down[0],
                           mix_norm[0], w_in[0], gla_w_a2[0], gla_b_a[0], gla_head_norm[0], w_out[0], ffn2_norm[0],
                           ffn2_w_gate[0], ffn2_w_up[0], ffn2_w_down[0], final_norm, cos2, sin2s))
    return jnp.stack(outs, axis=0)
```

```python
import functools

import numpy as np
import jax
import jax.numpy as jnp
from jax import lax
from jax.experimental import pallas as pl
from jax.experimental.pallas import tpu as pltpu

N_META = 16
EPS = 1e-6
ROPE_THETA = 10000.0
GLA_HEADS = 4
GLA_DK = 64
GLA_DV = 128
GLA_LOWRANK = 16
GLA_GATE_NORM = 16.0
DSA_HEADS = 8
DSA_HD = 64
IDX_HEADS = 8
IDX_HD = 64
TOPK_MAX = 256

GLA_QK = GLA_HEADS * GLA_DK
GLA_WIDTH = GLA_HEADS * GLA_DV
DSA_WIDTH = DSA_HEADS * DSA_HD
IDX_WIDTH = IDX_HEADS * IDX_HD

LANES = 128
SUBLANES = 8
ROW_TILE = 384
QBLK = 128
KSUB = 128
KTILE = 512
ATILE = KTILE // 2
N_GROUPS = 2 * KSUB
TOP_R = 6
INTERPOLATED_PROBES = 8
UNCHECKED_HALVINGS = 4
GLA_STEP = 128
GLA_SUB = 16
ROW_PAD = 1536
ACC_ROWS = DSA_HD + 16
VMEM_LIMIT = 60 * 1024 * 1024
NEG_BIG = -1e30

MISC_IK = 0
MISC_IW = IDX_HD
MISC_GA = IDX_HD + IDX_HEADS

F32 = jnp.float32
BF16 = jnp.bfloat16


def _dot(a, b):
    return jnp.dot(a, b, preferred_element_type=F32)


def _rmsnorm(x, g):
    return x * lax.rsqrt(jnp.mean(x * x, axis=-1, keepdims=True) + EPS) * g


def _const_spec(shape):
    return pl.BlockSpec(shape, lambda *_: (0,) * len(shape), pipeline_mode=pl.Buffered(1))


def _swiglu_half(x, norm_ref, wg_ref, wu_ref, wd_ref, ff_chunk):
    hn = _rmsnorm(x, norm_ref[...]).astype(BF16)
    d_ff = wg_ref.shape[1]
    acc = jnp.zeros(x.shape, F32)
    for c in range(d_ff // ff_chunk):
        sl = slice(c * ff_chunk, (c + 1) * ff_chunk)
        g = _dot(hn, wg_ref[:, sl])
        u = _dot(hn, wu_ref[:, sl])
        a = (g * jax.nn.sigmoid(g) * u).astype(BF16)
        acc = acc + _dot(a, wd_ref[sl, :])
    return x + 0.5 * acc


def _ffn1_kernel(h_ref, meta_ref, norm_ref, wg_ref, wu_ref, wd_ref, o_ref, *, ff_chunk):
    x = h_ref[...]
    rows = lax.broadcasted_iota(jnp.int32, (x.shape[0], 1), 0)
    meta = jnp.concatenate([meta_ref[...], jnp.zeros((x.shape[0] - N_META, x.shape[1]), x.dtype)], axis=0)
    x = jnp.where(jnp.logical_and(pl.program_id(0) == 0, rows < N_META), meta, x)
    o_ref[...] = _swiglu_half(x, norm_ref, wg_ref, wu_ref, wd_ref, ff_chunk)


def _ffn_chunk(d_ff):
    if d_ff <= 4096:
        return d_ff
    for c in (2048, 1024, 512, 256):
        if d_ff % c == 0:
            return c
    return d_ff


def _ffn1(h, meta, norm, wg, wu, wd):
    lp, d = h.shape
    d_ff = wg.shape[1]
    row = pl.BlockSpec((ROW_TILE, d), lambda i: (i, 0))
    return pl.pallas_call(
        functools.partial(_ffn1_kernel, ff_chunk=_ffn_chunk(d_ff)),
        grid=(lp // ROW_TILE,),
        in_specs=[row, _const_spec((N_META, d)), _const_spec((1, d)), _const_spec((d, d_ff)),
                  _const_spec((d, d_ff)), _const_spec((d_ff, d))],
        out_specs=row,
        out_shape=jax.ShapeDtypeStruct((lp, d), F32),
        compiler_params=pltpu.CompilerParams(dimension_semantics=("parallel",),
                                             vmem_limit_bytes=VMEM_LIMIT),
        name="ffn1",
    )(h, meta, norm, wg, wu, wd)


def _rope(x, cos, sin_signed, first_half):
    parts = []
    for c in range(x.shape[1] // LANES):
        xc = x[:, c * LANES:(c + 1) * LANES]
        rot = jnp.where(first_half, pltpu.roll(xc, LANES - 32, 1), pltpu.roll(xc, 32, 1))
        parts.append(xc * cos + rot * sin_signed)
    return parts[0] if len(parts) == 1 else jnp.concatenate(parts, axis=1)


def _log_sigmoid(x):
    return jnp.minimum(x, 0.0) - jnp.log(1.0 + jnp.exp(-jnp.abs(x)))


def _in_proj_kernel(h_ref, norm_ref, w_ref, wa2_ref, ba_ref, cos_ref, sin_ref, tcos_ref, tsin_ref,
                    gqk_ref, gv_ref, gr_ref, gg_ref, dqt_ref, dk_ref, dvt_ref, iqt_ref, ik_ref, iwt_ref):
    hn = _rmsnorm(h_ref[...], norm_ref[...]).astype(BF16)
    spread = functools.partial(jnp.dot, preferred_element_type=F32, precision=lax.Precision.HIGHEST)
    cos = spread(cos_ref[...], tcos_ref[...])
    sin_s = spread(sin_ref[...], tsin_ref[...])
    lane = lax.broadcasted_iota(jnp.int32, cos.shape, 1)
    first_half = (lane % DSA_HD) < (DSA_HD // 2)

    def proj(c0, width):
        return _dot(hn, w_ref[:, c0:c0 + width])

    c = 0
    gqk_ref[...] = proj(c, 2 * GLA_QK); c += 2 * GLA_QK
    gv_ref[...] = proj(c, GLA_WIDTH); c += GLA_WIDTH
    gr_ref[...] = proj(c, GLA_WIDTH); c += GLA_WIDTH
    dqt_ref[...] = (_rope(proj(c, DSA_WIDTH), cos, sin_s, first_half) * (DSA_HD ** -0.5)).T.astype(BF16)
    c += DSA_WIDTH
    dk_ref[...] = _rope(proj(c, DSA_WIDTH), cos, sin_s, first_half).astype(BF16); c += DSA_WIDTH
    dvt_ref[...] = proj(c, DSA_WIDTH).T.astype(BF16); c += DSA_WIDTH
    iqt_ref[...] = (_rope(proj(c, IDX_WIDTH), cos, sin_s, first_half) * (IDX_HD ** -0.5)).T.astype(BF16)
    c += IDX_WIDTH
    misc = proj(c, LANES)
    ik_ref[...] = _rope(misc, cos, sin_s, first_half)[:, MISC_IK:MISC_IK + IDX_HD].astype(BF16)
    iwt_ref[...] = (misc * (IDX_HEADS ** -0.5)).T
    pre = _dot(misc.astype(BF16), wa2_ref[...]) + ba_ref[...]
    gg_ref[...] = _log_sigmoid(pre) * (1.0 / GLA_GATE_NORM)


def _rope_spread():
    half = DSA_HD // 2
    lane = np.arange(LANES)
    hit = (lane[None, :] % half) == np.arange(half)[:, None]
    sign = np.where((lane % DSA_HD) < half, -1.0, 1.0)[None, :]
    return jnp.asarray(hit, F32), jnp.asarray(hit * sign, F32)


def _in_proj(h, norm, w_all, wa2p, ba, cos, sin):
    lp, d = h.shape
    ncols = w_all.shape[1]
    half = DSA_HD // 2
    tcos, tsin = _rope_spread()

    def row(width):
        return pl.BlockSpec((ROW_TILE, width), lambda i: (i, 0))

    def col(height):
        return pl.BlockSpec((height, ROW_TILE), lambda i: (0, i))

    outs = [
        (row(2 * GLA_QK), (lp, 2 * GLA_QK), F32), (row(GLA_WIDTH), (lp, GLA_WIDTH), F32),
        (row(GLA_WIDTH), (lp, GLA_WIDTH), F32), (row(GLA_QK), (lp, GLA_QK), F32),
        (col(DSA_WIDTH), (DSA_WIDTH, lp), BF16), (row(DSA_WIDTH), (lp, DSA_WIDTH), BF16),
        (col(DSA_WIDTH), (DSA_WIDTH, lp), BF16), (col(IDX_WIDTH), (IDX_WIDTH, lp), BF16),
        (row(IDX_HD), (lp, IDX_HD), BF16), (col(LANES), (LANES, lp), F32),
    ]
    return pl.pallas_call(
        _in_proj_kernel,
        grid=(lp // ROW_TILE,),
        in_specs=[row(d), _const_spec((1, d)), _const_spec((d, ncols)), _const_spec((LANES, GLA_QK)),
                  _const_spec((1, GLA_QK)), row(half), row(half), _const_spec((half, LANES)),
                  _const_spec((half, LANES))],
        out_specs=[spec for spec, _, _ in outs],
        out_shape=[jax.ShapeDtypeStruct(shape, dt) for _, shape, dt in outs],
        compiler_params=pltpu.CompilerParams(dimension_semantics=("parallel",),
                                             vmem_limit_bytes=VMEM_LIMIT),
        name="in_proj",
    )(h, norm, w_all, wa2p, ba, cos, sin, tcos, tsin)


def _gla_kernel(qk_ref, v_ref, r_ref, g_ref, hnorm_ref, esel_ref, o_ref, s_ref):
    n_sub = GLA_STEP // GLA_SUB

    @pl.when(pl.program_id(0) == 0)
    def _():
        s_ref[...] = jnp.zeros(s_ref.shape, F32)

    q = qk_ref[:, :GLA_QK] * (GLA_DK ** -0.5)
    k = qk_ref[:, GLA_QK:]
    v = v_ref[...]
    g = g_ref[...]

    ri = lax.broadcasted_iota(jnp.int32, (GLA_STEP, GLA_STEP), 0)
    ci = lax.broadcasted_iota(jnp.int32, (GLA_STEP, GLA_STEP), 1)
    tri = ((ri // GLA_SUB == ci // GLA_SUB) & (ci <= ri)).astype(F32)
    b = jnp.dot(tri, g, preferred_element_type=F32, precision=lax.Precision.HIGHEST)

    b3 = b.reshape(n_sub, GLA_SUB, GLA_QK)
    k3 = k.reshape(n_sub, GLA_SUB, GLA_QK)
    b_last = jnp.broadcast_to(b3[:, GLA_SUB - 1:GLA_SUB, :], b3.shape).reshape(GLA_STEP, GLA_QK)

    pos = lax.broadcasted_iota(jnp.int32, (GLA_STEP, GLA_QK), 0) % GLA_SUB
    t_cols = []
    for j in range(GLA_SUB):
        kj = jnp.broadcast_to(k3[:, j:j + 1, :], k3.shape).reshape(GLA_STEP, GLA_QK)
        bj = jnp.broadcast_to(b3[:, j:j + 1, :], b3.shape).reshape(GLA_STEP, GLA_QK)
        e = jnp.where(pos >= j, b - bj, -jnp.inf)
        t_cols.append((q * kj * jnp.exp(e)).astype(BF16))
    a = _dot(jnp.concatenate(t_cols, axis=1), esel_ref[...]).astype(BF16)
    head_of_lane = lax.broadcasted_iota(jnp.int32, (GLA_SUB, GLA_WIDTH), 1) // GLA_DV
    o_intra = []
    for c in range(n_sub):
        rows = slice(c * GLA_SUB, (c + 1) * GLA_SUB)
        v_c = v[rows, :]
        v_heads = jnp.concatenate([jnp.where(head_of_lane == h, v_c, 0.0) for h in range(GLA_HEADS)]
                                  + [jnp.zeros((LANES - GLA_HEADS * GLA_SUB, GLA_WIDTH), F32)], axis=0)
        o_intra.append(_dot(a[rows, :], v_heads.astype(BF16)))
    o = jnp.concatenate(o_intra, axis=0)

    qe = q * jnp.exp(b)
    kd = k * jnp.exp(b_last - b)
    kd_t = kd.T
    dec_t = jnp.exp(b).T
    lane_head = lax.broadcasted_iota(jnp.int32, (GLA_SUB, GLA_QK), 1) // GLA_DK
    s = s_ref[...]
    o_inter = []
    for c in range(n_sub):
        rows = slice(c * GLA_SUB, (c + 1) * GLA_SUB)
        qe_c = qe[rows, :]
        q_stack = jnp.concatenate(
            [jnp.where(lane_head == h, qe_c, 0.0) for h in range(GLA_HEADS)], axis=0).astype(BF16)
        r_stack = _dot(q_stack, s.astype(BF16))
        o_inter.append(jnp.concatenate(
            [r_stack[h * GLA_SUB:(h + 1) * GLA_SUB, :] for h in range(GLA_HEADS)], axis=1))
        kd_c = kd_t[:, rows].astype(BF16)
        v_c = v[rows, :].astype(BF16)
        u_c = jnp.concatenate(
            [_dot(kd_c[h * GLA_DK:(h + 1) * GLA_DK, :], v_c[:, h * GLA_DV:(h + 1) * GLA_DV])
             for h in range(GLA_HEADS)], axis=0)
        last = c * GLA_SUB + GLA_SUB - 1
        s = dec_t[:, last:last + 1] * s + u_c
    s_ref[...] = s
    o = o + jnp.concatenate(o_inter, axis=0)

    r = r_ref[...]
    gate = r * jax.nn.sigmoid(r)
    hn = hnorm_ref[...]
    outs = []
    for h in range(GLA_HEADS):
        oh = o[:, h * GLA_DV:(h + 1) * GLA_DV]
        outs.append(_rmsnorm(oh, hn))
    o_ref[...] = (jnp.concatenate(outs, axis=1) * gate).astype(o_ref.dtype)


def _gla_head_selector():
    r = np.arange(GLA_SUB * GLA_QK)
    j, h = r // GLA_QK, (r % GLA_QK) // GLA_DK
    sel = np.zeros((GLA_SUB * GLA_QK, LANES), np.float32)
    sel[r, h * GLA_SUB + j] = 1.0
    return jnp.asarray(sel, BF16)


def _gla(gqk, gv, gr, gg, hnorm):
    lp = gqk.shape[0]
    assert GLA_HEADS * GLA_SUB <= LANES

    def row(width):
        return pl.BlockSpec((GLA_STEP, width), lambda i: (i, 0))

    return pl.pallas_call(
        _gla_kernel,
        grid=(lp // GLA_STEP,),
        in_specs=[row(2 * GLA_QK), row(GLA_WIDTH), row(GLA_WIDTH), row(GLA_QK), _const_spec((1, GLA_DV)),
                  _const_spec((GLA_SUB * GLA_QK, LANES))],
        out_specs=row(GLA_WIDTH),
        out_shape=jax.ShapeDtypeStruct((lp, GLA_WIDTH), BF16),
        scratch_shapes=[pltpu.VMEM((GLA_QK, GLA_DV), F32)],
        compiler_params=pltpu.CompilerParams(dimension_semantics=("arbitrary",),
                                             vmem_limit_bytes=VMEM_LIMIT),
        name="gla",
    )(gqk, gv, gr, gg, hnorm, _gla_head_selector())


def _key_to_float(u):
    bits = jnp.where(u >= 0, u, u ^ jnp.int32(0x7FFFFFFF))
    return lax.bitcast_convert_type(bits, F32)


def _float_to_key(f):
    bits = lax.bitcast_convert_type(f, jnp.int32)
    return jnp.where(bits >= 0, bits, bits ^ jnp.int32(0x7FFFFFFF))


_KEY_NEG_INF = np.int32(np.uint32(0xFF800000) ^ np.uint32(0x7FFFFFFF))


def _tile_loop(n, body, carry, unroll):
    start = 0
    for width in ((unroll,) if isinstance(unroll, int) else unroll):
        trips = (n - start) // width

        def group(u, c, width=width, start=start):
            for r in range(width):
                c = body(start + u * width + r, c)
            return c

        carry = lax.fori_loop(0, trips, group, carry)
        start = start + trips * width
    return lax.fori_loop(start, n, body, carry)


def _fold_rows(x, op):
    parts = [x[r:r + SUBLANES, :] for r in range(0, x.shape[0], SUBLANES)]
    while len(parts) > 1:
        parts = [op(parts[a], parts[a + 1]) if a + 1 < len(parts) else parts[a] for a in range(0, len(parts), 2)]
    return parts[0]


def _dsa_block(qt_ref, iqt_ref, iwt_ref, k_ref, vt_ref, ik_ref, o_ref, sc_ref, bias_ref, acc_ref, rhs_ref,
               sa_ref, sb_ref, top_ref, k_sel):
    i = pl.program_id(0)
    n_tiles = (i * QBLK + QBLK + KTILE - 1) // KTILE
    last = n_tiles - 1
    q_pos = i * QBLK + lax.broadcasted_iota(jnp.int32, (KSUB, QBLK), 1)
    key_row = lax.broadcasted_iota(jnp.int32, (KSUB, QBLK), 0)

    def visible(ks):
        return (ks + key_row) <= q_pos

    def peeled(body, carry):
        carry = _tile_loop(last, functools.partial(body, masked=False), carry, (4, 2))
        return body(last, carry, masked=True)

    iqt = iqt_ref[...]
    iq_all = jnp.concatenate([iqt[h * IDX_HD:(h + 1) * IDX_HD, :] for h in range(IDX_HEADS)], axis=1)
    iw = iwt_ref[MISC_IW:MISC_IW + IDX_HEADS, :]
    top_ref[...] = jnp.full(top_ref.shape, -jnp.inf, F32)

    def score_tile(t, carry, masked):
        for sub in range(KTILE // KSUB):
            ks = pl.multiple_of(t * KTILE + sub * KSUB, KSUB)
            lg = _dot(ik_ref[pl.ds(ks, KSUB), :], iq_all)
            sc = iw[0:1, :] * jnp.maximum(lg[:, 0:QBLK], 0.0)
            for h in range(1, IDX_HEADS):
                sc = sc + iw[h:h + 1, :] * jnp.maximum(lg[:, h * QBLK:(h + 1) * QBLK], 0.0)
            if masked:
                sc = jnp.where(visible(ks), sc, -jnp.inf)
            sc_ref[pl.ds(ks, KSUB), :] = sc
            rest = sc
            for r in range(TOP_R):
                rows = slice(r * N_GROUPS + (sub % 2) * KSUB, r * N_GROUPS + (sub % 2 + 1) * KSUB)
                held = top_ref[rows, :]
                top_ref[rows, :] = jnp.maximum(held, rest)
                if r + 1 < TOP_R:
                    rest = jnp.minimum(held, rest)
        return carry

    peeled(score_tile, 0)

    kk = float(k_sel)

    def lane_sum(x):
        return jnp.sum(_fold_rows(x, jnp.add), axis=0, keepdims=True)

    def count_scores(pred):
        def body(t, acc):
            ks = pl.multiple_of(t * KTILE, KTILE)
            return acc + _fold_rows(jnp.where(pred(sc_ref[pl.ds(ks, KTILE), :]), 1.0, 0.0), jnp.add)
        return jnp.sum(_tile_loop(n_tiles, body, jnp.zeros((SUBLANES, QBLK), F32), 4), axis=0, keepdims=True)

    def count_ge(cand):
        return lane_sum(jnp.where(top_ref[...] >= cand, 1.0, 0.0))

    best = top_ref[0:N_GROUPS, :]
    lo = jnp.maximum(_float_to_key(jnp.min(_fold_rows(best, jnp.minimum), axis=0, keepdims=True)), _KEY_NEG_INF)
    hi = _float_to_key(jnp.max(_fold_rows(best, jnp.maximum), axis=0, keepdims=True))
    n_lo = count_ge(_key_to_float(lo))

    def settled(lo, hi, n_lo):
        return jnp.logical_or(n_lo <= kk, lo >= hi)

    def any_open(lo, hi, n_lo):
        return jnp.max(jnp.where(settled(lo, hi, n_lo), 0.0, 1.0))

    def midpoint(lo, hi):
        return (lo | hi) - ((lo ^ hi) >> 1)

    def probe_at(key, lo, hi, n_lo):
        open_ = jnp.logical_not(settled(lo, hi, n_lo))
        cnt = count_ge(_key_to_float(key))
        up = jnp.logical_and(open_, cnt >= kk)
        down = jnp.logical_and(open_, cnt < kk)
        return jnp.where(up, key, lo), jnp.where(down, key - 1, hi), jnp.where(up, cnt, n_lo), cnt, up, down

    def halve(lo, hi, n_lo):
        return probe_at(midpoint(lo, hi), lo, hi, n_lo)[:3]

    log_k = float(np.log(k_sel))

    def interpolate(st):
        lo, hi, n_lo, g_lo, g_hi, side = st
        f_lo, f_hi = _key_to_float(lo), _key_to_float(hi)
        guess = f_lo + (g_lo - log_k) / jnp.maximum(g_lo - g_hi, 1e-9) * (f_hi - f_lo)
        usable = jnp.logical_and(lo > _KEY_NEG_INF, jnp.abs(guess) < jnp.inf)
        key = jnp.where(usable, jnp.clip(_float_to_key(guess), lo + 1, hi), midpoint(lo, hi))
        lo, hi, n_lo, cnt, up, down = probe_at(key, lo, hi, n_lo)
        g_cnt = jnp.log(jnp.maximum(cnt, 0.5))
        g_hi = jnp.where(down, g_cnt, jnp.where(jnp.logical_and(up, side > 0.0), 0.5 * (g_hi + log_k), g_hi))
        g_lo = jnp.where(up, g_cnt, jnp.where(jnp.logical_and(down, side < 0.0), 0.5 * (g_lo + log_k), g_lo))
        side = jnp.where(up, 1.0, jnp.where(down, -1.0, side))
        return lo, hi, n_lo, g_lo, g_hi, side

    def checked_halve(st):
        lo, hi, n_lo = halve(*halve(*st[:3]))
        return lo, hi, n_lo, any_open(lo, hi, n_lo), st[4] + 1

    lo, hi, n_lo, _, _, _ = lax.fori_loop(
        0, INTERPOLATED_PROBES, lambda _, st: interpolate(st),
        (lo, hi, n_lo, jnp.log(jnp.maximum(n_lo, 0.5)), jnp.full((1, QBLK), float(np.log(0.5)), F32),
         jnp.zeros((1, QBLK), F32)))
    lo, hi, n_lo = lax.fori_loop(0, UNCHECKED_HALVINGS, lambda _, st: halve(*st), (lo, hi, n_lo))
    lo, hi, n_lo, _, _ = lax.while_loop(lambda st: jnp.logical_and(st[3] > 0.0, st[4] < 17), checked_halve,
                                        (lo, hi, n_lo, any_open(lo, hi, n_lo), jnp.int32(0)))
    tops = top_ref[...]
    thr = jnp.min(_fold_rows(jnp.where(tops >= _key_to_float(lo), tops, jnp.inf), jnp.minimum), axis=0,
                  keepdims=True)
    n_ge = count_scores(lambda x: x >= thr)

    def next_above(v):
        def body(t, carry):
            low, cnt = carry
            ks = pl.multiple_of(t * KTILE, KTILE)
            x = sc_ref[pl.ds(ks, KTILE), :]
            above = x > v
            return (jnp.minimum(low, _fold_rows(jnp.where(above, x, jnp.inf), jnp.minimum)),
                    cnt + _fold_rows(jnp.where(above, 1.0, 0.0), jnp.add))
        low, cnt = _tile_loop(n_tiles, body, (jnp.full((SUBLANES, QBLK), jnp.inf, F32),
                                              jnp.zeros((SUBLANES, QBLK), F32)), 4)
        return jnp.min(low, axis=0, keepdims=True), jnp.sum(cnt, axis=0, keepdims=True)

    def any_climbing(n_ge, stuck):
        return jnp.max(jnp.where(jnp.logical_or(n_ge <= kk, stuck > 0.0), 0.0, 1.0))

    def climb(st):
        thr, n_ge, stuck, _, it = st
        nxt, n_nxt = next_above(thr)
        go = jnp.logical_and(jnp.logical_and(n_ge > kk, stuck == 0.0), n_nxt >= kk)
        stuck = jnp.where(jnp.logical_and(n_ge > kk, n_nxt < kk), 1.0, stuck)
        thr, n_ge = jnp.where(go, nxt, thr), jnp.where(go, n_nxt, n_ge)
        return thr, n_ge, stuck, any_climbing(n_ge, stuck), it + 1

    stuck = jnp.zeros((1, QBLK), F32)
    thr, n_ge, _, _, _ = lax.while_loop(
        lambda st: jnp.logical_and(st[3] > 0.0, st[4] <= n_tiles * KTILE), climb,
        (thr, n_ge, stuck, any_climbing(n_ge, stuck), jnp.int32(0)))
    has_ties = jnp.max(n_ge) > kk

    def bias_fast(t, carry, masked):
        for sub in range(KTILE // KSUB):
            ks = pl.multiple_of(t * KTILE + sub * KSUB, KSUB)
            keep = sc_ref[pl.ds(ks, KSUB), :] >= thr
            if masked:
                keep = jnp.logical_and(keep, visible(ks))
            bias_ref[pl.ds(ks, KSUB), :] = jnp.where(keep, 0.0, NEG_BIG).astype(BF16)
        return carry

    def bias_ties(t, run, masked, budget, tri):
        for sub in range(KTILE // KSUB):
            ks = pl.multiple_of(t * KTILE + sub * KSUB, KSUB)
            x = sc_ref[pl.ds(ks, KSUB), :]
            above = x > thr
            tie = x == thr
            tie_f = jnp.where(tie, 1.0, 0.0)
            rank = run + _dot(tri, tie_f.astype(BF16))
            keep = jnp.logical_or(above, jnp.logical_and(tie, rank <= budget))
            if masked:
                keep = jnp.logical_and(keep, visible(ks))
            bias_ref[pl.ds(ks, KSUB), :] = jnp.where(keep, 0.0, NEG_BIG).astype(BF16)
            run = run + jnp.sum(tie_f, axis=0, keepdims=True)
        return run

    @pl.when(jnp.logical_not(has_ties))
    def _():
        peeled(bias_fast, 0)

    @pl.when(has_ties)
    def _():
        budget = kk - count_scores(lambda x: x > thr)
        tri = (lax.broadcasted_iota(jnp.int32, (KSUB, KSUB), 1) <= key_row).astype(BF16)
        peeled(functools.partial(bias_ties, budget=budget, tri=tri), jnp.zeros((1, QBLK), F32))

    qt = qt_ref[...]
    eye = (lax.broadcasted_iota(jnp.int32, (QBLK, QBLK), 0)
           == lax.broadcasted_iota(jnp.int32, (QBLK, QBLK), 1)).astype(BF16)
    zero = jnp.zeros((DSA_HD, QBLK), BF16)
    n_pairs = DSA_HEADS // 2
    for c in range(n_pairs):
        qa = qt[(2 * c) * DSA_HD:(2 * c + 1) * DSA_HD, :]
        qb = qt[(2 * c + 1) * DSA_HD:(2 * c + 2) * DSA_HD, :]
        rhs_ref[c] = jnp.concatenate([jnp.concatenate([qa, zero], axis=1),
                                      jnp.concatenate([zero, qb], axis=1),
                                      jnp.concatenate([eye, eye], axis=1)], axis=0)
    acc_ref[...] = jnp.zeros(acc_ref.shape, F32)
    ones_rows = jnp.ones((ACC_ROWS - DSA_HD, ATILE), BF16)

    def qk_stage(ks, s_ref):
        bias = bias_ref[pl.ds(ks, ATILE), :]
        tile_max = []
        for c in range(n_pairs):
            lhs = jnp.concatenate([k_ref[pl.ds(ks, ATILE), c * 2 * DSA_HD:(c + 1) * 2 * DSA_HD], bias], axis=1)
            s = _dot(lhs, rhs_ref[c])
            s_ref[c] = s
            tile_max.append(_fold_rows(s, jnp.maximum))
        return tuple(tile_max)

    def pv_stage(ks, s_ref, tile_max, ms):
        new_ms = []
        for c in range(n_pairs):
            m_new = jnp.maximum(ms[c], jnp.max(tile_max[c], axis=0, keepdims=True))
            alpha = jnp.exp(ms[c] - m_new)
            p = jnp.exp(s_ref[c] - m_new).astype(BF16)
            for hh in range(2):
                h = 2 * c + hh
                rows = slice(h * ACC_ROWS, (h + 1) * ACC_ROWS)
                vt_aug = jnp.concatenate([vt_ref[h * DSA_HD:(h + 1) * DSA_HD, pl.ds(ks, ATILE)], ones_rows], axis=0)
                pv = _dot(vt_aug, p[:, hh * QBLK:(hh + 1) * QBLK])
                acc_ref[rows, :] = alpha[:, hh * QBLK:(hh + 1) * QBLK] * acc_ref[rows, :] + pv
            new_ms.append(m_new)
        return tuple(new_ms)

    def attn_tile(t, carry):
        max_a, ms = carry
        k0 = pl.multiple_of(t * KTILE, KTILE)
        k1 = pl.multiple_of(t * KTILE + ATILE, ATILE)
        k2 = pl.multiple_of(t * KTILE + KTILE, KTILE)
        max_b = qk_stage(k1, sb_ref)
        ms = pv_stage(k0, sa_ref, max_a, ms)
        max_a = qk_stage(k2, sa_ref)
        ms = pv_stage(k1, sb_ref, max_b, ms)
        return max_a, ms

    ms = tuple(jnp.full((1, 2 * QBLK), NEG_BIG, F32) for _ in range(n_pairs))
    max_a, ms = _tile_loop(last, attn_tile, (qk_stage(0, sa_ref), ms), (4, 2))
    k0 = pl.multiple_of(last * KTILE, KTILE)
    k1 = pl.multiple_of(last * KTILE + ATILE, ATILE)
    max_b = qk_stage(k1, sb_ref)
    ms = pv_stage(k0, sa_ref, max_a, ms)
    pv_stage(k1, sb_ref, max_b, ms)
    outs = []
    for h in range(DSA_HEADS):
        a = acc_ref[h * ACC_ROWS:(h + 1) * ACC_ROWS, :]
        outs.append(a[:DSA_HD, :] / a[DSA_HD:DSA_HD + 1, :])
    o_ref[...] = jnp.concatenate(outs, axis=0).T.astype(o_ref.dtype)


def _dsa_kernel(qt_ref, iqt_ref, iwt_ref, k_ref, vt_ref, ik_ref, o_ref, sc_ref, bias_ref, acc_ref, rhs_ref,
                sa_ref, sb_ref, top_ref, *, k_sel, l_tok):
    has_tokens = pl.program_id(0) * QBLK < l_tok

    @pl.when(has_tokens)
    def _():
        _dsa_block(qt_ref, iqt_ref, iwt_ref, k_ref, vt_ref, ik_ref, o_ref, sc_ref, bias_ref, acc_ref, rhs_ref,
                   sa_ref, sb_ref, top_ref, k_sel)

    @pl.when(jnp.logical_not(has_tokens))
    def _():
        o_ref[...] = jnp.zeros(o_ref.shape, o_ref.dtype)


def _dsa(dqt, dk, dvt, iqt, ik, iwt, k_sel, l_tok):
    lp = dk.shape[0]

    def col(height):
        return pl.BlockSpec((height, QBLK), lambda i: (0, i))

    return pl.pallas_call(
        functools.partial(_dsa_kernel, k_sel=k_sel, l_tok=l_tok),
        grid=(lp // QBLK,),
        in_specs=[col(DSA_WIDTH), col(IDX_WIDTH), col(LANES), _const_spec((lp, DSA_WIDTH)),
                  _const_spec((DSA_WIDTH, lp)), _const_spec((lp, IDX_HD))],
        out_specs=pl.BlockSpec((QBLK, DSA_WIDTH), lambda i: (i, 0)),
        out_shape=jax.ShapeDtypeStruct((lp, DSA_WIDTH), BF16),
        scratch_shapes=[pltpu.VMEM((lp, QBLK), F32), pltpu.VMEM((lp, QBLK), BF16),
                        pltpu.VMEM((DSA_HEADS * ACC_ROWS, QBLK), F32),
                        pltpu.VMEM((DSA_HEADS // 2, 2 * LANES, 2 * QBLK), BF16),
                        pltpu.VMEM((DSA_HEADS // 2, ATILE, 2 * QBLK), F32),
                        pltpu.VMEM((DSA_HEADS // 2, ATILE, 2 * QBLK), F32),
                        pltpu.VMEM((TOP_R * N_GROUPS, QBLK), F32)],
        compiler_params=pltpu.CompilerParams(dimension_semantics=("parallel",),
                                             vmem_limit_bytes=VMEM_LIMIT),
        name="dsa",
    )(dqt, iqt, iwt, dk, dvt, ik)


def _out_ffn2_kernel(h_ref, ht_ref, og_ref, ogt_ref, od_ref, odt_ref, wo_ref, norm_ref, wg_ref, wu_ref, wd_ref,
                     fnorm_ref, o_ref, *, ff_chunk):
    def shifted(main_ref, tail_ref):
        return jnp.concatenate([main_ref[N_META:, :], tail_ref[...]], axis=0)

    h2 = (shifted(h_ref, ht_ref) + _dot(shifted(og_ref, ogt_ref), wo_ref[:GLA_WIDTH, :])
          + _dot(shifted(od_ref, odt_ref), wo_ref[GLA_WIDTH:, :]))
    h3 = _swiglu_half(h2, norm_ref, wg_ref, wu_ref, wd_ref, ff_chunk)
    o_ref[...] = _rmsnorm(h3, fnorm_ref[...])


def _out_ffn2(h, o_gla, o_dsa, wo, norm, wg, wu, wd, fnorm, seq):
    lp, d = h.shape
    d_ff = wg.shape[1]
    n_steps = -(-seq // ROW_TILE)
    assert ROW_TILE % N_META == 0 and n_steps * ROW_TILE + N_META <= lp
    tail_step = ROW_TILE // N_META

    def row(width):
        return pl.BlockSpec((ROW_TILE, width), lambda i: (i, 0))

    def tail(width):
        return pl.BlockSpec((N_META, width), lambda i: ((i + 1) * tail_step, 0))

    return pl.pallas_call(
        functools.partial(_out_ffn2_kernel, ff_chunk=_ffn_chunk(d_ff)),
        grid=(n_steps,),
        in_specs=[row(d), tail(d), row(GLA_WIDTH), tail(GLA_WIDTH), row(DSA_WIDTH), tail(DSA_WIDTH),
                  _const_spec((GLA_WIDTH + DSA_WIDTH, d)), _const_spec((1, d)), _const_spec((d, d_ff)),
                  _const_spec((d, d_ff)), _const_spec((d_ff, d)), _const_spec((1, d))],
        out_specs=row(d),
        out_shape=jax.ShapeDtypeStruct((seq, d), F32),
        compiler_params=pltpu.CompilerParams(dimension_semantics=("parallel",),
                                             vmem_limit_bytes=VMEM_LIMIT),
        name="out_ffn2",
    )(h, h, o_gla, o_gla, o_dsa, o_dsa, wo, norm, wg, wu, wd, fnorm)


def _rope_tables(lp):
    pos = jnp.arange(lp, dtype=F32)
    inv_freq = 1.0 / (ROPE_THETA ** (jnp.arange(0, DSA_HD, 2, dtype=F32) / DSA_HD))
    ang = pos[:, None] * inv_freq[None, :]
    return jnp.cos(ang), jnp.sin(ang)


def _layer(h, meta, k_sel, l_tok, ffn1_norm, ffn1_w_gate, ffn1_w_up, ffn1_w_down, mix_norm, w_in, gla_w_a2, gla_b_a,
           gla_head_norm, w_out, ffn2_norm, ffn2_w_gate, ffn2_w_up, ffn2_w_down, final_norm, cos_t, sin_t):
    d = h.shape[1]
    h1 = _ffn1(h, meta, ffn1_norm.reshape(1, d), ffn1_w_gate.astype(BF16), ffn1_w_up.astype(BF16),
               ffn1_w_down.astype(BF16))

    sizes = (GLA_QK, GLA_QK, GLA_WIDTH, GLA_WIDTH, GLA_LOWRANK, DSA_WIDTH, DSA_WIDTH, DSA_WIDTH, IDX_WIDTH,
             IDX_HD, IDX_HEADS)
    offs = np.concatenate([[0], np.cumsum(sizes)])
    w_bf = w_in.astype(BF16)
    w_all = jnp.concatenate([w_bf[:, :offs[4]], w_bf[:, offs[5]:], w_bf[:, offs[4]:offs[5]],
                             jnp.zeros((d, LANES - IDX_HD - IDX_HEADS - GLA_LOWRANK), BF16)], axis=1)
    wa2p = jnp.zeros((LANES, GLA_QK), F32).at[MISC_GA:MISC_GA + GLA_LOWRANK, :].set(gla_w_a2).astype(BF16)

    gqk, gv, gr, gg, dqt, dk, dvt, iqt, ik, iwt = _in_proj(
        h1, mix_norm.reshape(1, d), w_all, wa2p, gla_b_a.reshape(1, GLA_QK), cos_t, sin_t)
    o_gla = _gla(gqk, gv, gr, gg, gla_head_norm.reshape(1, GLA_DV))
    o_dsa = _dsa(dqt, dk, dvt, iqt, ik, iwt, k_sel, l_tok)
    return _out_ffn2(h1, o_gla, o_dsa, w_out.astype(BF16), ffn2_norm.reshape(1, d), ffn2_w_gate.astype(BF16),
                     ffn2_w_up.astype(BF16), ffn2_w_down.astype(BF16), final_norm.reshape(1, d), l_tok - N_META)


def kernel(x, meta_tokens, ffn1_norm, ffn1_w_gate, ffn1_w_up, ffn1_w_down, mix_norm, w_in, gla_w_a2, gla_b_a,
           gla_head_norm, w_out, ffn2_norm, ffn2_w_gate, ffn2_w_up, ffn2_w_down, final_norm):
    batch, seq, d = x.shape
    depth = ffn1_norm.shape[0]
    assert depth == 1, "the final norm is fused into the layer's last kernel"
    k_sel = min(TOPK_MAX, seq // 4)
    l_tok = seq + N_META
    lp = -(-max(l_tok, -(-seq // ROW_TILE) * ROW_TILE + N_META) // ROW_PAD) * ROW_PAD
    cos_t, sin_t = _rope_tables(lp)
    outs = []
    for bi in range(batch):
        h = jnp.pad(x[bi], ((N_META, lp - l_tok), (0, 0)))
        outs.append(_layer(h, meta_tokens.astype(x.dtype), k_sel, l_tok, ffn1_norm[0], ffn1_w_gate[0], ffn1_w_up[0], ffn1_w_down[0],
                           mix_norm[0], w_in[0], gla_w_a2[0], gla_b_a[0], gla_head_norm[0], w_out[0], ffn2_norm[0],
                           ffn2_w_gate[0], ffn2_w_up[0], ffn2_w_down[0], final_norm, cos_t, sin_t))
    return jnp.stack(outs, axis=0)
```

```python
import functools
import math

import numpy as np
import jax
import jax.numpy as jnp
from jax import lax
from jax.experimental import pallas as pl
from jax.experimental.pallas import tpu as pltpu

N_META = 16
EPS = 1e-6
ROPE_THETA = 10000.0
GLA_HEADS = 4
GLA_DK = 64
GLA_DV = 128
GLA_LOWRANK = 16
GLA_GATE_NORM = 16.0
DSA_HEADS = 8
DSA_HD = 64
IDX_HEADS = 8
IDX_HD = 64
TOPK_MAX = 256

GLA_QK = GLA_HEADS * GLA_DK
GLA_WIDTH = GLA_HEADS * GLA_DV
DSA_WIDTH = DSA_HEADS * DSA_HD
IDX_WIDTH = IDX_HEADS * IDX_HD

LANES = 128
SUBLANES = 8
ROW_TILE = 384
QBLK = 128
KSUB = 128
KTILE = 512
ATILE = KTILE // 2
N_GROUPS = 2 * KSUB
TOP_R = 6
INTERPOLATED_PROBES = 8
UNCHECKED_HALVINGS = 4
GLA_STEP = 128
GLA_SUB = 16
ROW_PAD = math.lcm(ROW_TILE, KTILE)
PAIR_ROWS = 2 * DSA_HD + 16
VMEM_LIMIT = 60 * 1024 * 1024
NEG_BIG = -1e30

MISC_IK = 0
MISC_IW = IDX_HD
MISC_GA = IDX_HD + IDX_HEADS

F32 = jnp.float32
BF16 = jnp.bfloat16


def _dot(a, b):
    return jnp.dot(a, b, preferred_element_type=F32)


def _rmsnorm(x, g):
    return x * lax.rsqrt(jnp.mean(x * x, axis=-1, keepdims=True) + EPS) * g


def _const_spec(shape):
    return pl.BlockSpec(shape, lambda *_: (0,) * len(shape), pipeline_mode=pl.Buffered(1))


def _swiglu_half(x, norm_ref, wg_ref, wu_ref, wd_ref, ff_chunk):
    hn = _rmsnorm(x, norm_ref[...]).astype(BF16)
    d_ff = wg_ref.shape[1]
    acc = jnp.zeros(x.shape, F32)
    for c in range(d_ff // ff_chunk):
        sl = slice(c * ff_chunk, (c + 1) * ff_chunk)
        g = _dot(hn, wg_ref[:, sl])
        u = _dot(hn, wu_ref[:, sl])
        a = (g * jax.nn.sigmoid(g) * u).astype(BF16)
        acc = acc + _dot(a, wd_ref[sl, :])
    return x + 0.5 * acc


def _ffn1_kernel(h_ref, meta_ref, norm_ref, wg_ref, wu_ref, wd_ref, o_ref, *, ff_chunk):
    x = h_ref[...]
    rows = lax.broadcasted_iota(jnp.int32, (x.shape[0], 1), 0)
    meta = jnp.concatenate([meta_ref[...], jnp.zeros((x.shape[0] - N_META, x.shape[1]), x.dtype)], axis=0)
    x = jnp.where(jnp.logical_and(pl.program_id(0) == 0, rows < N_META), meta, x)
    o_ref[...] = _swiglu_half(x, norm_ref, wg_ref, wu_ref, wd_ref, ff_chunk)


def _ffn_chunk(d_ff):
    if d_ff <= 4096:
        return d_ff
    for c in (2048, 1024, 512, 256):
        if d_ff % c == 0:
            return c
    return d_ff


def _ffn1(h, meta, norm, wg, wu, wd):
    lp, d = h.shape
    d_ff = wg.shape[1]
    row = pl.BlockSpec((ROW_TILE, d), lambda i: (i, 0))
    return pl.pallas_call(
        functools.partial(_ffn1_kernel, ff_chunk=_ffn_chunk(d_ff)),
        grid=(lp // ROW_TILE,),
        in_specs=[row, _const_spec((N_META, d)), _const_spec((1, d)), _const_spec((d, d_ff)),
                  _const_spec((d, d_ff)), _const_spec((d_ff, d))],
        out_specs=row,
        out_shape=jax.ShapeDtypeStruct((lp, d), F32),
        compiler_params=pltpu.CompilerParams(dimension_semantics=("parallel",),
                                             vmem_limit_bytes=VMEM_LIMIT),
        name="ffn1",
    )(h, meta, norm, wg, wu, wd)


def _rope(x, cos, sin_signed, first_half):
    parts = []
    for c in range(x.shape[1] // LANES):
        xc = x[:, c * LANES:(c + 1) * LANES]
        rot = jnp.where(first_half, pltpu.roll(xc, LANES - 32, 1), pltpu.roll(xc, 32, 1))
        parts.append(xc * cos + rot * sin_signed)
    return parts[0] if len(parts) == 1 else jnp.concatenate(parts, axis=1)


def _log_sigmoid(x):
    return jnp.minimum(x, 0.0) - jnp.log(1.0 + jnp.exp(-jnp.abs(x)))


def _in_proj_kernel(h_ref, norm_ref, w_ref, wa2_ref, ba_ref, cos_ref, sin_ref, tcos_ref, tsin_ref,
                    gqk_ref, gv_ref, gr_ref, gg_ref, dqt_ref, dk_ref, dvt_ref, iqt_ref, ik_ref, iwt_ref):
    hn = _rmsnorm(h_ref[...], norm_ref[...]).astype(BF16)
    spread = functools.partial(jnp.dot, preferred_element_type=F32, precision=lax.Precision.HIGHEST)
    cos = spread(cos_ref[...], tcos_ref[...])
    sin_s = spread(sin_ref[...], tsin_ref[...])
    lane = lax.broadcasted_iota(jnp.int32, cos.shape, 1)
    first_half = (lane % DSA_HD) < (DSA_HD // 2)

    def proj(c0, width):
        return _dot(hn, w_ref[:, c0:c0 + width])

    c = 0
    gqk_ref[...] = proj(c, 2 * GLA_QK); c += 2 * GLA_QK
    gv_ref[...] = proj(c, GLA_WIDTH); c += GLA_WIDTH
    gr_ref[...] = proj(c, GLA_WIDTH); c += GLA_WIDTH
    dqt_ref[...] = (_rope(proj(c, DSA_WIDTH), cos, sin_s, first_half) * (DSA_HD ** -0.5)).T.astype(BF16)
    c += DSA_WIDTH
    dk_ref[...] = _rope(proj(c, DSA_WIDTH), cos, sin_s, first_half).astype(BF16); c += DSA_WIDTH
    dvt_ref[...] = proj(c, DSA_WIDTH).T.astype(BF16); c += DSA_WIDTH
    iqt_ref[...] = (_rope(proj(c, IDX_WIDTH), cos, sin_s, first_half) * (IDX_HD ** -0.5)).T.astype(BF16)
    c += IDX_WIDTH
    misc = proj(c, LANES)
    ik_ref[...] = _rope(misc, cos, sin_s, first_half)[:, MISC_IK:MISC_IK + IDX_HD].astype(BF16)
    iwt_ref[...] = (misc * (IDX_HEADS ** -0.5)).T
    pre = _dot(misc.astype(BF16), wa2_ref[...]) + ba_ref[...]
    gg_ref[...] = _log_sigmoid(pre) * (1.0 / GLA_GATE_NORM)


def _rope_spread():
    half = DSA_HD // 2
    lane = np.arange(LANES)
    hit = (lane[None, :] % half) == np.arange(half)[:, None]
    sign = np.where((lane % DSA_HD) < half, -1.0, 1.0)[None, :]
    return jnp.asarray(hit, F32), jnp.asarray(hit * sign, F32)


def _in_proj(h, norm, w_all, wa2p, ba, cos, sin):
    lp, d = h.shape
    ncols = w_all.shape[1]
    half = DSA_HD // 2
    tcos, tsin = _rope_spread()

    def row(width):
        return pl.BlockSpec((ROW_TILE, width), lambda i: (i, 0))

    def col(height):
        return pl.BlockSpec((height, ROW_TILE), lambda i: (0, i))

    outs = [
        (row(2 * GLA_QK), (lp, 2 * GLA_QK), F32), (row(GLA_WIDTH), (lp, GLA_WIDTH), F32),
        (row(GLA_WIDTH), (lp, GLA_WIDTH), F32), (row(GLA_QK), (lp, GLA_QK), F32),
        (col(DSA_WIDTH), (DSA_WIDTH, lp), BF16), (row(DSA_WIDTH), (lp, DSA_WIDTH), BF16),
        (col(DSA_WIDTH), (DSA_WIDTH, lp), BF16), (col(IDX_WIDTH), (IDX_WIDTH, lp), BF16),
        (row(IDX_HD), (lp, IDX_HD), BF16), (col(LANES), (LANES, lp), F32),
    ]
    return pl.pallas_call(
        _in_proj_kernel,
        grid=(lp // ROW_TILE,),
        in_specs=[row(d), _const_spec((1, d)), _const_spec((d, ncols)), _const_spec((LANES, GLA_QK)),
                  _const_spec((1, GLA_QK)), row(half), row(half), _const_spec((half, LANES)),
                  _const_spec((half, LANES))],
        out_specs=[spec for spec, _, _ in outs],
        out_shape=[jax.ShapeDtypeStruct(shape, dt) for _, shape, dt in outs],
        compiler_params=pltpu.CompilerParams(dimension_semantics=("parallel",),
                                             vmem_limit_bytes=VMEM_LIMIT),
        name="in_proj",
    )(h, norm, w_all, wa2p, ba, cos, sin, tcos, tsin)


def _gla_kernel(qk_ref, v_ref, r_ref, g_ref, hnorm_ref, esel_ref, o_ref, s_ref):
    n_sub = GLA_STEP // GLA_SUB

    @pl.when(pl.program_id(0) == 0)
    def _():
        s_ref[...] = jnp.zeros(s_ref.shape, F32)

    q = qk_ref[:, :GLA_QK] * (GLA_DK ** -0.5)
    k = qk_ref[:, GLA_QK:]
    v = v_ref[...]
    g = g_ref[...]

    ri = lax.broadcasted_iota(jnp.int32, (GLA_STEP, GLA_STEP), 0)
    ci = lax.broadcasted_iota(jnp.int32, (GLA_STEP, GLA_STEP), 1)
    tri = ((ri // GLA_SUB == ci // GLA_SUB) & (ci <= ri)).astype(F32)
    b = jnp.dot(tri, g, preferred_element_type=F32, precision=lax.Precision.HIGHEST)

    b3 = b.reshape(n_sub, GLA_SUB, GLA_QK)
    k3 = k.reshape(n_sub, GLA_SUB, GLA_QK)
    b_last = jnp.broadcast_to(b3[:, GLA_SUB - 1:GLA_SUB, :], b3.shape).reshape(GLA_STEP, GLA_QK)

    pos = lax.broadcasted_iota(jnp.int32, (GLA_STEP, GLA_QK), 0) % GLA_SUB
    t_cols = []
    for j in range(GLA_SUB):
        kj = jnp.broadcast_to(k3[:, j:j + 1, :], k3.shape).reshape(GLA_STEP, GLA_QK)
        bj = jnp.broadcast_to(b3[:, j:j + 1, :], b3.shape).reshape(GLA_STEP, GLA_QK)
        e = jnp.where(pos >= j, b - bj, -jnp.inf)
        t_cols.append((q * kj * jnp.exp(e)).astype(BF16))
    a = _dot(jnp.concatenate(t_cols, axis=1), esel_ref[...]).astype(BF16)
    head_of_lane = lax.broadcasted_iota(jnp.int32, (GLA_SUB, GLA_WIDTH), 1) // GLA_DV
    o_intra = []
    for c in range(n_sub):
        rows = slice(c * GLA_SUB, (c + 1) * GLA_SUB)
        v_c = v[rows, :]
        v_heads = jnp.concatenate([jnp.where(head_of_lane == h, v_c, 0.0) for h in range(GLA_HEADS)]
                                  + [jnp.zeros((LANES - GLA_HEADS * GLA_SUB, GLA_WIDTH), F32)], axis=0)
        o_intra.append(_dot(a[rows, :], v_heads.astype(BF16)))
    o = jnp.concatenate(o_intra, axis=0)

    qe = q * jnp.exp(b)
    kd = k * jnp.exp(b_last - b)
    kd_t = kd.T
    dec_t = jnp.exp(b).T
    lane_head = lax.broadcasted_iota(jnp.int32, (GLA_SUB, GLA_QK), 1) // GLA_DK
    s = s_ref[...]
    o_inter = []
    for c in range(n_sub):
        rows = slice(c * GLA_SUB, (c + 1) * GLA_SUB)
        qe_c = qe[rows, :]
        q_stack = jnp.concatenate(
            [jnp.where(lane_head == h, qe_c, 0.0) for h in range(GLA_HEADS)], axis=0).astype(BF16)
        r_stack = _dot(q_stack, s.astype(BF16))
        o_inter.append(jnp.concatenate(
            [r_stack[h * GLA_SUB:(h + 1) * GLA_SUB, :] for h in range(GLA_HEADS)], axis=1))
        kd_c = kd_t[:, rows].astype(BF16)
        v_c = v[rows, :].astype(BF16)
        u_c = jnp.concatenate(
            [_dot(kd_c[h * GLA_DK:(h + 1) * GLA_DK, :], v_c[:, h * GLA_DV:(h + 1) * GLA_DV])
             for h in range(GLA_HEADS)], axis=0)
        last = c * GLA_SUB + GLA_SUB - 1
        s = dec_t[:, last:last + 1] * s + u_c
    s_ref[...] = s
    o = o + jnp.concatenate(o_inter, axis=0)

    r = r_ref[...]
    gate = r * jax.nn.sigmoid(r)
    hn = hnorm_ref[...]
    outs = []
    for h in range(GLA_HEADS):
        oh = o[:, h * GLA_DV:(h + 1) * GLA_DV]
        outs.append(_rmsnorm(oh, hn))
    o_ref[...] = (jnp.concatenate(outs, axis=1) * gate).astype(o_ref.dtype)


def _gla_head_selector():
    r = np.arange(GLA_SUB * GLA_QK)
    j, h = r // GLA_QK, (r % GLA_QK) // GLA_DK
    sel = np.zeros((GLA_SUB * GLA_QK, LANES), np.float32)
    sel[r, h * GLA_SUB + j] = 1.0
    return jnp.asarray(sel, BF16)


def _gla(gqk, gv, gr, gg, hnorm):
    lp = gqk.shape[0]
    assert GLA_HEADS * GLA_SUB <= LANES

    def row(width):
        return pl.BlockSpec((GLA_STEP, width), lambda i: (i, 0))

    return pl.pallas_call(
        _gla_kernel,
        grid=(lp // GLA_STEP,),
        in_specs=[row(2 * GLA_QK), row(GLA_WIDTH), row(GLA_WIDTH), row(GLA_QK), _const_spec((1, GLA_DV)),
                  _const_spec((GLA_SUB * GLA_QK, LANES))],
        out_specs=row(GLA_WIDTH),
        out_shape=jax.ShapeDtypeStruct((lp, GLA_WIDTH), BF16),
        scratch_shapes=[pltpu.VMEM((GLA_QK, GLA_DV), F32)],
        compiler_params=pltpu.CompilerParams(dimension_semantics=("arbitrary",),
                                             vmem_limit_bytes=VMEM_LIMIT),
        name="gla",
    )(gqk, gv, gr, gg, hnorm, _gla_head_selector())


def _key_to_float(u):
    bits = jnp.where(u >= 0, u, u ^ jnp.int32(0x7FFFFFFF))
    return lax.bitcast_convert_type(bits, F32)


def _float_to_key(f):
    bits = lax.bitcast_convert_type(f, jnp.int32)
    return jnp.where(bits >= 0, bits, bits ^ jnp.int32(0x7FFFFFFF))


_KEY_NEG_INF = np.int32(np.uint32(0xFF800000) ^ np.uint32(0x7FFFFFFF))


def _tile_loop(n, body, carry, unroll):
    start = 0
    for width in ((unroll,) if isinstance(unroll, int) else unroll):
        trips = (n - start) // width

        def group(u, c, width=width, start=start):
            for r in range(width):
                c = body(start + u * width + r, c)
            return c

        carry = lax.fori_loop(0, trips, group, carry)
        start = start + trips * width
    return lax.fori_loop(start, n, body, carry)


def _fold_rows(x, op):
    parts = [x[r:r + SUBLANES, :] for r in range(0, x.shape[0], SUBLANES)]
    while len(parts) > 1:
        parts = [op(parts[a], parts[a + 1]) if a + 1 < len(parts) else parts[a] for a in range(0, len(parts), 2)]
    return parts[0]


def _dsa_block(qt_ref, iqt_ref, iwt_ref, k_ref, vt_ref, ik_ref, o_ref, sc_ref, bias_ref, acc_ref, rhs_ref,
               sa_ref, sb_ref, top_ref, k_sel):
    i = pl.program_id(0)
    n_tiles = (i * QBLK + QBLK + KTILE - 1) // KTILE
    last = n_tiles - 1
    q_pos = i * QBLK + lax.broadcasted_iota(jnp.int32, (KSUB, QBLK), 1)
    key_row = lax.broadcasted_iota(jnp.int32, (KSUB, QBLK), 0)

    def visible(ks):
        return (ks + key_row) <= q_pos

    def peeled(body, carry):
        carry = _tile_loop(last, functools.partial(body, masked=False), carry, (4, 2))
        return body(last, carry, masked=True)

    iqt = iqt_ref[...]
    iq_all = jnp.concatenate([iqt[h * IDX_HD:(h + 1) * IDX_HD, :] for h in range(IDX_HEADS)], axis=1)
    iw = iwt_ref[MISC_IW:MISC_IW + IDX_HEADS, :]
    top_ref[...] = jnp.full(top_ref.shape, -jnp.inf, F32)

    def score_tile(t, carry, masked):
        for sub in range(KTILE // KSUB):
            ks = pl.multiple_of(t * KTILE + sub * KSUB, KSUB)
            lg = _dot(ik_ref[pl.ds(ks, KSUB), :], iq_all)
            sc = iw[0:1, :] * jnp.maximum(lg[:, 0:QBLK], 0.0)
            for h in range(1, IDX_HEADS):
                sc = sc + iw[h:h + 1, :] * jnp.maximum(lg[:, h * QBLK:(h + 1) * QBLK], 0.0)
            if masked:
                sc = jnp.where(visible(ks), sc, -jnp.inf)
            sc_ref[pl.ds(ks, KSUB), :] = sc
            rest = sc
            for r in range(TOP_R):
                rows = slice(r * N_GROUPS + (sub % 2) * KSUB, r * N_GROUPS + (sub % 2 + 1) * KSUB)
                held = top_ref[rows, :]
                top_ref[rows, :] = jnp.maximum(held, rest)
                if r + 1 < TOP_R:
                    rest = jnp.minimum(held, rest)
        return carry

    peeled(score_tile, 0)

    kk = float(k_sel)

    def lane_sum(x):
        return jnp.sum(_fold_rows(x, jnp.add), axis=0, keepdims=True)

    def count_scores(pred):
        def body(t, acc):
            ks = pl.multiple_of(t * KTILE, KTILE)
            return acc + _fold_rows(jnp.where(pred(sc_ref[pl.ds(ks, KTILE), :]), 1.0, 0.0), jnp.add)
        return jnp.sum(_tile_loop(n_tiles, body, jnp.zeros((SUBLANES, QBLK), F32), 4), axis=0, keepdims=True)

    def count_ge(cand):
        return lane_sum(jnp.where(top_ref[...] >= cand, 1.0, 0.0))

    best = top_ref[0:N_GROUPS, :]
    lo = jnp.maximum(_float_to_key(jnp.min(_fold_rows(best, jnp.minimum), axis=0, keepdims=True)), _KEY_NEG_INF)
    hi = _float_to_key(jnp.max(_fold_rows(best, jnp.maximum), axis=0, keepdims=True))
    n_lo = count_ge(_key_to_float(lo))

    def settled(lo, hi, n_lo):
        return jnp.logical_or(n_lo <= kk, lo >= hi)

    def any_open(lo, hi, n_lo):
        return jnp.max(jnp.where(settled(lo, hi, n_lo), 0.0, 1.0))

    def midpoint(lo, hi):
        return (lo | hi) - ((lo ^ hi) >> 1)

    def probe_at(key, lo, hi, n_lo):
        open_ = jnp.logical_not(settled(lo, hi, n_lo))
        cnt = count_ge(_key_to_float(key))
        up = jnp.logical_and(open_, cnt >= kk)
        down = jnp.logical_and(open_, cnt < kk)
        return jnp.where(up, key, lo), jnp.where(down, key - 1, hi), jnp.where(up, cnt, n_lo), cnt, up, down

    def halve(lo, hi, n_lo):
        return probe_at(midpoint(lo, hi), lo, hi, n_lo)[:3]

    log_k = float(np.log(k_sel))

    def interpolate(st):
        lo, hi, n_lo, g_lo, g_hi, side = st
        f_lo, f_hi = _key_to_float(lo), _key_to_float(hi)
        guess = f_lo + (g_lo - log_k) / jnp.maximum(g_lo - g_hi, 1e-9) * (f_hi - f_lo)
        usable = jnp.logical_and(lo > _KEY_NEG_INF, jnp.abs(guess) < jnp.inf)
        key = jnp.where(usable, jnp.clip(_float_to_key(guess), lo + 1, hi), midpoint(lo, hi))
        lo, hi, n_lo, cnt, up, down = probe_at(key, lo, hi, n_lo)
        g_cnt = jnp.log(jnp.maximum(cnt, 0.5))
        g_hi = jnp.where(down, g_cnt, jnp.where(jnp.logical_and(up, side > 0.0), 0.5 * (g_hi + log_k), g_hi))
        g_lo = jnp.where(up, g_cnt, jnp.where(jnp.logical_and(down, side < 0.0), 0.5 * (g_lo + log_k), g_lo))
        side = jnp.where(up, 1.0, jnp.where(down, -1.0, side))
        return lo, hi, n_lo, g_lo, g_hi, side

    def checked_halve(st):
        lo, hi, n_lo = halve(*halve(*st[:3]))
        return lo, hi, n_lo, any_open(lo, hi, n_lo), st[4] + 1

    lo, hi, n_lo, _, _, _ = lax.fori_loop(
        0, INTERPOLATED_PROBES, lambda _, st: interpolate(st),
        (lo, hi, n_lo, jnp.log(jnp.maximum(n_lo, 0.5)), jnp.full((1, QBLK), float(np.log(0.5)), F32),
         jnp.zeros((1, QBLK), F32)))
    lo, hi, n_lo = lax.fori_loop(0, UNCHECKED_HALVINGS, lambda _, st: halve(*st), (lo, hi, n_lo))
    lo, hi, n_lo, _, _ = lax.while_loop(lambda st: jnp.logical_and(st[3] > 0.0, st[4] < 17), checked_halve,
                                        (lo, hi, n_lo, any_open(lo, hi, n_lo), jnp.int32(0)))
    tops = top_ref[...]
    thr = jnp.min(_fold_rows(jnp.where(tops >= _key_to_float(lo), tops, jnp.inf), jnp.minimum), axis=0,
                  keepdims=True)
    n_ge = count_scores(lambda x: x >= thr)

    def next_above(v):
        def body(t, carry):
            low, cnt = carry
            ks = pl.multiple_of(t * KTILE, KTILE)
            x = sc_ref[pl.ds(ks, KTILE), :]
            above = x > v
            return (jnp.minimum(low, _fold_rows(jnp.where(above, x, jnp.inf), jnp.minimum)),
                    cnt + _fold_rows(jnp.where(above, 1.0, 0.0), jnp.add))
        low, cnt = _tile_loop(n_tiles, body, (jnp.full((SUBLANES, QBLK), jnp.inf, F32),
                                              jnp.zeros((SUBLANES, QBLK), F32)), 4)
        return jnp.min(low, axis=0, keepdims=True), jnp.sum(cnt, axis=0, keepdims=True)

    def any_climbing(n_ge, stuck):
        return jnp.max(jnp.where(jnp.logical_or(n_ge <= kk, stuck > 0.0), 0.0, 1.0))

    def climb(st):
        thr, n_ge, stuck, _, it = st
        nxt, n_nxt = next_above(thr)
        go = jnp.logical_and(jnp.logical_and(n_ge > kk, stuck == 0.0), n_nxt >= kk)
        stuck = jnp.where(jnp.logical_and(n_ge > kk, n_nxt < kk), 1.0, stuck)
        thr, n_ge = jnp.where(go, nxt, thr), jnp.where(go, n_nxt, n_ge)
        return thr, n_ge, stuck, any_climbing(n_ge, stuck), it + 1

    stuck = jnp.zeros((1, QBLK), F32)
    thr, n_ge, _, _, _ = lax.while_loop(
        lambda st: jnp.logical_and(st[3] > 0.0, st[4] <= n_tiles * KTILE), climb,
        (thr, n_ge, stuck, any_climbing(n_ge, stuck), jnp.int32(0)))
    has_ties = jnp.max(n_ge) > kk

    def bias_fast(t, carry, masked):
        for sub in range(KTILE // KSUB):
            ks = pl.multiple_of(t * KTILE + sub * KSUB, KSUB)
            keep = sc_ref[pl.ds(ks, KSUB), :] >= thr
            if masked:
                keep = jnp.logical_and(keep, visible(ks))
            bias_ref[pl.ds(ks, KSUB), :] = jnp.where(keep, 0.0, NEG_BIG).astype(BF16)
        return carry

    def bias_ties(t, run, masked, budget, tri):
        for sub in range(KTILE // KSUB):
            ks = pl.multiple_of(t * KTILE + sub * KSUB, KSUB)
            x = sc_ref[pl.ds(ks, KSUB), :]
            above = x > thr
            tie = x == thr
            tie_f = jnp.where(tie, 1.0, 0.0)
            rank = run + _dot(tri, tie_f.astype(BF16))
            keep = jnp.logical_or(above, jnp.logical_and(tie, rank <= budget))
            if masked:
                keep = jnp.logical_and(keep, visible(ks))
            bias_ref[pl.ds(ks, KSUB), :] = jnp.where(keep, 0.0, NEG_BIG).astype(BF16)
            run = run + jnp.sum(tie_f, axis=0, keepdims=True)
        return run

    @pl.when(jnp.logical_not(has_ties))
    def _():
        peeled(bias_fast, 0)

    @pl.when(has_ties)
    def _():
        budget = kk - count_scores(lambda x: x > thr)
        tri = (lax.broadcasted_iota(jnp.int32, (KSUB, KSUB), 1) <= key_row).astype(BF16)
        peeled(functools.partial(bias_ties, budget=budget, tri=tri), jnp.zeros((1, QBLK), F32))

    qt = qt_ref[...]
    eye = (lax.broadcasted_iota(jnp.int32, (QBLK, QBLK), 0)
           == lax.broadcasted_iota(jnp.int32, (QBLK, QBLK), 1)).astype(BF16)
    zero = jnp.zeros((DSA_HD, QBLK), BF16)
    n_pairs = DSA_HEADS // 2
    for c in range(n_pairs):
        qa = qt[(2 * c) * DSA_HD:(2 * c + 1) * DSA_HD, :]
        qb = qt[(2 * c + 1) * DSA_HD:(2 * c + 2) * DSA_HD, :]
        rhs_ref[c] = jnp.concatenate([jnp.concatenate([qa, zero], axis=1),
                                      jnp.concatenate([zero, qb], axis=1),
                                      jnp.concatenate([eye, eye], axis=1)], axis=0)
    acc_ref[...] = jnp.zeros(acc_ref.shape, F32)
    ones_rows = jnp.ones((PAIR_ROWS - 2 * DSA_HD, ATILE), BF16)

    def qk_stage(ks, s_ref):
        bias = bias_ref[pl.ds(ks, ATILE), :]
        tile_max = []
        for c in range(n_pairs):
            lhs = jnp.concatenate([k_ref[pl.ds(ks, ATILE), c * 2 * DSA_HD:(c + 1) * 2 * DSA_HD], bias], axis=1)
            s = _dot(lhs, rhs_ref[c])
            s_ref[c] = s
            tile_max.append(_fold_rows(s, jnp.maximum))
        return tuple(tile_max)

    def pv_stage(ks, s_ref, tile_max, ms):
        new_ms = []
        for c in range(n_pairs):
            m_new = jnp.maximum(ms[c], jnp.max(tile_max[c], axis=0, keepdims=True))
            alpha = jnp.exp(ms[c] - m_new)
            p = jnp.exp(s_ref[c] - m_new).astype(BF16)
            base = c * PAIR_ROWS
            vt_aug = jnp.concatenate([vt_ref[c * 2 * DSA_HD:(c + 1) * 2 * DSA_HD, pl.ds(ks, ATILE)], ones_rows],
                                     axis=0)
            pv = _dot(vt_aug, p)
            for hh in range(2):
                rows = slice(base + hh * DSA_HD, base + (hh + 1) * DSA_HD)
                cols = slice(hh * QBLK, (hh + 1) * QBLK)
                acc_ref[rows, cols] = alpha[:, cols] * acc_ref[rows, cols] + pv[hh * DSA_HD:(hh + 1) * DSA_HD, cols]
            sums = slice(base + 2 * DSA_HD, base + PAIR_ROWS)
            acc_ref[sums, :] = alpha * acc_ref[sums, :] + pv[2 * DSA_HD:, :]
            new_ms.append(m_new)
        return tuple(new_ms)

    def attn_tile(t, carry):
        max_a, ms = carry
        k0 = pl.multiple_of(t * KTILE, KTILE)
        k1 = pl.multiple_of(t * KTILE + ATILE, ATILE)
        k2 = pl.multiple_of(t * KTILE + KTILE, KTILE)
        max_b = qk_stage(k1, sb_ref)
        ms = pv_stage(k0, sa_ref, max_a, ms)
        max_a = qk_stage(k2, sa_ref)
        ms = pv_stage(k1, sb_ref, max_b, ms)
        return max_a, ms

    ms = tuple(jnp.full((1, 2 * QBLK), NEG_BIG, F32) for _ in range(n_pairs))
    max_a, ms = _tile_loop(last, attn_tile, (qk_stage(0, sa_ref), ms), (4, 2))
    k0 = pl.multiple_of(last * KTILE, KTILE)
    k1 = pl.multiple_of(last * KTILE + ATILE, ATILE)
    max_b = qk_stage(k1, sb_ref)
    ms = pv_stage(k0, sa_ref, max_a, ms)
    pv_stage(k1, sb_ref, max_b, ms)
    outs = []
    for h in range(DSA_HEADS):
        base, hh = (h // 2) * PAIR_ROWS, h % 2
        cols = slice(hh * QBLK, (hh + 1) * QBLK)
        outs.append(acc_ref[base + hh * DSA_HD:base + (hh + 1) * DSA_HD, cols]
                    / acc_ref[base + 2 * DSA_HD:base + 2 * DSA_HD + 1, cols])
    o_ref[...] = jnp.concatenate(outs, axis=0).T.astype(o_ref.dtype)


def _dsa_kernel(qt_ref, iqt_ref, iwt_ref, k_ref, vt_ref, ik_ref, o_ref, sc_ref, bias_ref, acc_ref, rhs_ref,
                sa_ref, sb_ref, top_ref, *, k_sel, l_tok):
    has_tokens = pl.program_id(0) * QBLK < l_tok

    @pl.when(has_tokens)
    def _():
        _dsa_block(qt_ref, iqt_ref, iwt_ref, k_ref, vt_ref, ik_ref, o_ref, sc_ref, bias_ref, acc_ref, rhs_ref,
                   sa_ref, sb_ref, top_ref, k_sel)

    @pl.when(jnp.logical_not(has_tokens))
    def _():
        o_ref[...] = jnp.zeros(o_ref.shape, o_ref.dtype)


def _dsa(dqt, dk, dvt, iqt, ik, iwt, k_sel, l_tok):
    lp = dk.shape[0]
    assert k_sel <= N_GROUPS, "the smallest group maximum must have at least k_sel tops above it"

    def col(height):
        return pl.BlockSpec((height, QBLK), lambda i: (0, i))

    return pl.pallas_call(
        functools.partial(_dsa_kernel, k_sel=k_sel, l_tok=l_tok),
        grid=(lp // QBLK,),
        in_specs=[col(DSA_WIDTH), col(IDX_WIDTH), col(LANES), _const_spec((lp, DSA_WIDTH)),
                  _const_spec((DSA_WIDTH, lp)), _const_spec((lp, IDX_HD))],
        out_specs=pl.BlockSpec((QBLK, DSA_WIDTH), lambda i: (i, 0)),
        out_shape=jax.ShapeDtypeStruct((lp, DSA_WIDTH), BF16),
        scratch_shapes=[pltpu.VMEM((lp, QBLK), F32), pltpu.VMEM((lp, QBLK), BF16),
                        pltpu.VMEM((DSA_HEADS // 2 * PAIR_ROWS, 2 * QBLK), F32),
                        pltpu.VMEM((DSA_HEADS // 2, 2 * LANES, 2 * QBLK), BF16),
                        pltpu.VMEM((DSA_HEADS // 2, ATILE, 2 * QBLK), F32),
                        pltpu.VMEM((DSA_HEADS // 2, ATILE, 2 * QBLK), F32),
                        pltpu.VMEM((TOP_R * N_GROUPS, QBLK), F32)],
        compiler_params=pltpu.CompilerParams(dimension_semantics=("parallel",),
                                             vmem_limit_bytes=VMEM_LIMIT),
        name="dsa",
    )(dqt, iqt, iwt, dk, dvt, ik)


def _out_ffn2_kernel(h_ref, ht_ref, og_ref, ogt_ref, od_ref, odt_ref, wo_ref, norm_ref, wg_ref, wu_ref, wd_ref,
                     fnorm_ref, o_ref, *, ff_chunk):
    def shifted(main_ref, tail_ref):
        return jnp.concatenate([main_ref[N_META:, :], tail_ref[...]], axis=0)

    h2 = (shifted(h_ref, ht_ref) + _dot(shifted(og_ref, ogt_ref), wo_ref[:GLA_WIDTH, :])
          + _dot(shifted(od_ref, odt_ref), wo_ref[GLA_WIDTH:, :]))
    h3 = _swiglu_half(h2, norm_ref, wg_ref, wu_ref, wd_ref, ff_chunk)
    o_ref[...] = _rmsnorm(h3, fnorm_ref[...])


def _out_ffn2(h, o_gla, o_dsa, wo, norm, wg, wu, wd, fnorm, seq):
    lp, d = h.shape
    d_ff = wg.shape[1]
    n_steps = -(-seq // ROW_TILE)
    assert ROW_TILE % N_META == 0 and n_steps * ROW_TILE + N_META <= lp
    tail_step = ROW_TILE // N_META

    def row(width):
        return pl.BlockSpec((ROW_TILE, width), lambda i: (i, 0))

    def tail(width):
        return pl.BlockSpec((N_META, width), lambda i: ((i + 1) * tail_step, 0))

    return pl.pallas_call(
        functools.partial(_out_ffn2_kernel, ff_chunk=_ffn_chunk(d_ff)),
        grid=(n_steps,),
        in_specs=[row(d), tail(d), row(GLA_WIDTH), tail(GLA_WIDTH), row(DSA_WIDTH), tail(DSA_WIDTH),
                  _const_spec((GLA_WIDTH + DSA_WIDTH, d)), _const_spec((1, d)), _const_spec((d, d_ff)),
                  _const_spec((d, d_ff)), _const_spec((d_ff, d)), _const_spec((1, d))],
        out_specs=row(d),
        out_shape=jax.ShapeDtypeStruct((seq, d), F32),
        compiler_params=pltpu.CompilerParams(dimension_semantics=("parallel",),
                                             vmem_limit_bytes=VMEM_LIMIT),
        name="out_ffn2",
    )(h, h, o_gla, o_gla, o_dsa, o_dsa, wo, norm, wg, wu, wd, fnorm)


def _rope_tables(lp):
    pos = jnp.arange(lp, dtype=F32)
    inv_freq = 1.0 / (ROPE_THETA ** (jnp.arange(0, DSA_HD, 2, dtype=F32) / DSA_HD))
    ang = pos[:, None] * inv_freq[None, :]
    return jnp.cos(ang), jnp.sin(ang)


def _layer(h, meta, k_sel, l_tok, ffn1_norm, ffn1_w_gate, ffn1_w_up, ffn1_w_down, mix_norm, w_in, gla_w_a2, gla_b_a,
           gla_head_norm, w_out, ffn2_norm, ffn2_w_gate, ffn2_w_up, ffn2_w_down, final_norm, cos_t, sin_t):
    d = h.shape[1]
    h1 = _ffn1(h, meta, ffn1_norm.reshape(1, d), ffn1_w_gate.astype(BF16), ffn1_w_up.astype(BF16),
               ffn1_w_down.astype(BF16))

    sizes = (GLA_QK, GLA_QK, GLA_WIDTH, GLA_WIDTH, GLA_LOWRANK, DSA_WIDTH, DSA_WIDTH, DSA_WIDTH, IDX_WIDTH,
             IDX_HD, IDX_HEADS)
    offs = np.concatenate([[0], np.cumsum(sizes)])
    w_bf = w_in.astype(BF16)
    w_all = jnp.concatenate([w_bf[:, :offs[4]], w_bf[:, offs[5]:], w_bf[:, offs[4]:offs[5]],
                             jnp.zeros((d, LANES - IDX_HD - IDX_HEADS - GLA_LOWRANK), BF16)], axis=1)
    wa2p = jnp.zeros((LANES, GLA_QK), F32).at[MISC_GA:MISC_GA + GLA_LOWRANK, :].set(gla_w_a2).astype(BF16)

    gqk, gv, gr, gg, dqt, dk, dvt, iqt, ik, iwt = _in_proj(
        h1, mix_norm.reshape(1, d), w_all, wa2p, gla_b_a.reshape(1, GLA_QK), cos_t, sin_t)
    o_gla = _gla(gqk, gv, gr, gg, gla_head_norm.reshape(1, GLA_DV))
    o_dsa = _dsa(dqt, dk, dvt, iqt, ik, iwt, k_sel, l_tok)
    return _out_ffn2(h1, o_gla, o_dsa, w_out.astype(BF16), ffn2_norm.reshape(1, d), ffn2_w_gate.astype(BF16),
                     ffn2_w_up.astype(BF16), ffn2_w_down.astype(BF16), final_norm.reshape(1, d), l_tok - N_META)


def kernel(x, meta_tokens, ffn1_norm, ffn1_w_gate, ffn1_w_up, ffn1_w_down, mix_norm, w_in, gla_w_a2, gla_b_a,
           gla_head_norm, w_out, ffn2_norm, ffn2_w_gate, ffn2_w_up, ffn2_w_down, final_norm):
    batch, seq, d = x.shape
    depth = ffn1_norm.shape[0]
    assert depth == 1, "the final norm is fused into the layer's last kernel"
    k_sel = min(TOPK_MAX, seq // 4)
    l_tok = seq + N_META
    lp = -(-max(l_tok, -(-seq // ROW_TILE) * ROW_TILE + N_META) // ROW_PAD) * ROW_PAD
    cos_t, sin_t = _rope_tables(lp)
    outs = []
    for bi in range(batch):
        h = jnp.pad(x[bi], ((N_META, lp - l_tok), (0, 0)))
        outs.append(_layer(h, meta_tokens.astype(x.dtype), k_sel, l_tok, ffn1_norm[0], ffn1_w_gate[0], ffn1_w_up[0], ffn1_w_down[0],
                           mix_norm[0], w_in[0], gla_w_a2[0], gla_b_a[0], gla_head_norm[0], w_out[0], ffn2_norm[0],
                           ffn2_w_gate[0], ffn2_w_up[0], ffn2_w_down[0], final_norm, cos_t, sin_t))
    return jnp.stack(outs, axis=0)
```

```python
import functools
import math

import numpy as np
import jax
import jax.numpy as jnp
from jax import lax
from jax.experimental import pallas as pl
from jax.experimental.pallas import tpu as pltpu

N_META = 16
EPS = 1e-6
ROPE_THETA = 10000.0
GLA_HEADS = 4
GLA_DK = 64
GLA_DV = 128
GLA_LOWRANK = 16
GLA_GATE_NORM = 16.0
DSA_HEADS = 8
DSA_HD = 64
IDX_HEADS = 8
IDX_HD = 64
TOPK_MAX = 256

GLA_QK = GLA_HEADS * GLA_DK
GLA_WIDTH = GLA_HEADS * GLA_DV
DSA_WIDTH = DSA_HEADS * DSA_HD
IDX_WIDTH = IDX_HEADS * IDX_HD

LANES = 128
SUBLANES = 8
ROW_TILE = 384
QBLK = 128
KSUB = 128
KTILE = 512
ATILE = KTILE // 2
N_GROUPS = 2 * KSUB
TOP_R = 6
INTERPOLATED_PROBES = 8
UNCHECKED_HALVINGS = 4
GLA_STEP = 128
GLA_SUB = 16
ROW_PAD = math.lcm(ROW_TILE, KTILE)
PAIR_ROWS = 2 * DSA_HD + 16
VMEM_LIMIT = 60 * 1024 * 1024
NEG_BIG = -1e30

MISC_IK = 0
MISC_IW = IDX_HD
MISC_GA = IDX_HD + IDX_HEADS

F32 = jnp.float32
BF16 = jnp.bfloat16


def _dot(a, b):
    return jnp.dot(a, b, preferred_element_type=F32)


def _rmsnorm(x, g):
    return x * lax.rsqrt(jnp.mean(x * x, axis=-1, keepdims=True) + EPS) * g


def _const_spec(shape):
    return pl.BlockSpec(shape, lambda *_: (0,) * len(shape), pipeline_mode=pl.Buffered(1))


def _swiglu_half(x, norm_ref, wg_ref, wu_ref, wd_ref, ff_chunk):
    hn = _rmsnorm(x, norm_ref[...]).astype(BF16)
    d_ff = wg_ref.shape[1]
    acc = jnp.zeros(x.shape, F32)
    for c in range(d_ff // ff_chunk):
        sl = slice(c * ff_chunk, (c + 1) * ff_chunk)
        g = _dot(hn, wg_ref[:, sl])
        u = _dot(hn, wu_ref[:, sl])
        a = (g * jax.nn.sigmoid(g) * u).astype(BF16)
        acc = acc + _dot(a, wd_ref[sl, :])
    return x + 0.5 * acc


def _ffn1_kernel(h_ref, meta_ref, norm_ref, wg_ref, wu_ref, wd_ref, o_ref, *, ff_chunk):
    x = h_ref[...]
    rows = lax.broadcasted_iota(jnp.int32, (x.shape[0], 1), 0)
    meta = jnp.concatenate([meta_ref[...], jnp.zeros((x.shape[0] - N_META, x.shape[1]), x.dtype)], axis=0)
    x = jnp.where(jnp.logical_and(pl.program_id(0) == 0, rows < N_META), meta, x)
    o_ref[...] = _swiglu_half(x, norm_ref, wg_ref, wu_ref, wd_ref, ff_chunk)


def _ffn_chunk(d_ff):
    if d_ff <= 4096:
        return d_ff
    for c in (2048, 1024, 512, 256):
        if d_ff % c == 0:
            return c
    return d_ff


def _ffn1(h, meta, norm, wg, wu, wd):
    lp, d = h.shape
    d_ff = wg.shape[1]
    row = pl.BlockSpec((ROW_TILE, d), lambda i: (i, 0))
    return pl.pallas_call(
        functools.partial(_ffn1_kernel, ff_chunk=_ffn_chunk(d_ff)),
        grid=(lp // ROW_TILE,),
        in_specs=[row, _const_spec((N_META, d)), _const_spec((1, d)), _const_spec((d, d_ff)),
                  _const_spec((d, d_ff)), _const_spec((d_ff, d))],
        out_specs=row,
        out_shape=jax.ShapeDtypeStruct((lp, d), F32),
        compiler_params=pltpu.CompilerParams(dimension_semantics=("parallel",),
                                             vmem_limit_bytes=VMEM_LIMIT),
        name="ffn1",
    )(h, meta, norm, wg, wu, wd)


def _rope(x, cos, sin_signed, first_half):
    parts = []
    for c in range(x.shape[1] // LANES):
        xc = x[:, c * LANES:(c + 1) * LANES]
        rot = jnp.where(first_half, pltpu.roll(xc, LANES - 32, 1), pltpu.roll(xc, 32, 1))
        parts.append(xc * cos + rot * sin_signed)
    return parts[0] if len(parts) == 1 else jnp.concatenate(parts, axis=1)


def _log_sigmoid(x):
    return jnp.minimum(x, 0.0) - jnp.log(1.0 + jnp.exp(-jnp.abs(x)))


def _in_proj_kernel(h_ref, norm_ref, w_ref, wa2_ref, ba_ref, cos_ref, sin_ref, tcos_ref, tsin_ref,
                    gqk_ref, gv_ref, gr_ref, gg_ref, dqt_ref, dk_ref, dvt_ref, iqt_ref, ik_ref, iwt_ref):
    hn = _rmsnorm(h_ref[...], norm_ref[...]).astype(BF16)
    spread = functools.partial(jnp.dot, preferred_element_type=F32, precision=lax.Precision.HIGHEST)
    cos = spread(cos_ref[...], tcos_ref[...])
    sin_s = spread(sin_ref[...], tsin_ref[...])
    lane = lax.broadcasted_iota(jnp.int32, cos.shape, 1)
    first_half = (lane % DSA_HD) < (DSA_HD // 2)

    def proj(c0, width):
        return _dot(hn, w_ref[:, c0:c0 + width])

    c = 0
    gqk_ref[...] = proj(c, 2 * GLA_QK); c += 2 * GLA_QK
    gv_ref[...] = proj(c, GLA_WIDTH); c += GLA_WIDTH
    gr_ref[...] = proj(c, GLA_WIDTH); c += GLA_WIDTH
    dqt_ref[...] = (_rope(proj(c, DSA_WIDTH), cos, sin_s, first_half) * (DSA_HD ** -0.5)).T.astype(BF16)
    c += DSA_WIDTH
    dk_ref[...] = _rope(proj(c, DSA_WIDTH), cos, sin_s, first_half).astype(BF16); c += DSA_WIDTH
    dvt_ref[...] = proj(c, DSA_WIDTH).T.astype(BF16); c += DSA_WIDTH
    iqt_ref[...] = (_rope(proj(c, IDX_WIDTH), cos, sin_s, first_half) * (IDX_HD ** -0.5)).T.astype(BF16)
    c += IDX_WIDTH
    misc = proj(c, LANES)
    ik_ref[...] = _rope(misc, cos, sin_s, first_half)[:, MISC_IK:MISC_IK + IDX_HD].astype(BF16)
    iwt_ref[...] = (misc * (IDX_HEADS ** -0.5)).T
    pre = _dot(misc.astype(BF16), wa2_ref[...]) + ba_ref[...]
    gg_ref[...] = _log_sigmoid(pre) * (1.0 / GLA_GATE_NORM)


def _rope_spread():
    half = DSA_HD // 2
    lane = np.arange(LANES)
    hit = (lane[None, :] % half) == np.arange(half)[:, None]
    sign = np.where((lane % DSA_HD) < half, -1.0, 1.0)[None, :]
    return jnp.asarray(hit, F32), jnp.asarray(hit * sign, F32)


def _in_proj(h, norm, w_all, wa2p, ba, cos, sin):
    lp, d = h.shape
    ncols = w_all.shape[1]
    half = DSA_HD // 2
    tcos, tsin = _rope_spread()

    def row(width):
        return pl.BlockSpec((ROW_TILE, width), lambda i: (i, 0))

    def col(height):
        return pl.BlockSpec((height, ROW_TILE), lambda i: (0, i))

    outs = [
        (row(2 * GLA_QK), (lp, 2 * GLA_QK), F32), (row(GLA_WIDTH), (lp, GLA_WIDTH), F32),
        (row(GLA_WIDTH), (lp, GLA_WIDTH), F32), (row(GLA_QK), (lp, GLA_QK), F32),
        (col(DSA_WIDTH), (DSA_WIDTH, lp), BF16), (row(DSA_WIDTH), (lp, DSA_WIDTH), BF16),
        (col(DSA_WIDTH), (DSA_WIDTH, lp), BF16), (col(IDX_WIDTH), (IDX_WIDTH, lp), BF16),
        (row(IDX_HD), (lp, IDX_HD), BF16), (col(LANES), (LANES, lp), F32),
    ]
    return pl.pallas_call(
        _in_proj_kernel,
        grid=(lp // ROW_TILE,),
        in_specs=[row(d), _const_spec((1, d)), _const_spec((d, ncols)), _const_spec((LANES, GLA_QK)),
                  _const_spec((1, GLA_QK)), row(half), row(half), _const_spec((half, LANES)),
                  _const_spec((half, LANES))],
        out_specs=[spec for spec, _, _ in outs],
        out_shape=[jax.ShapeDtypeStruct(shape, dt) for _, shape, dt in outs],
        compiler_params=pltpu.CompilerParams(dimension_semantics=("parallel",),
                                             vmem_limit_bytes=VMEM_LIMIT),
        name="in_proj",
    )(h, norm, w_all, wa2p, ba, cos, sin, tcos, tsin)


def _gla_kernel(qk_ref, v_ref, r_ref, g_ref, hnorm_ref, esel_ref, o_ref, s_ref):
    n_sub = GLA_STEP // GLA_SUB

    @pl.when(pl.program_id(0) == 0)
    def _():
        s_ref[...] = jnp.zeros(s_ref.shape, F32)

    q = qk_ref[:, :GLA_QK] * (GLA_DK ** -0.5)
    k = qk_ref[:, GLA_QK:]
    v = v_ref[...]
    g = g_ref[...]

    ri = lax.broadcasted_iota(jnp.int32, (GLA_STEP, GLA_STEP), 0)
    ci = lax.broadcasted_iota(jnp.int32, (GLA_STEP, GLA_STEP), 1)
    tri = ((ri // GLA_SUB == ci // GLA_SUB) & (ci <= ri)).astype(F32)
    b = jnp.dot(tri, g, preferred_element_type=F32, precision=lax.Precision.HIGHEST)

    b3 = b.reshape(n_sub, GLA_SUB, GLA_QK)
    k3 = k.reshape(n_sub, GLA_SUB, GLA_QK)
    b_last = jnp.broadcast_to(b3[:, GLA_SUB - 1:GLA_SUB, :], b3.shape).reshape(GLA_STEP, GLA_QK)

    pos = lax.broadcasted_iota(jnp.int32, (GLA_STEP, GLA_QK), 0) % GLA_SUB
    t_cols = []
    for j in range(GLA_SUB):
        kj = jnp.broadcast_to(k3[:, j:j + 1, :], k3.shape).reshape(GLA_STEP, GLA_QK)
        bj = jnp.broadcast_to(b3[:, j:j + 1, :], b3.shape).reshape(GLA_STEP, GLA_QK)
        e = jnp.where(pos >= j, b - bj, -jnp.inf)
        t_cols.append((q * kj * jnp.exp(e)).astype(BF16))
    a = _dot(jnp.concatenate(t_cols, axis=1), esel_ref[...]).astype(BF16)
    head_of_lane = lax.broadcasted_iota(jnp.int32, (GLA_SUB, GLA_WIDTH), 1) // GLA_DV
    o_intra = []
    for c in range(n_sub):
        rows = slice(c * GLA_SUB, (c + 1) * GLA_SUB)
        v_c = v[rows, :]
        v_heads = jnp.concatenate([jnp.where(head_of_lane == h, v_c, 0.0) for h in range(GLA_HEADS)]
                                  + [jnp.zeros((LANES - GLA_HEADS * GLA_SUB, GLA_WIDTH), F32)], axis=0)
        o_intra.append(_dot(a[rows, :], v_heads.astype(BF16)))
    o = jnp.concatenate(o_intra, axis=0)

    qe = q * jnp.exp(b)
    kd = k * jnp.exp(b_last - b)
    kd_t = kd.T
    dec_t = jnp.exp(b).T
    lane_head = lax.broadcasted_iota(jnp.int32, (GLA_SUB, GLA_QK), 1) // GLA_DK
    s = s_ref[...]
    o_inter = []
    for c in range(n_sub):
        rows = slice(c * GLA_SUB, (c + 1) * GLA_SUB)
        qe_c = qe[rows, :]
        q_stack = jnp.concatenate(
            [jnp.where(lane_head == h, qe_c, 0.0) for h in range(GLA_HEADS)], axis=0).astype(BF16)
        r_stack = _dot(q_stack, s.astype(BF16))
        o_inter.append(jnp.concatenate(
            [r_stack[h * GLA_SUB:(h + 1) * GLA_SUB, :] for h in range(GLA_HEADS)], axis=1))
        kd_c = kd_t[:, rows].astype(BF16)
        v_c = v[rows, :].astype(BF16)
        u_c = jnp.concatenate(
            [_dot(kd_c[h * GLA_DK:(h + 1) * GLA_DK, :], v_c[:, h * GLA_DV:(h + 1) * GLA_DV])
             for h in range(GLA_HEADS)], axis=0)
        last = c * GLA_SUB + GLA_SUB - 1
        s = dec_t[:, last:last + 1] * s + u_c
    s_ref[...] = s
    o = o + jnp.concatenate(o_inter, axis=0)

    r = r_ref[...]
    gate = r * jax.nn.sigmoid(r)
    hn = hnorm_ref[...]
    outs = []
    for h in range(GLA_HEADS):
        oh = o[:, h * GLA_DV:(h + 1) * GLA_DV]
        outs.append(_rmsnorm(oh, hn))
    o_ref[...] = (jnp.concatenate(outs, axis=1) * gate).astype(o_ref.dtype)


def _gla_head_selector():
    r = np.arange(GLA_SUB * GLA_QK)
    j, h = r // GLA_QK, (r % GLA_QK) // GLA_DK
    sel = np.zeros((GLA_SUB * GLA_QK, LANES), np.float32)
    sel[r, h * GLA_SUB + j] = 1.0
    return jnp.asarray(sel, BF16)


def _gla(gqk, gv, gr, gg, hnorm):
    lp = gqk.shape[0]
    assert GLA_HEADS * GLA_SUB <= LANES

    def row(width):
        return pl.BlockSpec((GLA_STEP, width), lambda i: (i, 0))

    return pl.pallas_call(
        _gla_kernel,
        grid=(lp // GLA_STEP,),
        in_specs=[row(2 * GLA_QK), row(GLA_WIDTH), row(GLA_WIDTH), row(GLA_QK), _const_spec((1, GLA_DV)),
                  _const_spec((GLA_SUB * GLA_QK, LANES))],
        out_specs=row(GLA_WIDTH),
        out_shape=jax.ShapeDtypeStruct((lp, GLA_WIDTH), BF16),
        scratch_shapes=[pltpu.VMEM((GLA_QK, GLA_DV), F32)],
        compiler_params=pltpu.CompilerParams(dimension_semantics=("arbitrary",),
                                             vmem_limit_bytes=VMEM_LIMIT),
        name="gla",
    )(gqk, gv, gr, gg, hnorm, _gla_head_selector())


def _key_to_float(u):
    bits = jnp.where(u >= 0, u, u ^ jnp.int32(0x7FFFFFFF))
    return lax.bitcast_convert_type(bits, F32)


def _float_to_key(f):
    bits = lax.bitcast_convert_type(f, jnp.int32)
    return jnp.where(bits >= 0, bits, bits ^ jnp.int32(0x7FFFFFFF))


_KEY_NEG_INF = np.int32(np.uint32(0xFF800000) ^ np.uint32(0x7FFFFFFF))


def _tile_loop(n, body, carry, unroll):
    start = 0
    for width in ((unroll,) if isinstance(unroll, int) else unroll):
        trips = (n - start) // width

        def group(u, c, width=width, start=start):
            for r in range(width):
                c = body(start + u * width + r, c)
            return c

        carry = lax.fori_loop(0, trips, group, carry)
        start = start + trips * width
    return lax.fori_loop(start, n, body, carry)


def _fold_rows(x, op):
    parts = [x[r:r + SUBLANES, :] for r in range(0, x.shape[0], SUBLANES)]
    while len(parts) > 1:
        parts = [op(parts[a], parts[a + 1]) if a + 1 < len(parts) else parts[a] for a in range(0, len(parts), 2)]
    return parts[0]


def _dsa_block(qt_ref, iqt_ref, iwt_ref, k_ref, vt_ref, ik_ref, o_ref, sc_ref, bias_ref, acc_ref, rhs_ref,
               sa_ref, sb_ref, top_ref, k_sel):
    i = pl.program_id(0)
    n_tiles = (i * QBLK + QBLK + KTILE - 1) // KTILE
    last = n_tiles - 1
    q_pos = i * QBLK + lax.broadcasted_iota(jnp.int32, (KSUB, QBLK), 1)
    key_row = lax.broadcasted_iota(jnp.int32, (KSUB, QBLK), 0)

    def visible(ks):
        return (ks + key_row) <= q_pos

    def peeled(body, carry):
        carry = _tile_loop(last, functools.partial(body, masked=False), carry, (4, 2))
        return body(last, carry, masked=True)

    iqt = iqt_ref[...]
    iq_all = jnp.concatenate([iqt[h * IDX_HD:(h + 1) * IDX_HD, :] for h in range(IDX_HEADS)], axis=1)
    iw = iwt_ref[MISC_IW:MISC_IW + IDX_HEADS, :]
    top_ref[...] = jnp.full(top_ref.shape, -jnp.inf, F32)

    def score_tile(t, carry, masked):
        for sub in range(KTILE // KSUB):
            ks = pl.multiple_of(t * KTILE + sub * KSUB, KSUB)
            lg = _dot(ik_ref[pl.ds(ks, KSUB), :], iq_all)
            sc = iw[0:1, :] * jnp.maximum(lg[:, 0:QBLK], 0.0)
            for h in range(1, IDX_HEADS):
                sc = sc + iw[h:h + 1, :] * jnp.maximum(lg[:, h * QBLK:(h + 1) * QBLK], 0.0)
            if masked:
                sc = jnp.where(visible(ks), sc, -jnp.inf)
            sc_ref[pl.ds(ks, KSUB), :] = sc
            rest = sc
            for r in range(TOP_R):
                rows = slice(r * N_GROUPS + (sub % 2) * KSUB, r * N_GROUPS + (sub % 2 + 1) * KSUB)
                held = top_ref[rows, :]
                top_ref[rows, :] = jnp.maximum(held, rest)
                if r + 1 < TOP_R:
                    rest = jnp.minimum(held, rest)
        return carry

    peeled(score_tile, 0)

    kk = float(k_sel)

    def lane_sum(x):
        return jnp.sum(_fold_rows(x, jnp.add), axis=0, keepdims=True)

    def count_scores(pred):
        def body(t, acc):
            ks = pl.multiple_of(t * KTILE, KTILE)
            return acc + _fold_rows(jnp.where(pred(sc_ref[pl.ds(ks, KTILE), :]), 1.0, 0.0), jnp.add)
        return jnp.sum(_tile_loop(n_tiles, body, jnp.zeros((SUBLANES, QBLK), F32), 4), axis=0, keepdims=True)

    def count_ge(cand):
        return lane_sum(jnp.where(top_ref[...] >= cand, 1.0, 0.0))

    best = top_ref[0:N_GROUPS, :]
    lo = jnp.maximum(_float_to_key(jnp.min(_fold_rows(best, jnp.minimum), axis=0, keepdims=True)), _KEY_NEG_INF)
    hi = _float_to_key(jnp.max(_fold_rows(best, jnp.maximum), axis=0, keepdims=True))
    n_lo = count_ge(_key_to_float(lo))

    def settled(lo, hi, n_lo):
        return jnp.logical_or(n_lo <= kk, lo >= hi)

    def any_open(lo, hi, n_lo):
        return jnp.max(jnp.where(settled(lo, hi, n_lo), 0.0, 1.0))

    def midpoint(lo, hi):
        return (lo | hi) - ((lo ^ hi) >> 1)

    def probe_at(key, lo, hi, n_lo):
        open_ = jnp.logical_not(settled(lo, hi, n_lo))
        cnt = count_ge(_key_to_float(key))
        up = jnp.logical_and(open_, cnt >= kk)
        down = jnp.logical_and(open_, cnt < kk)
        return jnp.where(up, key, lo), jnp.where(down, key - 1, hi), jnp.where(up, cnt, n_lo), cnt, up, down

    def halve(lo, hi, n_lo):
        return probe_at(midpoint(lo, hi), lo, hi, n_lo)[:3]

    log_k = float(np.log(k_sel))

    def interpolate(st):
        lo, hi, n_lo, g_lo, g_hi, side = st
        f_lo, f_hi = _key_to_float(lo), _key_to_float(hi)
        guess = f_lo + (g_lo - log_k) / jnp.maximum(g_lo - g_hi, 1e-9) * (f_hi - f_lo)
        usable = jnp.logical_and(lo > _KEY_NEG_INF, jnp.abs(guess) < jnp.inf)
        key = jnp.where(usable, jnp.clip(_float_to_key(guess), lo + 1, hi), midpoint(lo, hi))
        lo, hi, n_lo, cnt, up, down = probe_at(key, lo, hi, n_lo)
        g_cnt = jnp.log(jnp.maximum(cnt, 0.5))
        g_hi = jnp.where(down, g_cnt, jnp.where(jnp.logical_and(up, side > 0.0), 0.5 * (g_hi + log_k), g_hi))
        g_lo = jnp.where(up, g_cnt, jnp.where(jnp.logical_and(down, side < 0.0), 0.5 * (g_lo + log_k), g_lo))
        side = jnp.where(up, 1.0, jnp.where(down, -1.0, side))
        return lo, hi, n_lo, g_lo, g_hi, side

    def checked_halve(st):
        lo, hi, n_lo = halve(*halve(*st[:3]))
        return lo, hi, n_lo, any_open(lo, hi, n_lo), st[4] + 1

    lo, hi, n_lo, _, _, _ = lax.fori_loop(
        0, INTERPOLATED_PROBES, lambda _, st: interpolate(st),
        (lo, hi, n_lo, jnp.log(jnp.maximum(n_lo, 0.5)), jnp.full((1, QBLK), float(np.log(0.5)), F32),
         jnp.zeros((1, QBLK), F32)))
    lo, hi, n_lo = lax.fori_loop(0, UNCHECKED_HALVINGS, lambda _, st: halve(*st), (lo, hi, n_lo))
    lo, hi, n_lo, _, _ = lax.while_loop(lambda st: jnp.logical_and(st[3] > 0.0, st[4] < 17), checked_halve,
                                        (lo, hi, n_lo, any_open(lo, hi, n_lo), jnp.int32(0)))
    tops = top_ref[...]
    thr = jnp.min(_fold_rows(jnp.where(tops >= _key_to_float(lo), tops, jnp.inf), jnp.minimum), axis=0,
                  keepdims=True)
    n_ge = count_scores(lambda x: x >= thr)

    def next_above(v):
        def body(t, carry):
            low, cnt = carry
            ks = pl.multiple_of(t * KTILE, KTILE)
            x = sc_ref[pl.ds(ks, KTILE), :]
            above = x > v
            return (jnp.minimum(low, _fold_rows(jnp.where(above, x, jnp.inf), jnp.minimum)),
                    cnt + _fold_rows(jnp.where(above, 1.0, 0.0), jnp.add))
        low, cnt = _tile_loop(n_tiles, body, (jnp.full((SUBLANES, QBLK), jnp.inf, F32),
                                              jnp.zeros((SUBLANES, QBLK), F32)), 4)
        return jnp.min(low, axis=0, keepdims=True), jnp.sum(cnt, axis=0, keepdims=True)

    def any_climbing(n_ge, stuck):
        return jnp.max(jnp.where(jnp.logical_or(n_ge <= kk, stuck > 0.0), 0.0, 1.0))

    def climb(st):
        thr, n_ge, stuck, _, it = st
        nxt, n_nxt = next_above(thr)
        go = jnp.logical_and(jnp.logical_and(n_ge > kk, stuck == 0.0), n_nxt >= kk)
        stuck = jnp.where(jnp.logical_and(n_ge > kk, n_nxt < kk), 1.0, stuck)
        thr, n_ge = jnp.where(go, nxt, thr), jnp.where(go, n_nxt, n_ge)
        return thr, n_ge, stuck, any_climbing(n_ge, stuck), it + 1

    stuck = jnp.zeros((1, QBLK), F32)
    thr, n_ge, _, _, _ = lax.while_loop(
        lambda st: jnp.logical_and(st[3] > 0.0, st[4] <= n_tiles * KTILE), climb,
        (thr, n_ge, stuck, any_climbing(n_ge, stuck), jnp.int32(0)))
    has_ties = jnp.max(n_ge) > kk

    def bias_fast(t, carry, masked):
        for sub in range(KTILE // KSUB):
            ks = pl.multiple_of(t * KTILE + sub * KSUB, KSUB)
            keep = sc_ref[pl.ds(ks, KSUB), :] >= thr
            if masked:
                keep = jnp.logical_and(keep, visible(ks))
            bias_ref[pl.ds(ks, KSUB), :] = jnp.where(keep, 0.0, NEG_BIG).astype(BF16)
        return carry

    def bias_ties(t, run, masked, budget, tri):
        for sub in range(KTILE // KSUB):
            ks = pl.multiple_of(t * KTILE + sub * KSUB, KSUB)
            x = sc_ref[pl.ds(ks, KSUB), :]
            above = x > thr
            tie = x == thr
            tie_f = jnp.where(tie, 1.0, 0.0)
            rank = run + _dot(tri, tie_f.astype(BF16))
            keep = jnp.logical_or(above, jnp.logical_and(tie, rank <= budget))
            if masked:
                keep = jnp.logical_and(keep, visible(ks))
            bias_ref[pl.ds(ks, KSUB), :] = jnp.where(keep, 0.0, NEG_BIG).astype(BF16)
            run = run + jnp.sum(tie_f, axis=0, keepdims=True)
        return run

    @pl.when(jnp.logical_not(has_ties))
    def _():
        peeled(bias_fast, 0)

    @pl.when(has_ties)
    def _():
        budget = kk - count_scores(lambda x: x > thr)
        tri = (lax.broadcasted_iota(jnp.int32, (KSUB, KSUB), 1) <= key_row).astype(BF16)
        peeled(functools.partial(bias_ties, budget=budget, tri=tri), jnp.zeros((1, QBLK), F32))

    qt = qt_ref[...]
    eye = (lax.broadcasted_iota(jnp.int32, (QBLK, QBLK), 0)
           == lax.broadcasted_iota(jnp.int32, (QBLK, QBLK), 1)).astype(BF16)
    zero = jnp.zeros((DSA_HD, QBLK), BF16)
    n_pairs = DSA_HEADS // 2
    for c in range(n_pairs):
        qa = qt[(2 * c) * DSA_HD:(2 * c + 1) * DSA_HD, :]
        qb = qt[(2 * c + 1) * DSA_HD:(2 * c + 2) * DSA_HD, :]
        rhs_ref[c] = jnp.concatenate([jnp.concatenate([qa, zero], axis=1),
                                      jnp.concatenate([zero, qb], axis=1),
                                      jnp.concatenate([eye, eye], axis=1)], axis=0)
    acc_ref[...] = jnp.zeros(acc_ref.shape, F32)
    ones_rows = jnp.ones((PAIR_ROWS - 2 * DSA_HD, ATILE), BF16)

    def qk_stage(ks, s_ref):
        bias = bias_ref[pl.ds(ks, ATILE), :]
        tile_max = []
        for c in range(n_pairs):
            lhs = jnp.concatenate([k_ref[pl.ds(ks, ATILE), c * 2 * DSA_HD:(c + 1) * 2 * DSA_HD], bias], axis=1)
            s = _dot(lhs, rhs_ref[c])
            s_ref[c] = s
            tile_max.append(_fold_rows(s, jnp.maximum))
        return tuple(tile_max)

    def pv_stage(ks, s_ref, tile_max, ms):
        new_ms = []
        for c in range(n_pairs):
            m_new = jnp.maximum(ms[c], jnp.max(tile_max[c], axis=0, keepdims=True))
            alpha = jnp.exp(ms[c] - m_new)
            p = jnp.exp(s_ref[c] - m_new).astype(BF16)
            base = c * PAIR_ROWS
            vt_aug = jnp.concatenate([vt_ref[c * 2 * DSA_HD:(c + 1) * 2 * DSA_HD, pl.ds(ks, ATILE)], ones_rows],
                                     axis=0)
            pv = _dot(vt_aug, p)
            for hh in range(2):
                rows = slice(base + hh * DSA_HD, base + (hh + 1) * DSA_HD)
                cols = slice(hh * QBLK, (hh + 1) * QBLK)
                acc_ref[rows, cols] = alpha[:, cols] * acc_ref[rows, cols] + pv[hh * DSA_HD:(hh + 1) * DSA_HD, cols]
            sums = slice(base + 2 * DSA_HD, base + PAIR_ROWS)
            acc_ref[sums, :] = alpha * acc_ref[sums, :] + pv[2 * DSA_HD:, :]
            new_ms.append(m_new)
        return tuple(new_ms)

    def attn_tile(t, carry):
        max_a, ms = carry
        k0 = pl.multiple_of(t * KTILE, KTILE)
        k1 = pl.multiple_of(t * KTILE + ATILE, ATILE)
        k2 = pl.multiple_of(t * KTILE + KTILE, KTILE)
        max_b = qk_stage(k1, sb_ref)
        ms = pv_stage(k0, sa_ref, max_a, ms)
        max_a = qk_stage(k2, sa_ref)
        ms = pv_stage(k1, sb_ref, max_b, ms)
        return max_a, ms

    ms = tuple(jnp.full((1, 2 * QBLK), NEG_BIG, F32) for _ in range(n_pairs))
    max_a, ms = _tile_loop(last, attn_tile, (qk_stage(0, sa_ref), ms), (8, 4, 2))
    k0 = pl.multiple_of(last * KTILE, KTILE)
    k1 = pl.multiple_of(last * KTILE + ATILE, ATILE)
    max_b = qk_stage(k1, sb_ref)
    ms = pv_stage(k0, sa_ref, max_a, ms)
    pv_stage(k1, sb_ref, max_b, ms)
    outs = []
    for h in range(DSA_HEADS):
        base, hh = (h // 2) * PAIR_ROWS, h % 2
        cols = slice(hh * QBLK, (hh + 1) * QBLK)
        outs.append(acc_ref[base + hh * DSA_HD:base + (hh + 1) * DSA_HD, cols]
                    / acc_ref[base + 2 * DSA_HD:base + 2 * DSA_HD + 1, cols])
    o_ref[...] = jnp.concatenate(outs, axis=0).T.astype(o_ref.dtype)


def _dsa_kernel(qt_ref, iqt_ref, iwt_ref, k_ref, vt_ref, ik_ref, o_ref, sc_ref, bias_ref, acc_ref, rhs_ref,
                sa_ref, sb_ref, top_ref, *, k_sel, l_tok):
    has_tokens = pl.program_id(0) * QBLK < l_tok

    @pl.when(has_tokens)
    def _():
        _dsa_block(qt_ref, iqt_ref, iwt_ref, k_ref, vt_ref, ik_ref, o_ref, sc_ref, bias_ref, acc_ref, rhs_ref,
                   sa_ref, sb_ref, top_ref, k_sel)

    @pl.when(jnp.logical_not(has_tokens))
    def _():
        o_ref[...] = jnp.zeros(o_ref.shape, o_ref.dtype)


def _dsa(dqt, dk, dvt, iqt, ik, iwt, k_sel, l_tok):
    lp = dk.shape[0]
    assert k_sel <= N_GROUPS, "the smallest group maximum must have at least k_sel tops above it"

    def col(height):
        return pl.BlockSpec((height, QBLK), lambda i: (0, i))

    return pl.pallas_call(
        functools.partial(_dsa_kernel, k_sel=k_sel, l_tok=l_tok),
        grid=(lp // QBLK,),
        in_specs=[col(DSA_WIDTH), col(IDX_WIDTH), col(LANES), _const_spec((lp, DSA_WIDTH)),
                  _const_spec((DSA_WIDTH, lp)), _const_spec((lp, IDX_HD))],
        out_specs=pl.BlockSpec((QBLK, DSA_WIDTH), lambda i: (i, 0)),
        out_shape=jax.ShapeDtypeStruct((lp, DSA_WIDTH), BF16),
        scratch_shapes=[pltpu.VMEM((lp, QBLK), F32), pltpu.VMEM((lp, QBLK), BF16),
                        pltpu.VMEM((DSA_HEADS // 2 * PAIR_ROWS, 2 * QBLK), F32),
                        pltpu.VMEM((DSA_HEADS // 2, 2 * LANES, 2 * QBLK), BF16),
                        pltpu.VMEM((DSA_HEADS // 2, ATILE, 2 * QBLK), F32),
                        pltpu.VMEM((DSA_HEADS // 2, ATILE, 2 * QBLK), F32),
                        pltpu.VMEM((TOP_R * N_GROUPS, QBLK), F32)],
        compiler_params=pltpu.CompilerParams(dimension_semantics=("parallel",),
                                             vmem_limit_bytes=VMEM_LIMIT),
        name="dsa",
    )(dqt, iqt, iwt, dk, dvt, ik)


def _out_ffn2_kernel(h_ref, ht_ref, og_ref, ogt_ref, od_ref, odt_ref, wo_ref, norm_ref, wg_ref, wu_ref, wd_ref,
                     fnorm_ref, o_ref, *, ff_chunk):
    def shifted(main_ref, tail_ref):
        return jnp.concatenate([main_ref[N_META:, :], tail_ref[...]], axis=0)

    h2 = (shifted(h_ref, ht_ref) + _dot(shifted(og_ref, ogt_ref), wo_ref[:GLA_WIDTH, :])
          + _dot(shifted(od_ref, odt_ref), wo_ref[GLA_WIDTH:, :]))
    h3 = _swiglu_half(h2, norm_ref, wg_ref, wu_ref, wd_ref, ff_chunk)
    o_ref[...] = _rmsnorm(h3, fnorm_ref[...])


def _out_ffn2(h, o_gla, o_dsa, wo, norm, wg, wu, wd, fnorm, seq):
    lp, d = h.shape
    d_ff = wg.shape[1]
    n_steps = -(-seq // ROW_TILE)
    assert ROW_TILE % N_META == 0 and n_steps * ROW_TILE + N_META <= lp
    tail_step = ROW_TILE // N_META

    def row(width):
        return pl.BlockSpec((ROW_TILE, width), lambda i: (i, 0))

    def tail(width):
        return pl.BlockSpec((N_META, width), lambda i: ((i + 1) * tail_step, 0))

    return pl.pallas_call(
        functools.partial(_out_ffn2_kernel, ff_chunk=_ffn_chunk(d_ff)),
        grid=(n_steps,),
        in_specs=[row(d), tail(d), row(GLA_WIDTH), tail(GLA_WIDTH), row(DSA_WIDTH), tail(DSA_WIDTH),
                  _const_spec((GLA_WIDTH + DSA_WIDTH, d)), _const_spec((1, d)), _const_spec((d, d_ff)),
                  _const_spec((d, d_ff)), _const_spec((d_ff, d)), _const_spec((1, d))],
        out_specs=row(d),
        out_shape=jax.ShapeDtypeStruct((seq, d), F32),
        compiler_params=pltpu.CompilerParams(dimension_semantics=("parallel",),
                                             vmem_limit_bytes=VMEM_LIMIT),
        name="out_ffn2",
    )(h, h, o_gla, o_gla, o_dsa, o_dsa, wo, norm, wg, wu, wd, fnorm)


def _rope_tables(lp):
    pos = jnp.arange(lp, dtype=F32)
    inv_freq = 1.0 / (ROPE_THETA ** (jnp.arange(0, DSA_HD, 2, dtype=F32) / DSA_HD))
    ang = pos[:, None] * inv_freq[None, :]
    return jnp.cos(ang), jnp.sin(ang)


def _layer(h, meta, k_sel, l_tok, ffn1_norm, ffn1_w_gate, ffn1_w_up, ffn1_w_down, mix_norm, w_in, gla_w_a2, gla_b_a,
           gla_head_norm, w_out, ffn2_norm, ffn2_w_gate, ffn2_w_up, ffn2_w_down, final_norm, cos_t, sin_t):
    d = h.shape[1]
    h1 = _ffn1(h, meta, ffn1_norm.reshape(1, d), ffn1_w_gate.astype(BF16), ffn1_w_up.astype(BF16),
               ffn1_w_down.astype(BF16))

    sizes = (GLA_QK, GLA_QK, GLA_WIDTH, GLA_WIDTH, GLA_LOWRANK, DSA_WIDTH, DSA_WIDTH, DSA_WIDTH, IDX_WIDTH,
             IDX_HD, IDX_HEADS)
    offs = np.concatenate([[0], np.cumsum(sizes)])
    w_bf = w_in.astype(BF16)
    w_all = jnp.concatenate([w_bf[:, :offs[4]], w_bf[:, offs[5]:], w_bf[:, offs[4]:offs[5]],
                             jnp.zeros((d, LANES - IDX_HD - IDX_HEADS - GLA_LOWRANK), BF16)], axis=1)
    wa2p = jnp.zeros((LANES, GLA_QK), F32).at[MISC_GA:MISC_GA + GLA_LOWRANK, :].set(gla_w_a2).astype(BF16)

    gqk, gv, gr, gg, dqt, dk, dvt, iqt, ik, iwt = _in_proj(
        h1, mix_norm.reshape(1, d), w_all, wa2p, gla_b_a.reshape(1, GLA_QK), cos_t, sin_t)
    o_gla = _gla(gqk, gv, gr, gg, gla_head_norm.reshape(1, GLA_DV))
    o_dsa = _dsa(dqt, dk, dvt, iqt, ik, iwt, k_sel, l_tok)
    return _out_ffn2(h1, o_gla, o_dsa, w_out.astype(BF16), ffn2_norm.reshape(1, d), ffn2_w_gate.astype(BF16),
                     ffn2_w_up.astype(BF16), ffn2_w_down.astype(BF16), final_norm.reshape(1, d), l_tok - N_META)


def kernel(x, meta_tokens, ffn1_norm, ffn1_w_gate, ffn1_w_up, ffn1_w_down, mix_norm, w_in, gla_w_a2, gla_b_a,
           gla_head_norm, w_out, ffn2_norm, ffn2_w_gate, ffn2_w_up, ffn2_w_down, final_norm):
    batch, seq, d = x.shape
    depth = ffn1_norm.shape[0]
    assert depth == 1, "the final norm is fused into the layer's last kernel"
    k_sel = min(TOPK_MAX, seq // 4)
    l_tok = seq + N_META
    lp = -(-max(l_tok, -(-seq // ROW_TILE) * ROW_TILE + N_META) // ROW_PAD) * ROW_PAD
    cos_t, sin_t = _rope_tables(lp)
    outs = []
    for bi in range(batch):
        h = jnp.pad(x[bi], ((N_META, lp - l_tok), (0, 0)))
        outs.append(_layer(h, meta_tokens.astype(x.dtype), k_sel, l_tok, ffn1_norm[0], ffn1_w_gate[0], ffn1_w_up[0], ffn1_w_down[0],
                           mix_norm[0], w_in[0], gla_w_a2[0], gla_b_a[0], gla_head_norm[0], w_out[0], ffn2_norm[0],
                           ffn2_w_gate[0], ffn2_w_up[0], ffn2_w_down[0], final_norm, cos_t, sin_t))
    return jnp.stack(outs, axis=0)
```

```python
import functools
import math

import numpy as np
import jax
import jax.numpy as jnp
from jax import lax
from jax.experimental import pallas as pl
from jax.experimental.pallas import tpu as pltpu

N_META = 16
EPS = 1e-6
ROPE_THETA = 10000.0
GLA_HEADS = 4
GLA_DK = 64
GLA_DV = 128
GLA_LOWRANK = 16
GLA_GATE_NORM = 16.0
DSA_HEADS = 8
DSA_HD = 64
IDX_HEADS = 8
IDX_HD = 64
TOPK_MAX = 256

GLA_QK = GLA_HEADS * GLA_DK
GLA_WIDTH = GLA_HEADS * GLA_DV
DSA_WIDTH = DSA_HEADS * DSA_HD
IDX_WIDTH = IDX_HEADS * IDX_HD

LANES = 128
SUBLANES = 8
ROW_TILE = 384
QBLK = 128
KSUB = 128
KTILE = 512
ATILE = KTILE // 2
N_GROUPS = 2 * KSUB
TOP_R = 6
INTERPOLATED_PROBES = 8
UNCHECKED_HALVINGS = 4
GLA_STEP = 128
GLA_SUB = 16
ROW_PAD = math.lcm(ROW_TILE, KTILE)
PAIR_ROWS = 2 * DSA_HD + 16
VMEM_LIMIT = 60 * 1024 * 1024
NEG_BIG = -1e30

MISC_IK = 0
MISC_IW = IDX_HD
MISC_GA = IDX_HD + IDX_HEADS

F32 = jnp.float32
BF16 = jnp.bfloat16


def _dot(a, b):
    return jnp.dot(a, b, preferred_element_type=F32)


def _rmsnorm(x, g):
    return x * lax.rsqrt(jnp.mean(x * x, axis=-1, keepdims=True) + EPS) * g


def _const_spec(shape):
    return pl.BlockSpec(shape, lambda *_: (0,) * len(shape), pipeline_mode=pl.Buffered(1))


def _swiglu_half(x, norm_ref, wg_ref, wu_ref, wd_ref, ff_chunk):
    hn = _rmsnorm(x, norm_ref[...]).astype(BF16)
    d_ff = wg_ref.shape[1]
    acc = jnp.zeros(x.shape, F32)
    for c in range(d_ff // ff_chunk):
        sl = slice(c * ff_chunk, (c + 1) * ff_chunk)
        g = _dot(hn, wg_ref[:, sl])
        u = _dot(hn, wu_ref[:, sl])
        a = (g * jax.nn.sigmoid(g) * u).astype(BF16)
        acc = acc + _dot(a, wd_ref[sl, :])
    return x + 0.5 * acc


def _ffn1_kernel(h_ref, meta_ref, norm_ref, wg_ref, wu_ref, wd_ref, o_ref, *, ff_chunk):
    x = h_ref[...]
    rows = lax.broadcasted_iota(jnp.int32, (x.shape[0], 1), 0)
    meta = jnp.concatenate([meta_ref[...], jnp.zeros((x.shape[0] - N_META, x.shape[1]), x.dtype)], axis=0)
    x = jnp.where(jnp.logical_and(pl.program_id(0) == 0, rows < N_META), meta, x)
    o_ref[...] = _swiglu_half(x, norm_ref, wg_ref, wu_ref, wd_ref, ff_chunk)


def _ffn_chunk(d_ff):
    if d_ff <= 4096:
        return d_ff
    for c in (2048, 1024, 512, 256):
        if d_ff % c == 0:
            return c
    return d_ff


def _ffn1(h, meta, norm, wg, wu, wd):
    lp, d = h.shape
    d_ff = wg.shape[1]
    row = pl.BlockSpec((ROW_TILE, d), lambda i: (i, 0))
    return pl.pallas_call(
        functools.partial(_ffn1_kernel, ff_chunk=_ffn_chunk(d_ff)),
        grid=(lp // ROW_TILE,),
        in_specs=[row, _const_spec((N_META, d)), _const_spec((1, d)), _const_spec((d, d_ff)),
                  _const_spec((d, d_ff)), _const_spec((d_ff, d))],
        out_specs=row,
        out_shape=jax.ShapeDtypeStruct((lp, d), F32),
        compiler_params=pltpu.CompilerParams(dimension_semantics=("parallel",),
                                             vmem_limit_bytes=VMEM_LIMIT),
        name="ffn1",
    )(h, meta, norm, wg, wu, wd)


def _rope(x, cos, sin_signed, first_half):
    parts = []
    for c in range(x.shape[1] // LANES):
        xc = x[:, c * LANES:(c + 1) * LANES]
        rot = jnp.where(first_half, pltpu.roll(xc, LANES - 32, 1), pltpu.roll(xc, 32, 1))
        parts.append(xc * cos + rot * sin_signed)
    return parts[0] if len(parts) == 1 else jnp.concatenate(parts, axis=1)


def _log_sigmoid(x):
    return jnp.minimum(x, 0.0) - jnp.log(1.0 + jnp.exp(-jnp.abs(x)))


def _in_proj_kernel(h_ref, norm_ref, w_ref, wa2_ref, ba_ref, cos_ref, sin_ref, tcos_ref, tsin_ref,
                    gqk_ref, gv_ref, gr_ref, gg_ref, dqt_ref, dk_ref, dvt_ref, iqt_ref, ik_ref, iwt_ref):
    hn = _rmsnorm(h_ref[...], norm_ref[...]).astype(BF16)
    spread = functools.partial(jnp.dot, preferred_element_type=F32, precision=lax.Precision.HIGHEST)
    cos = spread(cos_ref[...], tcos_ref[...])
    sin_s = spread(sin_ref[...], tsin_ref[...])
    lane = lax.broadcasted_iota(jnp.int32, cos.shape, 1)
    first_half = (lane % DSA_HD) < (DSA_HD // 2)

    def proj(c0, width):
        return _dot(hn, w_ref[:, c0:c0 + width])

    c = 0
    gqk_ref[...] = proj(c, 2 * GLA_QK); c += 2 * GLA_QK
    gv_ref[...] = proj(c, GLA_WIDTH); c += GLA_WIDTH
    gr_ref[...] = proj(c, GLA_WIDTH); c += GLA_WIDTH
    dqt_ref[...] = (_rope(proj(c, DSA_WIDTH), cos, sin_s, first_half) * (DSA_HD ** -0.5)).T.astype(BF16)
    c += DSA_WIDTH
    dk_ref[...] = _rope(proj(c, DSA_WIDTH), cos, sin_s, first_half).astype(BF16); c += DSA_WIDTH
    dvt_ref[...] = proj(c, DSA_WIDTH).T.astype(BF16); c += DSA_WIDTH
    iqt_ref[...] = (_rope(proj(c, IDX_WIDTH), cos, sin_s, first_half) * (IDX_HD ** -0.5)).T.astype(BF16)
    c += IDX_WIDTH
    misc = proj(c, LANES)
    ik_ref[...] = _rope(misc, cos, sin_s, first_half)[:, MISC_IK:MISC_IK + IDX_HD].astype(BF16)
    iwt_ref[...] = (misc * (IDX_HEADS ** -0.5)).T
    pre = _dot(misc.astype(BF16), wa2_ref[...]) + ba_ref[...]
    gg_ref[...] = _log_sigmoid(pre) * (1.0 / GLA_GATE_NORM)


def _rope_spread():
    half = DSA_HD // 2
    lane = np.arange(LANES)
    hit = (lane[None, :] % half) == np.arange(half)[:, None]
    sign = np.where((lane % DSA_HD) < half, -1.0, 1.0)[None, :]
    return jnp.asarray(hit, F32), jnp.asarray(hit * sign, F32)


def _in_proj(h, norm, w_all, wa2p, ba, cos, sin):
    lp, d = h.shape
    ncols = w_all.shape[1]
    half = DSA_HD // 2
    tcos, tsin = _rope_spread()

    def row(width):
        return pl.BlockSpec((ROW_TILE, width), lambda i: (i, 0))

    def col(height):
        return pl.BlockSpec((height, ROW_TILE), lambda i: (0, i))

    outs = [
        (row(2 * GLA_QK), (lp, 2 * GLA_QK), F32), (row(GLA_WIDTH), (lp, GLA_WIDTH), F32),
        (row(GLA_WIDTH), (lp, GLA_WIDTH), F32), (row(GLA_QK), (lp, GLA_QK), F32),
        (col(DSA_WIDTH), (DSA_WIDTH, lp), BF16), (row(DSA_WIDTH), (lp, DSA_WIDTH), BF16),
        (col(DSA_WIDTH), (DSA_WIDTH, lp), BF16), (col(IDX_WIDTH), (IDX_WIDTH, lp), BF16),
        (row(IDX_HD), (lp, IDX_HD), BF16), (col(LANES), (LANES, lp), F32),
    ]
    return pl.pallas_call(
        _in_proj_kernel,
        grid=(lp // ROW_TILE,),
        in_specs=[row(d), _const_spec((1, d)), _const_spec((d, ncols)), _const_spec((LANES, GLA_QK)),
                  _const_spec((1, GLA_QK)), row(half), row(half), _const_spec((half, LANES)),
                  _const_spec((half, LANES))],
        out_specs=[spec for spec, _, _ in outs],
        out_shape=[jax.ShapeDtypeStruct(shape, dt) for _, shape, dt in outs],
        compiler_params=pltpu.CompilerParams(dimension_semantics=("parallel",),
                                             vmem_limit_bytes=VMEM_LIMIT),
        name="in_proj",
    )(h, norm, w_all, wa2p, ba, cos, sin, tcos, tsin)


def _gla_kernel(qk_ref, v_ref, r_ref, g_ref, hnorm_ref, esel_ref, o_ref, s_ref):
    n_sub = GLA_STEP // GLA_SUB

    @pl.when(pl.program_id(0) == 0)
    def _():
        s_ref[...] = jnp.zeros(s_ref.shape, F32)

    q = qk_ref[:, :GLA_QK] * (GLA_DK ** -0.5)
    k = qk_ref[:, GLA_QK:]
    v = v_ref[...]
    g = g_ref[...]

    ri = lax.broadcasted_iota(jnp.int32, (GLA_STEP, GLA_STEP), 0)
    ci = lax.broadcasted_iota(jnp.int32, (GLA_STEP, GLA_STEP), 1)
    tri = ((ri // GLA_SUB == ci // GLA_SUB) & (ci <= ri)).astype(F32)
    b = jnp.dot(tri, g, preferred_element_type=F32, precision=lax.Precision.HIGHEST)

    b3 = b.reshape(n_sub, GLA_SUB, GLA_QK)
    k3 = k.reshape(n_sub, GLA_SUB, GLA_QK)
    b_last = jnp.broadcast_to(b3[:, GLA_SUB - 1:GLA_SUB, :], b3.shape).reshape(GLA_STEP, GLA_QK)

    pos = lax.broadcasted_iota(jnp.int32, (GLA_STEP, GLA_QK), 0) % GLA_SUB
    t_cols = []
    for j in range(GLA_SUB):
        kj = jnp.broadcast_to(k3[:, j:j + 1, :], k3.shape).reshape(GLA_STEP, GLA_QK)
        bj = jnp.broadcast_to(b3[:, j:j + 1, :], b3.shape).reshape(GLA_STEP, GLA_QK)
        e = jnp.where(pos >= j, b - bj, -jnp.inf)
        t_cols.append((q * kj * jnp.exp(e)).astype(BF16))
    a = _dot(jnp.concatenate(t_cols, axis=1), esel_ref[...]).astype(BF16)
    head_of_lane = lax.broadcasted_iota(jnp.int32, (GLA_SUB, GLA_WIDTH), 1) // GLA_DV
    o_intra = []
    for c in range(n_sub):
        rows = slice(c * GLA_SUB, (c + 1) * GLA_SUB)
        v_c = v[rows, :]
        v_heads = jnp.concatenate([jnp.where(head_of_lane == h, v_c, 0.0) for h in range(GLA_HEADS)]
                                  + [jnp.zeros((LANES - GLA_HEADS * GLA_SUB, GLA_WIDTH), F32)], axis=0)
        o_intra.append(_dot(a[rows, :], v_heads.astype(BF16)))
    o = jnp.concatenate(o_intra, axis=0)

    qe = q * jnp.exp(b)
    kd = k * jnp.exp(b_last - b)
    kd_t = kd.T
    dec_t = jnp.exp(b).T
    lane_head = lax.broadcasted_iota(jnp.int32, (GLA_SUB, GLA_QK), 1) // GLA_DK
    s = s_ref[...]
    o_inter = []
    for c in range(n_sub):
        rows = slice(c * GLA_SUB, (c + 1) * GLA_SUB)
        qe_c = qe[rows, :]
        q_stack = jnp.concatenate(
            [jnp.where(lane_head == h, qe_c, 0.0) for h in range(GLA_HEADS)], axis=0).astype(BF16)
        r_stack = _dot(q_stack, s.astype(BF16))
        o_inter.append(jnp.concatenate(
            [r_stack[h * GLA_SUB:(h + 1) * GLA_SUB, :] for h in range(GLA_HEADS)], axis=1))
        kd_c = kd_t[:, rows].astype(BF16)
        v_c = v[rows, :].astype(BF16)
        u_c = jnp.concatenate(
            [_dot(kd_c[h * GLA_DK:(h + 1) * GLA_DK, :], v_c[:, h * GLA_DV:(h + 1) * GLA_DV])
             for h in range(GLA_HEADS)], axis=0)
        last = c * GLA_SUB + GLA_SUB - 1
        s = dec_t[:, last:last + 1] * s + u_c
    s_ref[...] = s
    o = o + jnp.concatenate(o_inter, axis=0)

    r = r_ref[...]
    gate = r * jax.nn.sigmoid(r)
    hn = hnorm_ref[...]
    outs = []
    for h in range(GLA_HEADS):
        oh = o[:, h * GLA_DV:(h + 1) * GLA_DV]
        outs.append(_rmsnorm(oh, hn))
    o_ref[...] = (jnp.concatenate(outs, axis=1) * gate).astype(o_ref.dtype)


def _gla_head_selector():
    r = np.arange(GLA_SUB * GLA_QK)
    j, h = r // GLA_QK, (r % GLA_QK) // GLA_DK
    sel = np.zeros((GLA_SUB * GLA_QK, LANES), np.float32)
    sel[r, h * GLA_SUB + j] = 1.0
    return jnp.asarray(sel, BF16)


def _gla(gqk, gv, gr, gg, hnorm):
    lp = gqk.shape[0]
    assert GLA_HEADS * GLA_SUB <= LANES

    def row(width):
        return pl.BlockSpec((GLA_STEP, width), lambda i: (i, 0))

    return pl.pallas_call(
        _gla_kernel,
        grid=(lp // GLA_STEP,),
        in_specs=[row(2 * GLA_QK), row(GLA_WIDTH), row(GLA_WIDTH), row(GLA_QK), _const_spec((1, GLA_DV)),
                  _const_spec((GLA_SUB * GLA_QK, LANES))],
        out_specs=row(GLA_WIDTH),
        out_shape=jax.ShapeDtypeStruct((lp, GLA_WIDTH), BF16),
        scratch_shapes=[pltpu.VMEM((GLA_QK, GLA_DV), F32)],
        compiler_params=pltpu.CompilerParams(dimension_semantics=("arbitrary",),
                                             vmem_limit_bytes=VMEM_LIMIT),
        name="gla",
    )(gqk, gv, gr, gg, hnorm, _gla_head_selector())


def _key_to_float(u):
    bits = jnp.where(u >= 0, u, u ^ jnp.int32(0x7FFFFFFF))
    return lax.bitcast_convert_type(bits, F32)


def _float_to_key(f):
    bits = lax.bitcast_convert_type(f, jnp.int32)
    return jnp.where(bits >= 0, bits, bits ^ jnp.int32(0x7FFFFFFF))


_KEY_NEG_INF = np.int32(np.uint32(0xFF800000) ^ np.uint32(0x7FFFFFFF))


def _tile_loop(n, body, carry, unroll):
    start = 0
    for width in ((unroll,) if isinstance(unroll, int) else unroll):
        trips = (n - start) // width

        def group(u, c, width=width, start=start):
            for r in range(width):
                c = body(start + u * width + r, c)
            return c

        carry = lax.fori_loop(0, trips, group, carry)
        start = start + trips * width
    return lax.fori_loop(start, n, body, carry)


def _fold_rows(x, op):
    parts = [x[r:r + SUBLANES, :] for r in range(0, x.shape[0], SUBLANES)]
    while len(parts) > 1:
        parts = [op(parts[a], parts[a + 1]) if a + 1 < len(parts) else parts[a] for a in range(0, len(parts), 2)]
    return parts[0]


def _dsa_block(qt_ref, iqt_ref, iwt_ref, k_ref, vt_ref, ik_ref, o_ref, sc_ref, bias_ref, acc_ref, rhs_ref,
               sa_ref, sb_ref, top_ref, k_sel):
    i = pl.program_id(0)
    n_tiles = (i * QBLK + QBLK + KTILE - 1) // KTILE
    last = n_tiles - 1
    q_pos = i * QBLK + lax.broadcasted_iota(jnp.int32, (KSUB, QBLK), 1)
    key_row = lax.broadcasted_iota(jnp.int32, (KSUB, QBLK), 0)

    def visible(ks):
        return (ks + key_row) <= q_pos

    def peeled(body, carry):
        carry = _tile_loop(last, functools.partial(body, masked=False), carry, (4, 2))
        return body(last, carry, masked=True)

    iqt = iqt_ref[...]
    iq_all = jnp.concatenate([iqt[h * IDX_HD:(h + 1) * IDX_HD, :] for h in range(IDX_HEADS)], axis=1)
    iw = iwt_ref[MISC_IW:MISC_IW + IDX_HEADS, :]
    top_ref[...] = jnp.full(top_ref.shape, -jnp.inf, F32)

    def score_sub(ks, sub, masked):
        lg = _dot(ik_ref[pl.ds(ks, KSUB), :], iq_all)
        sc = iw[0:1, :] * jnp.maximum(lg[:, 0:QBLK], 0.0)
        for h in range(1, IDX_HEADS):
            sc = sc + iw[h:h + 1, :] * jnp.maximum(lg[:, h * QBLK:(h + 1) * QBLK], 0.0)
        if masked:
            sc = jnp.where(visible(ks), sc, -jnp.inf)
        sc_ref[pl.ds(ks, KSUB), :] = sc
        rest = sc
        for r in range(TOP_R):
            rows = slice(r * N_GROUPS + (sub % 2) * KSUB, r * N_GROUPS + (sub % 2 + 1) * KSUB)
            held = top_ref[rows, :]
            top_ref[rows, :] = jnp.maximum(held, rest)
            if r + 1 < TOP_R:
                rest = jnp.minimum(held, rest)

    def score_tile(t, carry, masked):
        for sub in range(KTILE // KSUB):
            ks = pl.multiple_of(t * KTILE + sub * KSUB, KSUB)
            if masked and sub > 0:
                in_reach = ks < i * QBLK + QBLK
                pl.when(in_reach)(functools.partial(score_sub, ks, sub, True))

                @pl.when(jnp.logical_not(in_reach))
                def _():
                    sc_ref[pl.ds(ks, KSUB), :] = jnp.full((KSUB, QBLK), -jnp.inf, F32)
            else:
                score_sub(ks, sub, masked)
        return carry

    peeled(score_tile, 0)

    kk = float(k_sel)

    def lane_sum(x):
        return jnp.sum(_fold_rows(x, jnp.add), axis=0, keepdims=True)

    def count_scores(pred):
        def body(t, acc):
            ks = pl.multiple_of(t * KTILE, KTILE)
            return acc + _fold_rows(jnp.where(pred(sc_ref[pl.ds(ks, KTILE), :]), 1.0, 0.0), jnp.add)
        return jnp.sum(_tile_loop(n_tiles, body, jnp.zeros((SUBLANES, QBLK), F32), 4), axis=0, keepdims=True)

    def count_ge(cand):
        return lane_sum(jnp.where(top_ref[...] >= cand, 1.0, 0.0))

    best = top_ref[0:N_GROUPS, :]
    lo = jnp.maximum(_float_to_key(jnp.min(_fold_rows(best, jnp.minimum), axis=0, keepdims=True)), _KEY_NEG_INF)
    hi = _float_to_key(jnp.max(_fold_rows(best, jnp.maximum), axis=0, keepdims=True))
    n_lo = count_ge(_key_to_float(lo))

    def settled(lo, hi, n_lo):
        return jnp.logical_or(n_lo <= kk, lo >= hi)

    def any_open(lo, hi, n_lo):
        return jnp.max(jnp.where(settled(lo, hi, n_lo), 0.0, 1.0))

    def midpoint(lo, hi):
        return (lo | hi) - ((lo ^ hi) >> 1)

    def probe_at(key, lo, hi, n_lo):
        open_ = jnp.logical_not(settled(lo, hi, n_lo))
        cnt = count_ge(_key_to_float(key))
        up = jnp.logical_and(open_, cnt >= kk)
        down = jnp.logical_and(open_, cnt < kk)
        return jnp.where(up, key, lo), jnp.where(down, key - 1, hi), jnp.where(up, cnt, n_lo), cnt, up, down

    def halve(lo, hi, n_lo):
        return probe_at(midpoint(lo, hi), lo, hi, n_lo)[:3]

    log_k = float(np.log(k_sel))

    def interpolate(st):
        lo, hi, n_lo, g_lo, g_hi, side = st
        f_lo, f_hi = _key_to_float(lo), _key_to_float(hi)
        guess = f_lo + (g_lo - log_k) / jnp.maximum(g_lo - g_hi, 1e-9) * (f_hi - f_lo)
        usable = jnp.logical_and(lo > _KEY_NEG_INF, jnp.abs(guess) < jnp.inf)
        key = jnp.where(usable, jnp.clip(_float_to_key(guess), lo + 1, hi), midpoint(lo, hi))
        lo, hi, n_lo, cnt, up, down = probe_at(key, lo, hi, n_lo)
        g_cnt = jnp.log(jnp.maximum(cnt, 0.5))
        g_hi = jnp.where(down, g_cnt, jnp.where(jnp.logical_and(up, side > 0.0), 0.5 * (g_hi + log_k), g_hi))
        g_lo = jnp.where(up, g_cnt, jnp.where(jnp.logical_and(down, side < 0.0), 0.5 * (g_lo + log_k), g_lo))
        side = jnp.where(up, 1.0, jnp.where(down, -1.0, side))
        return lo, hi, n_lo, g_lo, g_hi, side

    def checked_halve(st):
        lo, hi, n_lo = halve(*halve(*st[:3]))
        return lo, hi, n_lo, any_open(lo, hi, n_lo), st[4] + 1

    lo, hi, n_lo, _, _, _ = lax.fori_loop(
        0, INTERPOLATED_PROBES, lambda _, st: interpolate(st),
        (lo, hi, n_lo, jnp.log(jnp.maximum(n_lo, 0.5)), jnp.full((1, QBLK), float(np.log(0.5)), F32),
         jnp.zeros((1, QBLK), F32)))
    lo, hi, n_lo = lax.fori_loop(0, UNCHECKED_HALVINGS, lambda _, st: halve(*st), (lo, hi, n_lo))
    lo, hi, n_lo, _, _ = lax.while_loop(lambda st: jnp.logical_and(st[3] > 0.0, st[4] < 17), checked_halve,
                                        (lo, hi, n_lo, any_open(lo, hi, n_lo), jnp.int32(0)))
    tops = top_ref[...]
    thr = jnp.min(_fold_rows(jnp.where(tops >= _key_to_float(lo), tops, jnp.inf), jnp.minimum), axis=0,
                  keepdims=True)
    n_ge = count_scores(lambda x: x >= thr)

    def next_above(v):
        def body(t, carry):
            low, cnt = carry
            ks = pl.multiple_of(t * KTILE, KTILE)
            x = sc_ref[pl.ds(ks, KTILE), :]
            above = x > v
            return (jnp.minimum(low, _fold_rows(jnp.where(above, x, jnp.inf), jnp.minimum)),
                    cnt + _fold_rows(jnp.where(above, 1.0, 0.0), jnp.add))
        low, cnt = _tile_loop(n_tiles, body, (jnp.full((SUBLANES, QBLK), jnp.inf, F32),
                                              jnp.zeros((SUBLANES, QBLK), F32)), 4)
        return jnp.min(low, axis=0, keepdims=True), jnp.sum(cnt, axis=0, keepdims=True)

    def any_climbing(n_ge, stuck):
        return jnp.max(jnp.where(jnp.logical_or(n_ge <= kk, stuck > 0.0), 0.0, 1.0))

    def climb(st):
        thr, n_ge, stuck, _, it = st
        nxt, n_nxt = next_above(thr)
        go = jnp.logical_and(jnp.logical_and(n_ge > kk, stuck == 0.0), n_nxt >= kk)
        stuck = jnp.where(jnp.logical_and(n_ge > kk, n_nxt < kk), 1.0, stuck)
        thr, n_ge = jnp.where(go, nxt, thr), jnp.where(go, n_nxt, n_ge)
        return thr, n_ge, stuck, any_climbing(n_ge, stuck), it + 1

    stuck = jnp.zeros((1, QBLK), F32)
    thr, n_ge, _, _, _ = lax.while_loop(
        lambda st: jnp.logical_and(st[3] > 0.0, st[4] <= n_tiles * KTILE), climb,
        (thr, n_ge, stuck, any_climbing(n_ge, stuck), jnp.int32(0)))
    has_ties = jnp.max(n_ge) > kk

    def bias_fast(t, carry, masked):
        for sub in range(KTILE // KSUB):
            ks = pl.multiple_of(t * KTILE + sub * KSUB, KSUB)
            keep = sc_ref[pl.ds(ks, KSUB), :] >= thr
            if masked:
                keep = jnp.logical_and(keep, visible(ks))
            bias_ref[pl.ds(ks, KSUB), :] = jnp.where(keep, 0.0, NEG_BIG).astype(BF16)
        return carry

    def bias_ties(t, run, masked, budget, tri):
        for sub in range(KTILE // KSUB):
            ks = pl.multiple_of(t * KTILE + sub * KSUB, KSUB)
            x = sc_ref[pl.ds(ks, KSUB), :]
            above = x > thr
            tie = x == thr
            tie_f = jnp.where(tie, 1.0, 0.0)
            rank = run + _dot(tri, tie_f.astype(BF16))
            keep = jnp.logical_or(above, jnp.logical_and(tie, rank <= budget))
            if masked:
                keep = jnp.logical_and(keep, visible(ks))
            bias_ref[pl.ds(ks, KSUB), :] = jnp.where(keep, 0.0, NEG_BIG).astype(BF16)
            run = run + jnp.sum(tie_f, axis=0, keepdims=True)
        return run

    @pl.when(jnp.logical_not(has_ties))
    def _():
        peeled(bias_fast, 0)

    @pl.when(has_ties)
    def _():
        budget = kk - count_scores(lambda x: x > thr)
        tri = (lax.broadcasted_iota(jnp.int32, (KSUB, KSUB), 1) <= key_row).astype(BF16)
        peeled(functools.partial(bias_ties, budget=budget, tri=tri), jnp.zeros((1, QBLK), F32))

    qt = qt_ref[...]
    eye = (lax.broadcasted_iota(jnp.int32, (QBLK, QBLK), 0)
           == lax.broadcasted_iota(jnp.int32, (QBLK, QBLK), 1)).astype(BF16)
    zero = jnp.zeros((DSA_HD, QBLK), BF16)
    n_pairs = DSA_HEADS // 2
    for c in range(n_pairs):
        qa = qt[(2 * c) * DSA_HD:(2 * c + 1) * DSA_HD, :]
        qb = qt[(2 * c + 1) * DSA_HD:(2 * c + 2) * DSA_HD, :]
        rhs_ref[c] = jnp.concatenate([jnp.concatenate([qa, zero], axis=1),
                                      jnp.concatenate([zero, qb], axis=1),
                                      jnp.concatenate([eye, eye], axis=1)], axis=0)
    acc_ref[...] = jnp.zeros(acc_ref.shape, F32)
    ones_rows = jnp.ones((PAIR_ROWS - 2 * DSA_HD, ATILE), BF16)

    def qk_stage(ks, s_ref):
        bias = bias_ref[pl.ds(ks, ATILE), :]
        tile_max = []
        for c in range(n_pairs):
            lhs = jnp.concatenate([k_ref[pl.ds(ks, ATILE), c * 2 * DSA_HD:(c + 1) * 2 * DSA_HD], bias], axis=1)
            s = _dot(lhs, rhs_ref[c])
            s_ref[c] = s
            tile_max.append(_fold_rows(s, jnp.maximum))
        return tuple(tile_max)

    def pv_stage(ks, s_ref, tile_max, ms):
        new_ms = []
        for c in range(n_pairs):
            m_new = jnp.maximum(ms[c], jnp.max(tile_max[c], axis=0, keepdims=True))
            alpha = jnp.exp(ms[c] - m_new)
            p = jnp.exp(s_ref[c] - m_new).astype(BF16)
            base = c * PAIR_ROWS
            vt_aug = jnp.concatenate([vt_ref[c * 2 * DSA_HD:(c + 1) * 2 * DSA_HD, pl.ds(ks, ATILE)], ones_rows],
                                     axis=0)
            pv = _dot(vt_aug, p)
            for hh in range(2):
                rows = slice(base + hh * DSA_HD, base + (hh + 1) * DSA_HD)
                cols = slice(hh * QBLK, (hh + 1) * QBLK)
                acc_ref[rows, cols] = alpha[:, cols] * acc_ref[rows, cols] + pv[hh * DSA_HD:(hh + 1) * DSA_HD, cols]
            sums = slice(base + 2 * DSA_HD, base + PAIR_ROWS)
            acc_ref[sums, :] = alpha * acc_ref[sums, :] + pv[2 * DSA_HD:, :]
            new_ms.append(m_new)
        return tuple(new_ms)

    def attn_tile(t, carry):
        max_a, ms = carry
        k0 = pl.multiple_of(t * KTILE, KTILE)
        k1 = pl.multiple_of(t * KTILE + ATILE, ATILE)
        k2 = pl.multiple_of(t * KTILE + KTILE, KTILE)
        max_b = qk_stage(k1, sb_ref)
        ms = pv_stage(k0, sa_ref, max_a, ms)
        max_a = qk_stage(k2, sa_ref)
        ms = pv_stage(k1, sb_ref, max_b, ms)
        return max_a, ms

    ms = tuple(jnp.full((1, 2 * QBLK), NEG_BIG, F32) for _ in range(n_pairs))
    max_a, ms = _tile_loop(last, attn_tile, (qk_stage(0, sa_ref), ms), (8, 4, 2))
    k0 = pl.multiple_of(last * KTILE, KTILE)
    k1 = pl.multiple_of(last * KTILE + ATILE, ATILE)
    second_half = i * QBLK + QBLK > last * KTILE + ATILE

    @pl.when(second_half)
    def _():
        max_b = qk_stage(k1, sb_ref)
        pv_stage(k1, sb_ref, max_b, pv_stage(k0, sa_ref, max_a, ms))

    @pl.when(jnp.logical_not(second_half))
    def _():
        pv_stage(k0, sa_ref, max_a, ms)

    outs = []
    for h in range(DSA_HEADS):
        base, hh = (h // 2) * PAIR_ROWS, h % 2
        cols = slice(hh * QBLK, (hh + 1) * QBLK)
        outs.append(acc_ref[base + hh * DSA_HD:base + (hh + 1) * DSA_HD, cols]
                    / acc_ref[base + 2 * DSA_HD:base + 2 * DSA_HD + 1, cols])
    o_ref[...] = jnp.concatenate(outs, axis=0).T.astype(o_ref.dtype)


def _dsa_kernel(qt_ref, iqt_ref, iwt_ref, k_ref, vt_ref, ik_ref, o_ref, sc_ref, bias_ref, acc_ref, rhs_ref,
                sa_ref, sb_ref, top_ref, *, k_sel, l_tok):
    has_tokens = pl.program_id(0) * QBLK < l_tok

    @pl.when(has_tokens)
    def _():
        _dsa_block(qt_ref, iqt_ref, iwt_ref, k_ref, vt_ref, ik_ref, o_ref, sc_ref, bias_ref, acc_ref, rhs_ref,
                   sa_ref, sb_ref, top_ref, k_sel)

    @pl.when(jnp.logical_not(has_tokens))
    def _():
        o_ref[...] = jnp.zeros(o_ref.shape, o_ref.dtype)


def _dsa(dqt, dk, dvt, iqt, ik, iwt, k_sel, l_tok):
    lp = dk.shape[0]
    assert k_sel <= N_GROUPS, "the smallest group maximum must have at least k_sel tops above it"

    def col(height):
        return pl.BlockSpec((height, QBLK), lambda i: (0, i))

    return pl.pallas_call(
        functools.partial(_dsa_kernel, k_sel=k_sel, l_tok=l_tok),
        grid=(lp // QBLK,),
        in_specs=[col(DSA_WIDTH), col(IDX_WIDTH), col(LANES), _const_spec((lp, DSA_WIDTH)),
                  _const_spec((DSA_WIDTH, lp)), _const_spec((lp, IDX_HD))],
        out_specs=pl.BlockSpec((QBLK, DSA_WIDTH), lambda i: (i, 0)),
        out_shape=jax.ShapeDtypeStruct((lp, DSA_WIDTH), BF16),
        scratch_shapes=[pltpu.VMEM((lp, QBLK), F32), pltpu.VMEM((lp, QBLK), BF16),
                        pltpu.VMEM((DSA_HEADS // 2 * PAIR_ROWS, 2 * QBLK), F32),
                        pltpu.VMEM((DSA_HEADS // 2, 2 * LANES, 2 * QBLK), BF16),
                        pltpu.VMEM((DSA_HEADS // 2, ATILE, 2 * QBLK), F32),
                        pltpu.VMEM((DSA_HEADS // 2, ATILE, 2 * QBLK), F32),
                        pltpu.VMEM((TOP_R * N_GROUPS, QBLK), F32)],
        compiler_params=pltpu.CompilerParams(dimension_semantics=("parallel",),
                                             vmem_limit_bytes=VMEM_LIMIT),
        name="dsa",
    )(dqt, iqt, iwt, dk, dvt, ik)


def _out_ffn2_kernel(h_ref, ht_ref, og_ref, ogt_ref, od_ref, odt_ref, wo_ref, norm_ref, wg_ref, wu_ref, wd_ref,
                     fnorm_ref, o_ref, *, ff_chunk):
    def shifted(main_ref, tail_ref):
        return jnp.concatenate([main_ref[N_META:, :], tail_ref[...]], axis=0)

    h2 = (shifted(h_ref, ht_ref) + _dot(shifted(og_ref, ogt_ref), wo_ref[:GLA_WIDTH, :])
          + _dot(shifted(od_ref, odt_ref), wo_ref[GLA_WIDTH:, :]))
    h3 = _swiglu_half(h2, norm_ref, wg_ref, wu_ref, wd_ref, ff_chunk)
    o_ref[...] = _rmsnorm(h3, fnorm_ref[...])


def _out_ffn2(h, o_gla, o_dsa, wo, norm, wg, wu, wd, fnorm, seq):
    lp, d = h.shape
    d_ff = wg.shape[1]
    n_steps = -(-seq // ROW_TILE)
    assert ROW_TILE % N_META == 0 and n_steps * ROW_TILE + N_META <= lp
    tail_step = ROW_TILE // N_META

    def row(width):
        return pl.BlockSpec((ROW_TILE, width), lambda i: (i, 0))

    def tail(width):
        return pl.BlockSpec((N_META, width), lambda i: ((i + 1) * tail_step, 0))

    return pl.pallas_call(
        functools.partial(_out_ffn2_kernel, ff_chunk=_ffn_chunk(d_ff)),
        grid=(n_steps,),
        in_specs=[row(d), tail(d), row(GLA_WIDTH), tail(GLA_WIDTH), row(DSA_WIDTH), tail(DSA_WIDTH),
                  _const_spec((GLA_WIDTH + DSA_WIDTH, d)), _const_spec((1, d)), _const_spec((d, d_ff)),
                  _const_spec((d, d_ff)), _const_spec((d_ff, d)), _const_spec((1, d))],
        out_specs=row(d),
        out_shape=jax.ShapeDtypeStruct((seq, d), F32),
        compiler_params=pltpu.CompilerParams(dimension_semantics=("parallel",),
                                             vmem_limit_bytes=VMEM_LIMIT),
        name="out_ffn2",
    )(h, h, o_gla, o_gla, o_dsa, o_dsa, wo, norm, wg, wu, wd, fnorm)


def _rope_tables(lp):
    pos = jnp.arange(lp, dtype=F32)
    inv_freq = 1.0 / (ROPE_THETA ** (jnp.arange(0, DSA_HD, 2, dtype=F32) / DSA_HD))
    ang = pos[:, None] * inv_freq[None, :]
    return jnp.cos(ang), jnp.sin(ang)


def _layer(h, meta, k_sel, l_tok, ffn1_norm, ffn1_w_gate, ffn1_w_up, ffn1_w_down, mix_norm, w_in, gla_w_a2, gla_b_a,
           gla_head_norm, w_out, ffn2_norm, ffn2_w_gate, ffn2_w_up, ffn2_w_down, final_norm, cos_t, sin_t):
    d = h.shape[1]
    h1 = _ffn1(h, meta, ffn1_norm.reshape(1, d), ffn1_w_gate.astype(BF16), ffn1_w_up.astype(BF16),
               ffn1_w_down.astype(BF16))

    sizes = (GLA_QK, GLA_QK, GLA_WIDTH, GLA_WIDTH, GLA_LOWRANK, DSA_WIDTH, DSA_WIDTH, DSA_WIDTH, IDX_WIDTH,
             IDX_HD, IDX_HEADS)
    offs = np.concatenate([[0], np.cumsum(sizes)])
    w_bf = w_in.astype(BF16)
    w_all = jnp.concatenate([w_bf[:, :offs[4]], w_bf[:, offs[5]:], w_bf[:, offs[4]:offs[5]],
                             jnp.zeros((d, LANES - IDX_HD - IDX_HEADS - GLA_LOWRANK), BF16)], axis=1)
    wa2p = jnp.zeros((LANES, GLA_QK), F32).at[MISC_GA:MISC_GA + GLA_LOWRANK, :].set(gla_w_a2).astype(BF16)

    gqk, gv, gr, gg, dqt, dk, dvt, iqt, ik, iwt = _in_proj(
        h1, mix_norm.reshape(1, d), w_all, wa2p, gla_b_a.reshape(1, GLA_QK), cos_t, sin_t)
    o_gla = _gla(gqk, gv, gr, gg, gla_head_norm.reshape(1, GLA_DV))
    o_dsa = _dsa(dqt, dk, dvt, iqt, ik, iwt, k_sel, l_tok)
    return _out_ffn2(h1, o_gla, o_dsa, w_out.astype(BF16), ffn2_norm.reshape(1, d), ffn2_w_gate.astype(BF16),
                     ffn2_w_up.astype(BF16), ffn2_w_down.astype(BF16), final_norm.reshape(1, d), l_tok - N_META)


def kernel(x, meta_tokens, ffn1_norm, ffn1_w_gate, ffn1_w_up, ffn1_w_down, mix_norm, w_in, gla_w_a2, gla_b_a,
           gla_head_norm, w_out, ffn2_norm, ffn2_w_gate, ffn2_w_up, ffn2_w_down, final_norm):
    batch, seq, d = x.shape
    depth = ffn1_norm.shape[0]
    assert depth == 1, "the final norm is fused into the layer's last kernel"
    k_sel = min(TOPK_MAX, seq // 4)
    l_tok = seq + N_META
    lp = -(-max(l_tok, -(-seq // ROW_TILE) * ROW_TILE + N_META) // ROW_PAD) * ROW_PAD
    cos_t, sin_t = _rope_tables(lp)
    outs = []
    for bi in range(batch):
        h = jnp.pad(x[bi], ((N_META, lp - l_tok), (0, 0)))
        outs.append(_layer(h, meta_tokens.astype(x.dtype), k_sel, l_tok, ffn1_norm[0], ffn1_w_gate[0], ffn1_w_up[0], ffn1_w_down[0],
                           mix_norm[0], w_in[0], gla_w_a2[0], gla_b_a[0], gla_head_norm[0], w_out[0], ffn2_norm[0],
                           ffn2_w_gate[0], ffn2_w_up[0], ffn2_w_down[0], final_norm, cos_t, sin_t))
    return jnp.stack(outs, axis=0)
```
